```python
import math
import jax, jax.numpy as jnp
from jax import lax
import numpy as np

D_MODEL = 1024
BATCH = 16
SEQ = 256
DEPTH = 2
DEC_BATCH = 2
DEC_SEQ = 2048
PAST_LEN = 256

GRID_W = 64
HEAD_DIM = 64
N_MIXERS = 4
GROUP_W = D_MODEL // N_MIXERS
N_HG = GROUP_W // HEAD_DIM
D_FF = 4 * D_MODEL
CHUNK = 64
CONV_K = 5
RWKV_W_RANK = 64
RWKV_A_RANK = 64
RWKV_G_RANK = 128
ROPE_BASE = 10000.0
LN_EPS = 1e-5
DEEPNORM_ALPHA = (2 * DEPTH) ** 0.25
DEEPNORM_BETA = (8 * DEPTH) ** -0.25

A_SIZES = (GROUP_W, GROUP_W, GROUP_W, GROUP_W, 2 * N_HG, 2 * N_HG)
B_SIZES = (3 * GROUP_W, GROUP_W, 2 * N_HG, 2 * N_HG)
C_SIZES = (GROUP_W, GROUP_W, GROUP_W, GROUP_W)
D_SIZES = (GROUP_W, GROUP_W, GROUP_W, 2 * RWKV_W_RANK, 2 * RWKV_A_RANK, RWKV_G_RANK)
GROUP_COLS = (sum(A_SIZES), sum(B_SIZES), sum(C_SIZES), sum(D_SIZES))
P_IN = sum(GROUP_COLS)

kernel_name = 'hybrid_bidir_recurrent_diffusion_step'


def split_cols(x, sizes):
    offsets = [sum(sizes[:i + 1]) for i in range(len(sizes) - 1)]
    return jnp.split(x, offsets, axis=-1)


def layer_norm(x, gain=None, bias=None):
    xf = x.astype(jnp.float32)
    mu = jnp.mean(xf, axis=-1, keepdims=True)
    var = jnp.mean(jnp.square(xf - mu), axis=-1, keepdims=True)
    y = (xf - mu) * lax.rsqrt(var + LN_EPS)
    if gain is not None:
        y = y * gain + bias
    return y.astype(x.dtype)


def head_norm(y, gain, center):
    if center:
        y = y - jnp.mean(y, axis=-1, keepdims=True)
    y = y * lax.rsqrt(jnp.mean(jnp.square(y), axis=-1, keepdims=True) + LN_EPS)
    return y * gain.reshape(N_HG, HEAD_DIM)


def l2norm(x):
    return x * lax.rsqrt(jnp.sum(jnp.square(x), axis=-1, keepdims=True) + 1e-6)


def heads(x):
    return x.reshape(x.shape[:2] + (N_HG, HEAD_DIM))


def bidir(x):
    x = jnp.swapaxes(x, 1, 2)
    return jnp.stack([x, jnp.flip(x, axis=2)], axis=0)


def merge_dirs(y):
    return jnp.swapaxes(y[0] + jnp.flip(y[1], axis=2), 1, 2)


def dir_gate(g):
    g = jnp.transpose(g.reshape(g.shape[:2] + (2, N_HG)), (2, 0, 3, 1))
    return jnp.stack([g[0], jnp.flip(g[1], axis=2)], axis=0)


def dir_vec(x):
    x = jnp.transpose(x, (2, 0, 3, 1, 4))
    return jnp.stack([x[0], jnp.flip(x[1], axis=2)], axis=0)


def to_chunks(x):
    s = x.shape
    x = x.reshape(s[:3] + (s[3] // CHUNK, CHUNK) + s[4:])
    return jnp.moveaxis(x, 3, 0)


def from_chunks(y):
    y = jnp.moveaxis(y, 0, 3)
    s = y.shape
    return y.reshape(s[:3] + (s[3] * s[4],) + s[5:])


def chunk_masks():
    idx = jnp.arange(CHUNK)
    return idx[:, None] >= idx[None, :], idx[:, None] > idx[None, :]


def short_conv(x, w):
    return lax.conv_general_dilated(x, w[:, None, :].astype(x.dtype), window_strides=(1,),
                                    padding=[(CONV_K // 2, CONV_K // 2)],
                                    dimension_numbers=('NWC', 'WIO', 'NWC'),
                                    feature_group_count=x.shape[-1])


def centred_shift(x):
    pad = jnp.pad(x, ((0, 0), (1, 1), (0, 0)))
    return 0.5 * (pad[:, :-2] + pad[:, 2:])


def grid_rope(x, pos):
    half = HEAD_DIM // 2
    nf = half // 2
    inv = ROPE_BASE ** (-jnp.arange(nf, dtype=jnp.float32) / nf)

    def rot(xp, p):
        ang = p.astype(jnp.float32)[:, None] * inv[None, :]
        cos, sin = jnp.cos(ang)[:, None, :], jnp.sin(ang)[:, None, :]
        x1, x2 = xp[..., :nf], xp[..., nf:]
        return jnp.concatenate([x1 * cos - x2 * sin, x1 * sin + x2 * cos], axis=-1)

    return jnp.concatenate([rot(x[..., :half], pos[0]), rot(x[..., half:], pos[1])], axis=-1)


def mlstm_scan(q, k, v, li, lf, c0, n0, m0):
    incl, _ = chunk_masks()

    def step(carry, xs):
        cm, nv, m = carry
        qc, kc, vc, ic, fc = xs
        b = jnp.cumsum(fc, axis=-1)
        bl = b[..., -1]
        logd = jnp.where(incl, b[..., :, None] - b[..., None, :] + ic[..., None, :], -jnp.inf)
        inter = b + m[..., None]
        mt = jnp.maximum(inter, jnp.max(logd, axis=-1))
        s = jnp.einsum('zbhld,zbhsd->zbhls', qc, kc) * jnp.exp(logd - mt[..., None])
        ew = jnp.exp(inter - mt)
        num = jnp.einsum('zbhls,zbhse->zbhle', s, vc) + jnp.einsum('zbhld,zbhde->zbhle', qc, cm) * ew[..., None]
        den = jnp.sum(s, axis=-1) + jnp.einsum('zbhld,zbhd->zbhl', qc, nv) * ew
        h = num / jnp.maximum(jnp.abs(den), jnp.exp(-mt))[..., None]
        wk = bl[..., None] - b + ic
        m_new = jnp.maximum(bl + m, jnp.max(wk, axis=-1))
        sk = jnp.exp(wk - m_new[..., None])
        sp = jnp.exp(bl + m - m_new)
        c_new = cm * sp[..., None, None] + jnp.einsum('zbhs,zbhsd,zbhse->zbhde', sk, kc, vc)
        n_new = nv * sp[..., None] + jnp.einsum('zbhs,zbhsd->zbhd', sk, kc)
        return (c_new, n_new, m_new), h

    xs = (to_chunks(q), to_chunks(k), to_chunks(v), to_chunks(li), to_chunks(lf))
    (c1, n1, m1), h = lax.scan(step, (c0, n0, m0), xs)
    return from_chunks(h), c1, n1, m1


def delta_scan(q, k, v, beta, g, s0):
    incl, strict = chunk_masks()
    eye = jnp.eye(CHUNK, dtype=jnp.float32)

    def step(s, xs):
        qc, kc, vc, bc, gc = xs
        gcum = jnp.cumsum(gc, axis=-1)
        decay = jnp.exp(jnp.where(incl, gcum[..., :, None] - gcum[..., None, :], -jnp.inf))
        a = jnp.where(strict, jnp.einsum('zbhld,zbhsd->zbhls', kc, kc) * decay, 0.0) * bc[..., :, None]
        rhs = jnp.concatenate([vc * bc[..., None], kc * (bc * jnp.exp(gcum))[..., None]], axis=-1)
        x = lax.linalg.triangular_solve(eye + a, rhs, left_side=True, lower=True, unit_diagonal=True)
        u, w = x[..., :HEAD_DIM], x[..., HEAD_DIM:]
        v_new = u - jnp.einsum('zbhld,zbhde->zbhle', w, s)
        qk = jnp.einsum('zbhld,zbhsd->zbhls', qc, kc) * decay
        o = jnp.einsum('zbhld,zbhde->zbhle', qc * jnp.exp(gcum)[..., None], s) + jnp.einsum('zbhls,zbhse->zbhle', qk, v_new)
        gl = gcum[..., -1]
        s_new = s * jnp.exp(gl)[..., None, None] + jnp.einsum('zbhs,zbhsd,zbhse->zbhde', jnp.exp(gl[..., None] - gcum), kc, v_new)
        return s_new, o

    xs = (to_chunks(q), to_chunks(k), to_chunks(v), to_chunks(beta), to_chunks(g))
    s1, o = lax.scan(step, s0, xs)
    return from_chunks(o), s1


def retention_scan(q, k, v, log_gamma, s0):
    incl, _ = chunk_masks()
    idx = jnp.arange(CHUNK, dtype=jnp.float32)
    lg = log_gamma.astype(jnp.float32)[:, None, :, None]
    dmat = jnp.exp(jnp.where(incl, (idx[:, None] - idx[None, :]) * lg[..., None], -jnp.inf))
    xi = jnp.exp((idx + 1.0) * lg)
    zeta = jnp.exp((CHUNK - 1.0 - idx) * lg)
    gl = jnp.exp(CHUNK * lg[..., 0])

    def step(s, xs):
        qc, kc, vc = xs
        sc = jnp.einsum('zbhld,zbhsd->zbhls', qc, kc) * dmat
        o = jnp.einsum('zbhls,zbhse->zbhle', sc, vc) + jnp.einsum('zbhld,zbhde->zbhle', qc, s) * xi[..., None]
        s_new = s * gl[..., None, None] + jnp.einsum('zbhsd,zbhse->zbhde', kc * zeta[..., None], vc)
        return s_new, o

    s1, o = lax.scan(step, s0, (to_chunks(q), to_chunks(k), to_chunks(v)))
    return from_chunks(o), s1


def rwkv_scan(r, kh, kt, v, w, a, s0):
    def step(s, xs):
        rt, kht, ktt, vt, wt, at = xs
        sk = jnp.einsum('zbhij,zbhj->zbhi', s, kht)
        s = s * wt[..., None, :] - sk[..., :, None] * (at * kht)[..., None, :] + vt[..., :, None] * ktt[..., None, :]
        return s, jnp.einsum('zbhij,zbhj->zbhi', s, rt)

    xs = (jnp.moveaxis(r, 3, 0), jnp.moveaxis(kh, 3, 0), jnp.moveaxis(kt, 3, 0),
          jnp.moveaxis(v, 3, 0), jnp.moveaxis(w, 3, 0), jnp.moveaxis(a, 3, 0))
    s1, y = lax.scan(step, s0, xs)
    return jnp.moveaxis(y, 0, 3), s1


def zero_states(bsz):
    f = jnp.float32
    mat = (2, bsz, N_HG, HEAD_DIM, HEAD_DIM)
    return (jnp.zeros(mat, f), jnp.zeros((2, bsz, N_HG, HEAD_DIM), f), jnp.zeros((2, bsz, N_HG), f),
            jnp.zeros(mat, f), jnp.zeros(mat, f), jnp.zeros(mat, f))


def mixing_block(u, lp, init, pos):
    bsz, t = u.shape[0], u.shape[1]
    c0, n0, m0, sb0, sc0, sd0 = init
    pa, pb, pc, pd = split_cols((u @ lp['w_in']).astype(jnp.float32), GROUP_COLS)

    aq, ak, av, ao, ai, af = split_cols(pa, A_SIZES)
    li = dir_gate(ai + lp['mlstm_i_bias'].reshape(-1))
    lf = jax.nn.log_sigmoid(dir_gate(af + lp['mlstm_f_bias'].reshape(-1)))
    ha, c1, n1, m1 = mlstm_scan(bidir(heads(aq)), bidir(heads(ak) * HEAD_DIM ** -0.5), bidir(heads(av)),
                                li, lf, c0, n0, m0)
    ya = jax.nn.sigmoid(heads(ao)) * head_norm(merge_dirs(ha), lp['mlstm_norm'], True)

    bqkv, bz, bbeta, balpha = split_cols(pb, B_SIZES)
    bq, bk, bv = jnp.split(jax.nn.silu(short_conv(bqkv, lp['delta_conv'])), 3, axis=-1)
    beta = jax.nn.sigmoid(dir_gate(bbeta))
    g = -jnp.exp(lp['delta_a_log'])[:, None, :, None] * jax.nn.softplus(dir_gate(balpha) + lp['delta_dt_bias'][:, None, :, None])
    hb, sb1 = delta_scan(bidir(l2norm(heads(bq)) * HEAD_DIM ** -0.5), bidir(l2norm(heads(bk))),
                         bidir(heads(bv)), beta, g, sb0)
    yb = head_norm(merge_dirs(hb), lp['delta_norm'], False) * jax.nn.silu(heads(bz))

    cq, ck, cv, cg = split_cols(pc, C_SIZES)
    q, k = heads(cq) * HEAD_DIM ** -0.5, heads(ck)
    if pos is not None:
        q, k = grid_rope(q, pos), grid_rope(k, pos)
    hc, sc1 = retention_scan(bidir(q), bidir(k), bidir(heads(cv)), -jnp.exp(lp['ret_decay']), sc0)
    yc = jax.nn.silu(heads(cg)) * head_norm(merge_dirs(hc), lp['ret_norm'], True)

    pd = pd + lp['rwkv_mu'] * (centred_shift(pd) - pd)
    dr, dk, dv, dw, da, dg = split_cols(pd, D_SIZES)
    r, k, v = heads(dr), heads(dk), heads(dv)
    w_pre = lp['rwkv_w0'] + jnp.einsum('btzr,zrc->btzc', jnp.tanh(dw.reshape(bsz, t, 2, RWKV_W_RANK)), lp['rwkv_w2'])
    decay = jnp.exp(-jnp.exp(-jax.nn.softplus(-w_pre) - 0.5))
    a = jax.nn.sigmoid(lp['rwkv_a0'] + jnp.einsum('btzr,zrc->btzc', da.reshape(bsz, t, 2, RWKV_A_RANK), lp['rwkv_a2']))
    a = a.reshape(bsz, t, 2, N_HG, HEAD_DIM)
    kh = l2norm(k * lp['rwkv_kk'].reshape(N_HG, HEAD_DIM))
    kt = k[:, :, None] * (1.0 + (a - 1.0) * lp['rwkv_ka'].reshape(N_HG, HEAD_DIM))
    hd, sd1 = rwkv_scan(bidir(r), bidir(kh), dir_vec(kt), bidir(v),
                        dir_vec(decay.reshape(bsz, t, 2, N_HG, HEAD_DIM)), dir_vec(a), sd0)
    bonus = jnp.sum(r[:, :, None] * kt * lp['rwkv_rk'], axis=-1, keepdims=True) * v[:, :, None]
    yd = heads(jax.nn.sigmoid(dg) @ lp['rwkv_g2']) * (head_norm(merge_dirs(hd), lp['rwkv_norm'], True) + jnp.sum(bonus, axis=2))

    y = jnp.concatenate([ya, yb, yc, yd], axis=2).reshape(bsz, t, D_MODEL)
    return y.astype(u.dtype) @ lp['w_out'], (c1, n1, m1, sb1, sc1, sd1)


def trunk_layer(h, mod, lp, init, pos):
    sh1, sc1, g1, sh2, sc2, g2 = jnp.split(mod, 6, axis=-1)
    y, states = mixing_block(layer_norm(h) * (1.0 + sc1) + sh1, lp, init, pos)
    h = layer_norm(DEEPNORM_ALPHA * h + g1 * y, lp['ln1_g'], lp['ln1_b'])
    u = layer_norm(h) * (1.0 + sc2) + sh2
    f = jnp.square(jax.nn.relu(u @ lp['w_ff1'])) @ lp['w_ff2']
    h = layer_norm(DEEPNORM_ALPHA * h + g2 * f, lp['ln2_g'], lp['ln2_b'])
    return h, states


def setup_inputs(seed: int = 0) -> dict:
    key = jax.random.key(seed)
    keys = iter(jax.random.split(key, 64))
    f32 = jnp.float32
    H, d, G = N_HG, HEAD_DIM, GROUP_W

    def nrm(shape, scale):
        return scale * jax.random.normal(next(keys), shape, f32)

    def gain(shape):
        return 1.0 + nrm(shape, 0.02)

    ret_base = jnp.log(-jnp.log(1.0 - 2.0 ** (-5.0 - jnp.arange(H, dtype=f32))))
    dt = jnp.exp(jax.random.uniform(next(keys), (DEPTH, 2, H), f32, math.log(1e-3), math.log(1e-1)))
    mat = (DEC_BATCH, DEPTH, 2, H, d, d)
    return {
        'x_prompt': nrm((BATCH, SEQ, D_MODEL), 1.0),
        'x_sample': nrm((DEC_BATCH, DEC_SEQ, D_MODEL), 1.0),
        'c': nrm((DEC_BATCH, D_MODEL), 1.0),
        'state_mlstm_C': nrm(mat, 0.1),
        'state_mlstm_n': nrm((DEC_BATCH, DEPTH, 2, H, d), 0.1),
        'state_mlstm_m': nrm((DEC_BATCH, DEPTH, 2, H), 0.5),
        'state_delta': nrm(mat, 0.1),
        'state_ret': nrm(mat, 0.3),
        'state_rwkv': nrm(mat, 0.1),
        'c_ctx': nrm((D_MODEL,), 1.0),
        'w_mod': nrm((DEPTH, D_MODEL, 6 * D_MODEL), 0.5 * D_MODEL ** -0.5),
        'b_mod': nrm((DEPTH, 6 * D_MODEL), 0.02),
        'w_in': nrm((DEPTH, D_MODEL, P_IN), D_MODEL ** -0.5),
        'w_out': nrm((DEPTH, D_MODEL, D_MODEL), DEEPNORM_BETA * D_MODEL ** -0.5),
        'ln1_g': gain((DEPTH, D_MODEL)),
        'ln1_b': nrm((DEPTH, D_MODEL), 0.02),
        'ln2_g': gain((DEPTH, D_MODEL)),
        'ln2_b': nrm((DEPTH, D_MODEL), 0.02),
        'w_ff1': nrm((DEPTH, D_MODEL, D_FF), D_MODEL ** -0.5),
        'w_ff2': nrm((DEPTH, D_FF, D_MODEL), DEEPNORM_BETA * D_FF ** -0.5),
        'mlstm_i_bias': nrm((DEPTH, 2, H), 0.1),
        'mlstm_f_bias': jnp.linspace(3.0, 6.0, H, dtype=f32) + nrm((DEPTH, 2, H), 0.1),
        'mlstm_norm': gain((DEPTH, G)),
        'delta_conv': nrm((DEPTH, CONV_K, 3 * G), CONV_K ** -0.5),
        'delta_a_log': jnp.log(jax.random.uniform(next(keys), (DEPTH, 2, H), f32, 1.0, 16.0)),
        'delta_dt_bias': dt + jnp.log(-jnp.expm1(-dt)),
        'delta_norm': gain((DEPTH, G)),
        'ret_decay': ret_base + nrm((DEPTH, 2, H), 0.05),
        'ret_norm': gain((DEPTH, G)),
        'rwkv_mu': jax.random.uniform(next(keys), (DEPTH, GROUP_COLS[3]), f32, 0.2, 0.8),
        'rwkv_w0': jnp.linspace(-6.5, -1.5, G, dtype=f32) + nrm((DEPTH, 2, G), 0.1),
        'rwkv_w2': nrm((DEPTH, 2, RWKV_W_RANK, G), 0.5 * RWKV_W_RANK ** -0.5),
        'rwkv_a0': nrm((DEPTH, 2, G), 0.1),
        'rwkv_a2': nrm((DEPTH, 2, RWKV_A_RANK, G), 0.5 * RWKV_A_RANK ** -0.5),
        'rwkv_g2': nrm((DEPTH, RWKV_G_RANK, G), RWKV_G_RANK ** -0.5),
        'rwkv_kk': 0.85 + nrm((DEPTH, G), 0.02),
        'rwkv_ka': 1.0 + nrm((DEPTH, G), 0.02),
        'rwkv_rk': nrm((DEPTH, H, d), 0.1),
        'rwkv_norm': gain((DEPTH, G)),
    }


def reference(x_prompt, x_sample, c, state_mlstm_C, state_mlstm_n, state_mlstm_m, state_delta, state_ret,
              state_rwkv, c_ctx, w_mod, b_mod, w_in, w_out, ln1_g, ln1_b, ln2_g, ln2_b, w_ff1, w_ff2,
              mlstm_i_bias, mlstm_f_bias, mlstm_norm, delta_conv, delta_a_log, delta_dt_bias, delta_norm,
              ret_decay, ret_norm, rwkv_mu, rwkv_w0, rwkv_w2, rwkv_a0, rwkv_a2, rwkv_g2, rwkv_kk, rwkv_ka,
              rwkv_rk, rwkv_norm):
    n_rows = x_sample.shape[1] // GRID_W
    pos = (jnp.repeat(jnp.arange(n_rows), GRID_W), jnp.tile(jnp.arange(GRID_W), n_rows))
    caches = (state_mlstm_C, state_mlstm_n, state_mlstm_m, state_delta, state_ret, state_rwkv)
    hp, hs = x_prompt, x_sample
    ctx_states = []
    for l in range(DEPTH):
        lp = {'w_in': w_in[l], 'w_out': w_out[l], 'ln1_g': ln1_g[l], 'ln1_b': ln1_b[l],
              'ln2_g': ln2_g[l], 'ln2_b': ln2_b[l], 'w_ff1': w_ff1[l], 'w_ff2': w_ff2[l],
              'mlstm_i_bias': mlstm_i_bias[l], 'mlstm_f_bias': mlstm_f_bias[l], 'mlstm_norm': mlstm_norm[l],
              'delta_conv': delta_conv[l], 'delta_a_log': delta_a_log[l], 'delta_dt_bias': delta_dt_bias[l],
              'delta_norm': delta_norm[l], 'ret_decay': ret_decay[l], 'ret_norm': ret_norm[l],
              'rwkv_mu': rwkv_mu[l], 'rwkv_w0': rwkv_w0[l], 'rwkv_w2': rwkv_w2[l], 'rwkv_a0': rwkv_a0[l],
              'rwkv_a2': rwkv_a2[l], 'rwkv_g2': rwkv_g2[l], 'rwkv_kk': rwkv_kk[l], 'rwkv_ka': rwkv_ka[l],
              'rwkv_rk': rwkv_rk[l], 'rwkv_norm': rwkv_norm[l]}
        mod_ctx = jax.nn.silu(c_ctx) @ w_mod[l] + b_mod[l]
        hp, st = trunk_layer(hp, mod_ctx, lp, zero_states(hp.shape[0]), None)
        ctx_states.append(st)
        mod_lat = (jax.nn.silu(c) @ w_mod[l] + b_mod[l])[:, None, :]
        init = tuple(jnp.swapaxes(s[:, l], 0, 1).astype(jnp.float32) for s in caches)
        hs, _ = trunk_layer(hs, mod_lat, lp, init, pos)
    new_mlstm_C, new_mlstm_n, new_mlstm_m, new_delta, new_ret, new_rwkv = [
        jnp.stack([jnp.swapaxes(st[i], 0, 1) for st in ctx_states], axis=1) for i in range(6)]
    return (hp, hs, new_mlstm_C, new_mlstm_n, new_mlstm_m, new_delta, new_ret, new_rwkv)
```

```python
import functools
import math

import numpy as np
import jax
import jax.numpy as jnp
from jax import lax
from jax.experimental import pallas as pl
from jax.experimental.pallas import tpu as pltpu

D_MODEL = 1024
DEPTH = 2
GRID_W = 64
HEAD_DIM = 64
N_HEADS = 4
GROUP_W = 256
D_FF = 4 * D_MODEL
CHUNK = 64
CONV_K = 5
ROPE_BASE = 10000.0
LN_EPS = 1e-5
DEEPNORM_ALPHA = (2 * DEPTH) ** 0.25
RWKV_LORA = 64
GATE_LANES = 128
N_GATES = 32
ROW_TILE = 256
VMEM_LIMIT = 56 * 1024 * 1024

HI = lax.Precision.HIGHEST
NEG_INF = float("-inf")


def _acc(a):
    return jnp.result_type(a.dtype, jnp.float32)


def mm(a, b, prec=HI):
    return lax.dot_general(a, b, (((1,), (0,)), ((), ())), precision=prec,
                           preferred_element_type=_acc(a))


def mm_nt(a, b, prec=HI):
    return lax.dot_general(a, b, (((1,), (1,)), ((), ())), precision=prec,
                           preferred_element_type=_acc(a))


def mm_tn(a, b, prec=HI):
    return lax.dot_general(a, b, (((0,), (0,)), ((), ())), precision=prec,
                           preferred_element_type=_acc(a))


def bdot(a, b):
    return lax.dot_general(a.astype(jnp.bfloat16), b.astype(jnp.bfloat16),
                           (((1,), (0,)), ((), ())), preferred_element_type=jnp.float32)


def sigmoid(x):
    return 1.0 / (1.0 + jnp.exp(-x))


def silu(x):
    return x * sigmoid(x)


def softplus(x):
    return jnp.maximum(x, 0.0) + jnp.log(1.0 + jnp.exp(-jnp.abs(x)))


def log_sigmoid(x):
    return -softplus(-x)


def chunk_masks(rev, dtype=jnp.float32):
    r = lax.broadcasted_iota(jnp.int32, (CHUNK, CHUNK), 0)
    c = lax.broadcasted_iota(jnp.int32, (CHUNK, CHUNK), 1)
    if rev:
        incl, strict = r <= c, r < c
    else:
        incl, strict = r >= c, r > c
    eye = (r == c).astype(dtype)
    return incl, strict, incl.astype(dtype), eye


def inv_unit_tri(a, eye):
    p = -a
    t = eye + p
    for _ in range(5):
        p = mm(p, p)
        t = t + mm(t, p)
    return t


def head_sum(x, bd):
    return mm(x, bd)


def head_norm(y, gain, bd, center):
    if center:
        y = y - head_sum(y, bd) * (1.0 / HEAD_DIM)
    ms = head_sum(y * y, bd) * (1.0 / HEAD_DIM)
    return y * lax.rsqrt(ms + LN_EPS) * gain


def l2norm_heads(x, bd):
    return x * lax.rsqrt(head_sum(x * x, bd) + 1e-6)


def layer_norm(x):
    mu = jnp.mean(x, axis=-1, keepdims=True)
    xc = x - mu
    var = jnp.mean(xc * xc, axis=-1, keepdims=True)
    return xc * lax.rsqrt(var + LN_EPS)


def mlstm_chunk(q, k, v_aug, b_col, b_row, i_col, i_row, bl, c_aug, m, incl):
    logd = jnp.where(incl, b_col - b_row + i_row, NEG_INF)
    inter = b_col + m
    mt = jnp.maximum(inter, jnp.max(logd, axis=1, keepdims=True))
    s = mm_nt(q, k) * jnp.exp(logd - mt)
    ew = jnp.exp(inter - mt)
    na = mm(s, v_aug) + mm(q, c_aug) * ew
    num = na[:, :HEAD_DIM]
    den = na[:, HEAD_DIM:HEAD_DIM + 1]
    h = num / jnp.maximum(jnp.abs(den), jnp.exp(-mt))
    wk = bl - b_col + i_col
    m_new = jnp.maximum(bl + m, jnp.max(wk, axis=0, keepdims=True))
    sk = jnp.exp(wk - m_new)
    sp = jnp.exp(bl + m - m_new)
    c_new = c_aug * sp + mm_tn(k * sk, v_aug)
    return h, c_new, m_new


def delta_chunk(q, k, v, beta, gc_col, gc_row, gl, s, incl, strict, eye):
    decay = jnp.exp(jnp.where(incl, gc_col - gc_row, NEG_INF))
    a = jnp.where(strict, mm_nt(k, k) * decay, 0.0) * beta
    t = inv_unit_tri(a, eye)
    eg = jnp.exp(gc_col)
    u = mm(t, v * beta)
    w = mm(t, k * (beta * eg))
    v_new = u - mm(w, s)
    qk = mm_nt(q, k) * decay
    o = mm(q * eg, s) + mm(qk, v_new)
    s_new = s * jnp.exp(gl) + mm_tn(k * jnp.exp(gl - gc_col), v_new)
    return o, s_new


def retention_chunk(q, k, v, dmat, xi, zeta, gl, s):
    sc = mm_nt(q, k) * dmat
    o = mm(sc, v) + mm(q, s) * xi
    s_new = s * gl + mm_tn(k * zeta, v)
    return o, s_new


def rwkv_chunk(rp, khp, ktn, bn, kte, be, v, etot, s, incl, strict, eye):
    a_kt = jnp.where(strict, mm_nt(khp, ktn), 0.0)
    a_b = jnp.where(strict, mm_nt(khp, bn), 0.0)
    b_kt = jnp.where(incl, mm_nt(rp, ktn), 0.0)
    b_b = jnp.where(incl, mm_nt(rp, bn), 0.0)
    t = inv_unit_tri(a_b, eye)
    u = mm(t, mm_nt(khp, s) + mm(a_kt, v))
    o = mm_nt(rp, s) + mm(b_kt, v) - mm(b_b, u)
    s_new = s * etot + mm_tn(v, kte) - mm_tn(u, be)
    return o, s_new


def _bd_mat():
    r = lax.shift_right_logical(lax.broadcasted_iota(jnp.int32, (GROUP_W, GROUP_W), 0), 6)
    c = lax.shift_right_logical(lax.broadcasted_iota(jnp.int32, (GROUP_W, GROUP_W), 1), 6)
    return (r == c).astype(jnp.float32)


def _two_pass(nc, body):
    half = nc // 2

    def first(i, carry):
        body(i, True)
        return carry

    def second(i, carry):
        body(i, False)
        return carry

    lax.fori_loop(0, half, first, 0)
    lax.fori_loop(half, nc, second, 0)


def _chunk_pos(i, z, nc):
    c = i if z == 0 else nc - 1 - i
    return c, pl.multiple_of(c * CHUNK, CHUNK)


def _halo(ref, c, t0, nc, width):
    t_len = nc * CHUNK
    ps = pl.multiple_of(jnp.maximum(t0 - 8, 0), 8)
    ns = pl.multiple_of(jnp.minimum(t0 + CHUNK, t_len - 8), 8)
    prev = ref[pl.ds(ps, 8), 0:width]
    nxt = ref[pl.ds(ns, 8), 0:width]
    prev = jnp.where(c > 0, prev, 0.0)
    nxt = jnp.where(c < nc - 1, nxt, 0.0)
    cur = ref[pl.ds(t0, CHUNK), 0:width]
    return jnp.concatenate([prev, cur, nxt], axis=0)


def _hsl(x, h, base=0):
    return x[:, base + h * HEAD_DIM: base + (h + 1) * HEAD_DIM]


def mlstm_kernel(p_ref, gc_ref, gr_ref, bc_ref, br_ref, norm_ref, c0_ref, n0_ref, m0_ref,
                 y_ref, c1_ref, n1_ref, m1_ref, hscr, caug, mscr, *, nc):
    f32 = jnp.float32
    zeros63 = jnp.zeros((CHUNK, HEAD_DIM - 1), f32)
    ones_col = jnp.concatenate([jnp.ones((CHUNK, 1), f32), zeros63], axis=1)
    for z in range(2):
        for h in range(N_HEADS):
            caug[z, h] = jnp.concatenate([c0_ref[0, 0, z, h], n0_ref[0, 0, z, h], zeros63], axis=1)
            mscr[z, h] = m0_ref[0, 0, z, h]
    bd = _bd_mat()
    masks = [chunk_masks(False), chunk_masks(True)]

    def body(i, first):
        for z in range(2):
            incl, _, mc, _ = masks[z]
            c, t0 = _chunk_pos(i, z, nc)
            x = p_ref[pl.ds(t0, CHUNK), :]
            g_c = gc_ref[pl.ds(t0, CHUNK), :] + bc_ref[...]
            g_r = gr_ref[c] + br_ref[...]
            lf_c = log_sigmoid(g_c)
            lf_r = log_sigmoid(g_r)
            b_c = mm(mc, lf_c)
            b_r = mm_nt(lf_r, mc)
            bl = jnp.sum(lf_c, axis=0, keepdims=True)
            hs = []
            for h in range(N_HEADS):
                j = z * N_HEADS + h
                jf = 8 + j
                q = _hsl(x, h)
                k = _hsl(x, h, GROUP_W) * HEAD_DIM ** -0.5
                v_aug = jnp.concatenate([_hsl(x, h, 2 * GROUP_W), ones_col], axis=1)
                hh, c_new, m_new = mlstm_chunk(
                    q, k, v_aug, b_c[:, jf:jf + 1], b_r[jf:jf + 1, :], g_c[:, j:j + 1], g_r[j:j + 1, :],
                    bl[:, jf:jf + 1], caug[z, h], mscr[z, h], incl)
                caug[z, h] = c_new
                mscr[z, h] = m_new
                hs.append(hh)
            hcat = jnp.concatenate(hs, axis=1)
            if first:
                hscr[pl.ds(t0, CHUNK), :] = hcat
            else:
                tot = hscr[pl.ds(t0, CHUNK), :] + hcat
                y_ref[pl.ds(t0, CHUNK), :] = sigmoid(x[:, 3 * GROUP_W:]) * head_norm(tot, norm_ref[...], bd, True)

    _two_pass(nc, body)
    for z in range(2):
        for h in range(N_HEADS):
            ca = caug[z, h]
            c1_ref[0, z, h] = ca[:, :HEAD_DIM]
            n1_ref[0, z, h] = ca[:, HEAD_DIM:HEAD_DIM + 1]
            m1_ref[0, z, h] = mscr[z, h]


def delta_kernel(p_ref, gc_ref, gr_ref, conv_ref, prow_ref, pcol_ref, norm_ref, s0_ref,
                 y_ref, s1_ref, hscr, *, nc):
    for z in range(2):
        for h in range(N_HEADS):
            s1_ref[0, z, h] = s0_ref[0, 0, z, h]
    bd = _bd_mat()
    masks = [chunk_masks(False), chunk_masks(True)]
    nega_row = -jnp.exp(prow_ref[0:1, :])
    dtb_row = prow_ref[1:2, :]
    nega_col = -jnp.exp(pcol_ref[:, 0:1])
    dtb_col = pcol_ref[:, 1:2]
    qkv_w = 3 * GROUP_W

    def body(i, first):
        for z in range(2):
            incl, strict, mc, eye = masks[z]
            c, t0 = _chunk_pos(i, z, nc)
            ext = _halo(p_ref, c, t0, nc, qkv_w)
            acc = ext[6:6 + CHUNK] * conv_ref[0:1, :]
            for j in range(1, CONV_K):
                acc = acc + ext[6 + j:6 + j + CHUNK] * conv_ref[j:j + 1, :]
            qkv = silu(acc)
            q = l2norm_heads(qkv[:, :GROUP_W], bd) * HEAD_DIM ** -0.5
            k = l2norm_heads(qkv[:, GROUP_W:2 * GROUP_W], bd)
            v = qkv[:, 2 * GROUP_W:]
            g_c = gc_ref[pl.ds(t0, CHUNK), :]
            g_r = gr_ref[c]
            beta_c = sigmoid(g_c)
            gg_c = nega_row * softplus(g_c + dtb_row)
            gg_r = nega_col * softplus(g_r + dtb_col)
            gcum_c = mm(mc, gg_c)
            gcum_r = mm_nt(gg_r, mc)
            gl = jnp.sum(gg_c, axis=0, keepdims=True)
            hs = []
            for h in range(N_HEADS):
                jb = 16 + z * N_HEADS + h
                ja = 24 + z * N_HEADS + h
                o, s_new = delta_chunk(_hsl(q, h), _hsl(k, h), _hsl(v, h), beta_c[:, jb:jb + 1],
                                       gcum_c[:, ja:ja + 1], gcum_r[ja:ja + 1, :], gl[:, ja:ja + 1],
                                       s1_ref[0, z, h], incl, strict, eye)
                s1_ref[0, z, h] = s_new
                hs.append(o)
            hcat = jnp.concatenate(hs, axis=1)
            if first:
                hscr[pl.ds(t0, CHUNK), :] = hcat
            else:
                tot = hscr[pl.ds(t0, CHUNK), :] + hcat
                zg = p_ref[pl.ds(t0, CHUNK), qkv_w:]
                y_ref[pl.ds(t0, CHUNK), :] = head_norm(tot, norm_ref[...], bd, False) * silu(zg)

    _two_pass(nc, body)


def _rope(x, cos, sin, lane_lo):
    n = x.shape[1]
    swapped = jnp.where(lane_lo, pltpu.roll(x, n - 16, 1), pltpu.roll(x, 16, 1))
    return x * cos + swapped * sin


def ret_kernel(*refs, nc, rope):
    if rope:
        p_ref, cos_ref, sin_ref, rd_ref, norm_ref, s0_ref, y_ref, s1_ref, hscr, cst = refs
    else:
        p_ref, rd_ref, norm_ref, s0_ref, y_ref, s1_ref, hscr, cst = refs
    f32 = jnp.float32
    bd = _bd_mat()
    masks = [chunk_masks(False), chunk_masks(True)]
    r = lax.broadcasted_iota(jnp.int32, (CHUNK, CHUNK), 0).astype(f32)
    cc = lax.broadcasted_iota(jnp.int32, (CHUNK, CHUNK), 1).astype(f32)
    lg_row = -jnp.exp(rd_ref[...])
    for z in range(2):
        incl = masks[z][0]
        for h in range(N_HEADS):
            j = z * N_HEADS + h
            lg = lg_row[:, j:j + 1]
            if z == 0:
                cst[z, h, 0] = jnp.exp(jnp.where(incl, (r - cc) * lg, NEG_INF))
                cst[z, h, 1] = jnp.exp((r + 1.0) * lg)
                cst[z, h, 2] = jnp.exp((CHUNK - 1.0 - r) * lg)
            else:
                cst[z, h, 0] = jnp.exp(jnp.where(incl, (cc - r) * lg, NEG_INF))
                cst[z, h, 1] = jnp.exp((CHUNK - r) * lg)
                cst[z, h, 2] = jnp.exp(r * lg)
            cst[z, h, 3] = jnp.exp(CHUNK * lg + 0.0 * r)
            s1_ref[0, z, h] = s0_ref[0, 0, z, h]
    if rope:
        lane = lax.broadcasted_iota(jnp.int32, (CHUNK, GROUP_W), 1)
        lane_lo = jnp.bitwise_and(lane, 31) < 16

    def body(i, first):
        for z in range(2):
            c, t0 = _chunk_pos(i, z, nc)
            x = p_ref[pl.ds(t0, CHUNK), :]
            q = x[:, :GROUP_W] * HEAD_DIM ** -0.5
            k = x[:, GROUP_W:2 * GROUP_W]
            v = x[:, 2 * GROUP_W:3 * GROUP_W]
            if rope:
                cos = cos_ref[pl.ds(t0, CHUNK), :]
                sin = sin_ref[pl.ds(t0, CHUNK), :]
                q = _rope(q, cos, sin, lane_lo)
                k = _rope(k, cos, sin, lane_lo)
            hs = []
            for h in range(N_HEADS):
                o, s_new = retention_chunk(_hsl(q, h), _hsl(k, h), _hsl(v, h), cst[z, h, 0], cst[z, h, 1],
                                           cst[z, h, 2], cst[z, h, 3], s1_ref[0, z, h])
                s1_ref[0, z, h] = s_new
                hs.append(o)
            hcat = jnp.concatenate(hs, axis=1)
            if first:
                hscr[pl.ds(t0, CHUNK), :] = hcat
            else:
                tot = hscr[pl.ds(t0, CHUNK), :] + hcat
                y_ref[pl.ds(t0, CHUNK), :] = silu(x[:, 3 * GROUP_W:]) * head_norm(tot, norm_ref[...], bd, True)

    _two_pass(nc, body)


def rwkv_kernel(p_ref, mu_ref, w0_ref, w2_ref, a0_ref, a2_ref, g2_ref, vec_ref, s0_ref,
                y_ref, s1_ref, hscr, *, nc):
    for z in range(2):
        for h in range(N_HEADS):
            s1_ref[0, z, h] = s0_ref[0, 0, z, h]
    bd = _bd_mat()
    masks = [chunk_masks(False), chunk_masks(True)]
    width = p_ref.shape[1]
    kk = vec_ref[0:1, :]
    ka = vec_ref[1:2, :]
    rk = vec_ref[2:3, :]
    gain = vec_ref[3:4, :]

    def lora_a(da, zz):
        return sigmoid(a0_ref[zz:zz + 1, :] + mm(da[:, zz * RWKV_LORA:(zz + 1) * RWKV_LORA], a2_ref[zz]))

    def body(i, first):
        for z in range(2):
            incl, strict, mc, eye = masks[z]
            c, t0 = _chunk_pos(i, z, nc)
            ext = _halo(p_ref, c, t0, nc, width)
            cur = ext[8:8 + CHUNK]
            pdm = cur + mu_ref[...] * (0.5 * (ext[7:7 + CHUNK] + ext[9:9 + CHUNK]) - cur)
            r = pdm[:, :GROUP_W]
            k = pdm[:, GROUP_W:2 * GROUP_W]
            v = pdm[:, 2 * GROUP_W:3 * GROUP_W]
            dw = pdm[:, 3 * GROUP_W:3 * GROUP_W + 128]
            da = pdm[:, 3 * GROUP_W + 128:3 * GROUP_W + 256]
            a_z = lora_a(da, z)
            w_pre = w0_ref[z:z + 1, :] + mm(jnp.tanh(dw[:, z * RWKV_LORA:(z + 1) * RWKV_LORA]), w2_ref[z])
            lw = -jnp.exp(-softplus(-w_pre) - 0.5)
            kh = l2norm_heads(k * kk, bd)
            kt_z = k * (1.0 + (a_z - 1.0) * ka)
            b = a_z * kh
            cw = mm(mc, lw)
            tot = jnp.sum(lw, axis=0, keepdims=True)
            rp = r * jnp.exp(cw)
            khp = kh * jnp.exp(cw - lw)
            en = jnp.exp(-cw)
            et = jnp.exp(tot - cw)
            etot = jnp.exp(tot)
            ktn, bn, kte, be = kt_z * en, b * en, kt_z * et, b * et
            hs = []
            for h in range(N_HEADS):
                o, s_new = rwkv_chunk(_hsl(rp, h), _hsl(khp, h), _hsl(ktn, h), _hsl(bn, h), _hsl(kte, h),
                                      _hsl(be, h), _hsl(v, h), _hsl(etot, h), s1_ref[0, z, h], incl, strict, eye)
                s1_ref[0, z, h] = s_new
                hs.append(o)
            hcat = jnp.concatenate(hs, axis=1)
            if first:
                hscr[pl.ds(t0, CHUNK), :] = hcat
            else:
                tot_h = hscr[pl.ds(t0, CHUNK), :] + hcat
                a_o = lora_a(da, 1 - z)
                kt_o = k * (1.0 + (a_o - 1.0) * ka)
                bonus = (head_sum(r * kt_z * rk, bd) + head_sum(r * kt_o * rk, bd)) * v
                dg = pdm[:, 3 * GROUP_W + 256:]
                g = mm(sigmoid(dg), g2_ref[...])
                y_ref[pl.ds(t0, CHUNK), :] = g * (head_norm(tot_h, gain, bd, True) + bonus)

    _two_pass(nc, body)


def mod_kernel(c_ref, w_ref, b_ref, o_ref):
    o_ref[0] = bdot(silu(c_ref[...]), w_ref[0]) + b_ref[0]


def proj_kernel(x_ref, mod_ref, wa, wb, wc, wd, wg, wgt, pa, pb, pc, pd, pg, pgt):
    m = mod_ref[0]
    u = layer_norm(x_ref[...]) * (1.0 + m[1:2]) + m[0:1]
    ub = u.astype(jnp.bfloat16)
    f32 = jnp.float32
    pa[...] = jnp.dot(ub, wa[...], preferred_element_type=f32)
    pb[...] = jnp.dot(ub, wb[...], preferred_element_type=f32)
    pc[...] = jnp.dot(ub, wc[...], preferred_element_type=f32)
    pd[...] = jnp.dot(ub, wd[...], preferred_element_type=f32)
    pg[...] = jnp.dot(ub, wg[...], preferred_element_type=f32)
    gt = lax.dot_general(wgt[...], ub, (((1,), (1,)), ((), ())), preferred_element_type=f32)
    for j in range(ROW_TILE // CHUNK):
        pgt[j] = gt[:, j * CHUNK:(j + 1) * CHUNK]


def post_kernel(h_ref, ya, yb, yc, yd, mod_ref, wout, l1g, l1b, w1, w2, l2g, l2b, o_ref):
    f32 = jnp.float32
    bf = jnp.bfloat16
    m = mod_ref[0]
    y = jnp.concatenate([ya[...], yb[...], yc[...], yd[...]], axis=1).astype(bf)
    attn = jnp.dot(y, wout[...], preferred_element_type=f32)
    h1 = layer_norm(DEEPNORM_ALPHA * h_ref[...] + m[2:3] * attn) * l1g[...] + l1b[...]
    u2 = layer_norm(h1) * (1.0 + m[4:5]) + m[3:4]
    f = jnp.dot(u2.astype(bf), w1[...], preferred_element_type=f32)
    f = jnp.square(jnp.maximum(f, 0.0))
    f2 = jnp.dot(f.astype(bf), w2[...], preferred_element_type=f32)
    o_ref[...] = layer_norm(DEEPNORM_ALPHA * h1 + m[5:6] * f2) * l2g[...] + l2b[...]


def _params(n_grid_axes):
    return pltpu.CompilerParams(dimension_semantics=("arbitrary",) * n_grid_axes,
                                vmem_limit_bytes=VMEM_LIMIT)


def _const_spec(a):
    nd = a.ndim
    return pl.BlockSpec(a.shape, lambda *_: (0,) * nd)


def _f32(shape):
    return jax.ShapeDtypeStruct(shape, jnp.float32)


def _mod_call(c_all, w_mod, b_mod):
    depth, d, n = w_mod.shape
    tn = 512
    return pl.pallas_call(
        mod_kernel, grid=(depth, n // tn),
        in_specs=[pl.BlockSpec(c_all.shape, lambda l, j: (0, 0)),
                  pl.BlockSpec((1, d, tn), lambda l, j: (l, 0, j)),
                  pl.BlockSpec((1, 1, tn), lambda l, j: (l, 0, j))],
        out_specs=pl.BlockSpec((1, c_all.shape[0], tn), lambda l, j: (l, 0, j)),
        out_shape=_f32((depth, c_all.shape[0], n)),
        compiler_params=_params(2), name="mod")(c_all, w_mod, b_mod.reshape(depth, 1, n))


def _mod_spec(n_ptiles, tiles_per_dec):
    def index(i):
        return (jnp.where(i < n_ptiles, 0, 1 + (i - n_ptiles) // tiles_per_dec), 0, 0)
    return pl.BlockSpec((1, 6, D_MODEL), index)


def _proj_call(x_all, mod_l, ws, n_ptiles, tiles_per_dec):
    n_rows = x_all.shape[0]
    row = lambda i: (i, 0)
    widths = [w.shape[1] for w in ws[:5]]
    per_tile = ROW_TILE // CHUNK
    return pl.pallas_call(
        proj_kernel, grid=(n_rows // ROW_TILE,),
        in_specs=[pl.BlockSpec((ROW_TILE, D_MODEL), row), _mod_spec(n_ptiles, tiles_per_dec)]
                 + [_const_spec(w) for w in ws],
        out_specs=[pl.BlockSpec((ROW_TILE, w), row) for w in widths]
                  + [pl.BlockSpec((per_tile, N_GATES, CHUNK), lambda i: (i, 0, 0))],
        out_shape=[_f32((n_rows, w)) for w in widths] + [_f32((n_rows // CHUNK, N_GATES, CHUNK))],
        compiler_params=_params(1), name="proj")(x_all, mod_l, *ws)


def _post_call(h_all, ys, mod_l, ws, n_ptiles, tiles_per_dec):
    n_rows = h_all.shape[0]
    row = lambda i: (i, 0)
    return pl.pallas_call(
        post_kernel, grid=(n_rows // ROW_TILE,),
        in_specs=[pl.BlockSpec((ROW_TILE, D_MODEL), row)]
                 + [pl.BlockSpec((ROW_TILE, GROUP_W), row) for _ in ys]
                 + [_mod_spec(n_ptiles, tiles_per_dec)] + [_const_spec(w) for w in ws],
        out_specs=pl.BlockSpec((ROW_TILE, D_MODEL), row),
        out_shape=_f32((n_rows, D_MODEL)),
        compiler_params=_params(1), name="post")(h_all, *ys, mod_l, *ws)


def _state_spec(shape_tail, layer):
    nd = len(shape_tail)
    return pl.BlockSpec((1, 1) + shape_tail, lambda i: (i, layer) + (0,) * nd)


def _out_state_spec(shape_tail):
    nd = len(shape_tail)
    return pl.BlockSpec((1,) + shape_tail, lambda i: (i,) + (0,) * nd)


MAT = (2, N_HEADS, HEAD_DIM, HEAD_DIM)


def _mixer_call(kernel, name, t_len, n_seq, blk_off, seq_inputs, const_inputs, state_inputs, layer,
                state_tails, scratch):
    nc = t_len // CHUNK
    in_specs = []
    for a in seq_inputs:
        if a.ndim == 2:
            in_specs.append(pl.BlockSpec((t_len, a.shape[1]), lambda i: (i + blk_off, 0)))
        else:
            in_specs.append(pl.BlockSpec((nc,) + a.shape[1:], lambda i: (i + blk_off, 0, 0)))
    in_specs += [_const_spec(a) for a in const_inputs]
    in_specs += [_state_spec(a.shape[2:], layer) for a in state_inputs]
    out_specs = [pl.BlockSpec((t_len, GROUP_W), lambda i: (i, 0))] + [_out_state_spec(t) for t in state_tails]
    out_shape = [_f32((n_seq * t_len, GROUP_W))] + [_f32((n_seq,) + t) for t in state_tails]
    return pl.pallas_call(
        kernel, grid=(n_seq,), in_specs=in_specs, out_specs=out_specs, out_shape=out_shape,
        scratch_shapes=[pltpu.VMEM((t_len, GROUP_W), jnp.float32)] + scratch,
        compiler_params=_params(1), name=name)(*seq_inputs, *const_inputs, *state_inputs)


def _rope_tables(t_len):
    nf = HEAD_DIM // 4
    inv = ROPE_BASE ** (-jnp.arange(nf, dtype=jnp.float32) / nf)
    t = jnp.arange(t_len)
    ang_r = (t // GRID_W).astype(jnp.float32)[:, None] * inv[None, :]
    ang_c = (t % GRID_W).astype(jnp.float32)[:, None] * inv[None, :]
    cos = jnp.concatenate([jnp.cos(ang_r)] * 2 + [jnp.cos(ang_c)] * 2, axis=1)
    sin = jnp.concatenate([-jnp.sin(ang_r), jnp.sin(ang_r), -jnp.sin(ang_c), jnp.sin(ang_c)], axis=1)
    return jnp.tile(cos, (1, N_HEADS)), jnp.tile(sin, (1, N_HEADS))


def _lane_row(vals, lo, n_rows=1, row=0, width=GATE_LANES):
    out = jnp.zeros((n_rows, width), jnp.float32)
    return out.at[row, lo:lo + vals.size].set(vals.reshape(-1))


def kernel(x_prompt, x_sample, c, state_mlstm_C, state_mlstm_n, state_mlstm_m, state_delta, state_ret,
           state_rwkv, c_ctx, w_mod, b_mod, w_in, w_out, ln1_g, ln1_b, ln2_g, ln2_b, w_ff1, w_ff2,
           mlstm_i_bias, mlstm_f_bias, mlstm_norm, delta_conv, delta_a_log, delta_dt_bias, delta_norm,
           ret_decay, ret_norm, rwkv_mu, rwkv_w0, rwkv_w2, rwkv_a0, rwkv_a2, rwkv_g2, rwkv_kk, rwkv_ka,
           rwkv_rk, rwkv_norm):
    f32, bf = jnp.float32, jnp.bfloat16
    batch, seq, d = x_prompt.shape
    dec_batch, dec_seq, _ = x_sample.shape
    assert d == D_MODEL and seq % ROW_TILE == 0 and dec_seq % ROW_TILE == 0
    assert (seq // CHUNK) % 2 == 0 and (dec_seq // CHUNK) % 2 == 0 and (batch * seq) % dec_seq == 0
    n_prows = batch * seq
    n_ptiles = n_prows // ROW_TILE
    tiles_per_dec = dec_seq // ROW_TILE

    c_all = jnp.zeros((8, d), f32).at[0].set(c_ctx).at[1:1 + dec_batch].set(c)
    mod = _mod_call(c_all, w_mod, b_mod).reshape(DEPTH, 8, 6, d)
    h_all = jnp.concatenate([x_prompt.reshape(n_prows, d), x_sample.reshape(dec_batch * dec_seq, d)], axis=0)
    cos_t, sin_t = _rope_tables(dec_seq)

    zmat = jnp.zeros((batch, 1) + MAT, f32)
    zn = jnp.zeros((batch, 1, 2, N_HEADS, HEAD_DIM, 1), f32)
    zm = jnp.zeros((batch, 1, 2, N_HEADS, 1, 1), f32)
    n_cache = state_mlstm_n.reshape(state_mlstm_n.shape + (1,))
    m_cache = state_mlstm_m.reshape(state_mlstm_m.shape + (1, 1))
    groups = ((seq, batch, 0), (dec_seq, dec_batch, n_prows // dec_seq))

    new_states = []
    for l in range(DEPTH):
        wl = w_in[l]
        ws = [wl[:, 0:1024], wl[:, 1040:2064], wl[:, 2080:3104], wl[:, 3104:4256],
              jnp.pad(jnp.concatenate([wl[:, 1024:1040], wl[:, 2064:2080]], axis=1), ((0, 0), (0, GATE_LANES - N_GATES)))]
        ws = [w.astype(bf) for w in ws]
        ws.append(ws[4][:, :N_GATES].T)
        pa, pb, pc, pd, pg, pgt = _proj_call(h_all, mod[l], ws, n_ptiles, tiles_per_dec)

        bias_c = _lane_row(mlstm_i_bias[l], 0) + _lane_row(mlstm_f_bias[l], 8)
        bias_r = bias_c[0, :N_GATES].reshape(N_GATES, 1)
        dl_row = _lane_row(delta_a_log[l], 24, 2, 0) + _lane_row(delta_dt_bias[l], 24, 2, 1)
        dl_col = dl_row[:, :N_GATES].T
        rd_row = _lane_row(ret_decay[l], 0)
        rw_vec = jnp.stack([rwkv_kk[l], rwkv_ka[l], rwkv_rk[l].reshape(-1), rwkv_norm[l]], axis=0)
        row = lambda a: a.reshape(1, -1)

        ys = [[], [], [], []]
        layer_states = None
        for gi, (t_len, n_seq, off) in enumerate(groups):
            nc = t_len // CHUNK
            ctx = gi == 0
            lay = 0 if ctx else l
            ya, c1, n1, m1 = _mixer_call(
                functools.partial(mlstm_kernel, nc=nc), "mlstm", t_len, n_seq, off, [pa, pg, pgt],
                [bias_c, bias_r, row(mlstm_norm[l])],
                [zmat, zn, zm] if ctx else [state_mlstm_C, n_cache, m_cache], lay,
                [MAT, (2, N_HEADS, HEAD_DIM, 1), (2, N_HEADS, 1, 1)],
                [pltpu.VMEM((2, N_HEADS, HEAD_DIM, 2 * HEAD_DIM), f32), pltpu.VMEM((2, N_HEADS, 1, 1), f32)])
            yb, sb1 = _mixer_call(
                functools.partial(delta_kernel, nc=nc), "delta", t_len, n_seq, off, [pb, pg, pgt],
                [delta_conv[l], dl_row, dl_col, row(delta_norm[l])],
                [zmat if ctx else state_delta], lay, [MAT], [])
            ret_consts = [rd_row, row(ret_norm[l])]
            if not ctx:
                ret_consts = [cos_t, sin_t] + ret_consts
            yc, sc1 = _mixer_call(
                functools.partial(ret_kernel, nc=nc, rope=not ctx), "ret", t_len, n_seq, off, [pc],
                ret_consts, [zmat if ctx else state_ret], lay, [MAT],
                [pltpu.VMEM((2, N_HEADS, 4, CHUNK, CHUNK), f32)])
            yd, sd1 = _mixer_call(
                functools.partial(rwkv_kernel, nc=nc), "rwkv", t_len, n_seq, off, [pd],
                [row(rwkv_mu[l]), rwkv_w0[l], rwkv_w2[l], rwkv_a0[l], rwkv_a2[l], rwkv_g2[l], rw_vec],
                [zmat if ctx else state_rwkv], lay, [MAT], [])
            for lst, y in zip(ys, (ya, yb, yc, yd)):
                lst.append(y)
            if ctx:
                layer_states = (c1, n1.reshape(n1.shape[:-1]), m1.reshape(m1.shape[:-2]), sb1, sc1, sd1)
        new_states.append(layer_states)
        ys = [jnp.concatenate(y, axis=0) for y in ys]
        post_ws = [w_out[l].astype(bf), row(ln1_g[l]), row(ln1_b[l]), w_ff1[l].astype(bf), w_ff2[l].astype(bf),
                   row(ln2_g[l]), row(ln2_b[l])]
        h_all = _post_call(h_all, ys, mod[l], post_ws, n_ptiles, tiles_per_dec)

    outs = [jnp.stack([st[i] for st in new_states], axis=1) for i in range(6)]
    return (h_all[:n_prows].reshape(batch, seq, d), h_all[n_prows:].reshape(dec_batch, dec_seq, d), *outs)
```

```python
import functools

import jax
import jax.numpy as jnp
from jax import lax
from jax.experimental import pallas as pl
from jax.experimental.pallas import tpu as pltpu

D_MODEL = 1024
DEPTH = 2
GRID_W = 64
HEAD_DIM = 64
N_HEADS = 4
GROUP_W = N_HEADS * HEAD_DIM
CHUNK = 64
CONV_K = 5
ROPE_BASE = 10000.0
LN_EPS = 1e-5
DEEPNORM_ALPHA = (2 * DEPTH) ** 0.25
RWKV_LORA = 64
GATE_LANES = 128
N_GATES = 32
ROW_TILE = 256
VMEM_LIMIT = 56 * 1024 * 1024

NEG_INF = float("-inf")
BF = jnp.bfloat16
_NN = (((1,), (0,)), ((), ()))
_NT = (((1,), (1,)), ((), ()))
_TN = (((0,), (0,)), ((), ()))


def _dot(a, b, dims):
    return lax.dot_general(a, b, dims, preferred_element_type=jnp.result_type(a.dtype, jnp.float32))


def _split(x, n):
    parts = []
    for _ in range(n - 1):
        p = x.astype(BF)
        parts.append(p)
        x = x - p.astype(jnp.float32)
    parts.append(x.astype(BF))
    return parts


def bmm(a, b, dims=_NN):
    return _dot(a.astype(BF), b.astype(BF), dims)


def _expand(x, bdm):
    return jnp.where(bdm, jnp.concatenate([x] * N_HEADS, axis=0), jnp.zeros((), x.dtype))


def wmm(a, b, bdm, passes=1, nt=False):
    dims = _NT if nt else _NN
    if passes == 1:
        return _dot(a.astype(BF), _expand(b.astype(BF), bdm), dims)
    ah, al = _split(a, 2)
    bh, bl = _split(b, 2)
    ebh = _expand(bh, bdm)
    lhs = jnp.concatenate([ah, ah, al], axis=1)
    rhs = jnp.concatenate([ebh, _expand(bl, bdm), ebh], axis=1 if nt else 0)
    return _dot(lhs, rhs, dims)


def cumsum_col(mc, x):
    return _dot(jnp.concatenate([mc.astype(BF)] * 3, axis=1), jnp.concatenate(_split(x, 3), axis=0), _NN)


def cumsum_row(x, mc):
    return _dot(jnp.concatenate(_split(x, 3), axis=1), jnp.concatenate([mc.astype(BF)] * 3, axis=1), _NT)


def head_sum(x, bd):
    return _dot(jnp.concatenate(_split(x, 3), axis=1), jnp.concatenate([bd.astype(BF)] * 3, axis=0), _NN)


def sigmoid(x):
    return 1.0 / (1.0 + jnp.exp(-x))


def silu(x):
    return x * sigmoid(x)


def softplus(x):
    return jnp.maximum(x, 0.0) + jnp.log(1.0 + jnp.exp(-jnp.abs(x)))


def log_sigmoid(x):
    return -softplus(-x)


def head_norm(y, gain, bd, center):
    if center:
        y = y - head_sum(y, bd) * (1.0 / HEAD_DIM)
    ms = head_sum(y * y, bd) * (1.0 / HEAD_DIM)
    return y * lax.rsqrt(ms + LN_EPS) * gain


def l2norm_heads(x, bd):
    return x * lax.rsqrt(head_sum(x * x, bd) + 1e-6)


def layer_norm(x):
    mu = jnp.mean(x, axis=-1, keepdims=True)
    xc = x - mu
    var = jnp.mean(xc * xc, axis=-1, keepdims=True)
    return xc * lax.rsqrt(var + LN_EPS)


def colw(x, idx):
    m = x.shape[0]
    return jnp.concatenate([jnp.broadcast_to(x[:, j:j + 1], (m, HEAD_DIM)) for j in idx], axis=1)


def roww(x, idx):
    return jnp.concatenate([x[j:j + 1, :] for j in idx], axis=1)


def segmax(x):
    m = x.shape[0]
    return jnp.concatenate(
        [jnp.broadcast_to(jnp.max(x[:, h * HEAD_DIM:(h + 1) * HEAD_DIM], axis=1, keepdims=True), (m, HEAD_DIM))
         for h in range(N_HEADS)], axis=1)


def wide_masks(rev, dtype=jnp.float32):
    r = lax.broadcasted_iota(jnp.int32, (CHUNK, GROUP_W), 0)
    c = jnp.bitwise_and(lax.broadcasted_iota(jnp.int32, (CHUNK, GROUP_W), 1), HEAD_DIM - 1)
    if rev:
        incl_w, strict_w = r <= c, r < c
    else:
        incl_w, strict_w = r >= c, r > c
    mc = incl_w[:, :CHUNK].astype(dtype)
    return incl_w, strict_w, mc, (r == c).astype(dtype)


def block_diag_mask():
    r = jnp.right_shift(lax.broadcasted_iota(jnp.int32, (GROUP_W, GROUP_W), 0), 6)
    c = jnp.right_shift(lax.broadcasted_iota(jnp.int32, (GROUP_W, GROUP_W), 1), 6)
    return r == c


def inv_unit_tri_w(a, eye_w, bdm):
    p = -a
    t = eye_w + p
    for _ in range(5):
        p = wmm(p, p, bdm, 3)
        t = t + wmm(t, p, bdm, 3)
    return t


def mlstm_chunk(q, k, v, bcol_w, brow_w, icol_w, irow_w, bl_w, c_bd, n_row, m_w, incl_w, bdm, bd):
    logd = jnp.where(incl_w, bcol_w - brow_w + irow_w, NEG_INF)
    inter = bcol_w + m_w
    mt = jnp.maximum(inter, segmax(logd))
    s = wmm(q, k, bdm, 1, nt=True) * jnp.exp(logd - mt)
    ew = jnp.exp(inter - mt)
    num = wmm(s, v, bdm, 1) + bmm(q, c_bd) * ew
    den = head_sum(s, bd) + head_sum(q * n_row, bd) * ew
    h = num / jnp.maximum(jnp.abs(den), jnp.exp(-mt))
    wk = bl_w - bcol_w + icol_w
    m_new = jnp.maximum(bl_w + m_w, jnp.max(wk, axis=0, keepdims=True))
    sk = jnp.exp(wk - m_new)
    sp = jnp.exp(bl_w + m_w - m_new)
    ks = k * sk
    c_new = c_bd * sp + jnp.where(bdm, bmm(ks, v, _TN), 0.0)
    n_new = n_row * sp + jnp.sum(ks, axis=0, keepdims=True)
    return h, c_new, n_new, m_new


def delta_chunk(q, k, v, beta_w, gcol_w, grow_w, gl_w, s_bd, incl_w, strict_w, eye_w, bdm):
    decay = jnp.exp(jnp.where(incl_w, gcol_w - grow_w, NEG_INF))
    a = jnp.where(strict_w, wmm(k, k, bdm, 1, nt=True) * decay, 0.0) * beta_w
    t = inv_unit_tri_w(a, eye_w, bdm)
    eg = jnp.exp(gcol_w)
    u = wmm(t, v * beta_w, bdm, 3)
    w = wmm(t, k * (beta_w * eg), bdm, 3)
    v_new = u - bmm(w, s_bd)
    qk = wmm(q, k, bdm, 1, nt=True) * decay
    o = bmm(q * eg, s_bd) + wmm(qk, v_new, bdm, 1)
    s_new = s_bd * jnp.exp(gl_w) + jnp.where(bdm, bmm(k * jnp.exp(gl_w - gcol_w), v_new, _TN), 0.0)
    return o, s_new


def retention_chunk(q, k, v, dmat_w, xi_w, zeta_w, gl_w, s_bd, bdm):
    sc = wmm(q, k, bdm, 1, nt=True) * dmat_w
    o = wmm(sc, v, bdm, 1) + bmm(q, s_bd) * xi_w
    s_new = s_bd * gl_w + jnp.where(bdm, bmm(k * zeta_w, v, _TN), 0.0)
    return o, s_new


def rwkv_chunk(rp, khp, ktn, bn, kte, be, v, etot, s_bd, incl_w, strict_w, eye_w, bdm):
    lhs = jnp.concatenate([khp, rp], axis=0)
    ab_kt = wmm(lhs, ktn, bdm, 1, nt=True)
    ab_b = wmm(lhs, bn, bdm, 1, nt=True)
    a_kt = jnp.where(strict_w, ab_kt[:CHUNK], 0.0)
    b_kt = jnp.where(incl_w, ab_kt[CHUNK:], 0.0)
    a_b = jnp.where(strict_w, ab_b[:CHUNK], 0.0)
    b_b = jnp.where(incl_w, ab_b[CHUNK:], 0.0)
    t = inv_unit_tri_w(a_b, eye_w, bdm)
    ls = bmm(lhs, s_bd, _NT)
    u = wmm(t, ls[:CHUNK] + wmm(a_kt, v, bdm, 1), bdm, 3)
    o = ls[CHUNK:] + wmm(b_kt, v, bdm, 1) - wmm(b_b, u, bdm, 1)
    upd = bmm(jnp.concatenate([v, -u], axis=0), jnp.concatenate([kte, be], axis=0), _TN)
    s_new = s_bd * etot + jnp.where(bdm, upd, 0.0)
    return o, s_new


def _two_pass(nc, body):
    half = nc // 2

    def first(i, carry):
        body(i, True)
        return carry

    def second(i, carry):
        body(i, False)
        return carry

    lax.fori_loop(0, half, first, 0)
    lax.fori_loop(half, nc, second, 0)


def _chunk_pos(i, z, nc):
    c = i if z == 0 else nc - 1 - i
    return c, pl.multiple_of(c * CHUNK, CHUNK)


def _halo(ref, c, t0, nc, width):
    t_len = nc * CHUNK
    ps = pl.multiple_of(jnp.maximum(t0 - 8, 0), 8)
    ns = pl.multiple_of(jnp.minimum(t0 + CHUNK, t_len - 8), 8)
    prev = ref[pl.ds(ps, 8), 0:width]
    nxt = ref[pl.ds(ns, 8), 0:width]
    prev = jnp.where(c > 0, prev, 0.0)
    nxt = jnp.where(c < nc - 1, nxt, 0.0)
    cur = ref[pl.ds(t0, CHUNK), 0:width]
    return jnp.concatenate([prev, cur, nxt], axis=0)


def _hs(h):
    return slice(h * HEAD_DIM, (h + 1) * HEAD_DIM)


def _load_state(sbd, s0_ref):
    for z in range(2):
        sbd[z] = jnp.zeros((GROUP_W, GROUP_W), jnp.float32)
        for h in range(N_HEADS):
            sbd[z, _hs(h), _hs(h)] = s0_ref[0, 0, z, h]


def _store_state(s1_ref, sbd):
    for z in range(2):
        for h in range(N_HEADS):
            s1_ref[0, z, h] = sbd[z, _hs(h), _hs(h)]


def _merge(first, hscr, t0, h):
    if first:
        hscr[pl.ds(t0, CHUNK), :] = h
        return None
    return hscr[pl.ds(t0, CHUNK), :] + h


def mlstm_kernel(p_ref, gc_ref, gr_ref, bc_ref, br_ref, norm_ref, c0_ref, n0_ref, m0_ref,
                 y_ref, c1_ref, n1_ref, m1_ref, hscr, cbd, nrow, mw, *, nc):
    _load_state(cbd, c0_ref)
    for z in range(2):
        for h in range(N_HEADS):
            nrow[z, :, _hs(h)] = n0_ref[0, 0, z, h]
            mw[z, :, _hs(h)] = jnp.broadcast_to(m0_ref[0, 0, z, h], (1, HEAD_DIM))
    bdm = block_diag_mask()
    bd = bdm.astype(jnp.float32)
    masks = [wide_masks(False), wide_masks(True)]

    def body(i, first):
        for z in range(2):
            incl_w, _, mc, _ = masks[z]
            c, t0 = _chunk_pos(i, z, nc)
            x = p_ref[pl.ds(t0, CHUNK), :]
            g_c = gc_ref[pl.ds(t0, CHUNK), :] + bc_ref[...]
            g_r = gr_ref[c] + br_ref[...]
            lf_c = log_sigmoid(g_c)
            b_c = cumsum_col(mc, lf_c)
            b_r = cumsum_row(log_sigmoid(g_r), mc)
            bl = jnp.sum(lf_c, axis=0, keepdims=True)
            idx_i = [z * N_HEADS + h for h in range(N_HEADS)]
            idx_f = [8 + j for j in idx_i]
            hh, c_new, n_new, m_new = mlstm_chunk(
                x[:, :GROUP_W], x[:, GROUP_W:2 * GROUP_W] * HEAD_DIM ** -0.5, x[:, 2 * GROUP_W:3 * GROUP_W],
                colw(b_c, idx_f), roww(b_r, idx_f), colw(g_c, idx_i), roww(g_r, idx_i), colw(bl, idx_f),
                cbd[z], nrow[z], mw[z], incl_w, bdm, bd)
            cbd[z] = c_new
            nrow[z] = n_new
            mw[z] = m_new
            tot = _merge(first, hscr, t0, hh)
            if not first:
                y_ref[pl.ds(t0, CHUNK), :] = sigmoid(x[:, 3 * GROUP_W:]) * head_norm(tot, norm_ref[...], bd, True)

    _two_pass(nc, body)
    _store_state(c1_ref, cbd)
    for z in range(2):
        for h in range(N_HEADS):
            n1_ref[0, z, h] = nrow[z, :, _hs(h)]
            m1_ref[0, z, h] = mw[z, :, h * HEAD_DIM:h * HEAD_DIM + 1]


def delta_kernel(p_ref, gc_ref, gr_ref, conv_ref, prow_ref, pcol_ref, norm_ref, s0_ref,
                 y_ref, s1_ref, hscr, sbd, *, nc):
    _load_state(sbd, s0_ref)
    bdm = block_diag_mask()
    bd = bdm.astype(jnp.float32)
    masks = [wide_masks(False), wide_masks(True)]
    nega_row = -jnp.exp(prow_ref[0:1, :])
    dtb_row = prow_ref[1:2, :]
    nega_col = -jnp.exp(pcol_ref[:, 0:1])
    dtb_col = pcol_ref[:, 1:2]
    qkv_w = 3 * GROUP_W

    def body(i, first):
        for z in range(2):
            incl_w, strict_w, mc, eye_w = masks[z]
            c, t0 = _chunk_pos(i, z, nc)
            ext = _halo(p_ref, c, t0, nc, qkv_w)
            acc = ext[6:6 + CHUNK] * conv_ref[0:1, :]
            for j in range(1, CONV_K):
                acc = acc + ext[6 + j:6 + j + CHUNK] * conv_ref[j:j + 1, :]
            qkv = silu(acc)
            q = l2norm_heads(qkv[:, :GROUP_W], bd) * HEAD_DIM ** -0.5
            k = l2norm_heads(qkv[:, GROUP_W:2 * GROUP_W], bd)
            v = qkv[:, 2 * GROUP_W:]
            g_c = gc_ref[pl.ds(t0, CHUNK), :]
            g_r = gr_ref[c]
            gg_c = nega_row * softplus(g_c + dtb_row)
            gg_r = nega_col * softplus(g_r + dtb_col)
            gcum_c = cumsum_col(mc, gg_c)
            gcum_r = cumsum_row(gg_r, mc)
            gl = jnp.sum(gg_c, axis=0, keepdims=True)
            idx_b = [16 + z * N_HEADS + h for h in range(N_HEADS)]
            idx_a = [8 + j for j in idx_b]
            o, s_new = delta_chunk(q, k, v, colw(sigmoid(g_c), idx_b), colw(gcum_c, idx_a), roww(gcum_r, idx_a),
                                   colw(gl, idx_a), sbd[z], incl_w, strict_w, eye_w, bdm)
            sbd[z] = s_new
            tot = _merge(first, hscr, t0, o)
            if not first:
                zg = p_ref[pl.ds(t0, CHUNK), qkv_w:]
                y_ref[pl.ds(t0, CHUNK), :] = head_norm(tot, norm_ref[...], bd, False) * silu(zg)

    _two_pass(nc, body)
    _store_state(s1_ref, sbd)


def _rope(x, cos, sin, lane_lo):
    n = x.shape[1]
    swapped = jnp.where(lane_lo, pltpu.roll(x, n - 16, 1), pltpu.roll(x, 16, 1))
    return x * cos + swapped * sin


def ret_kernel(*refs, nc, rope):
    if rope:
        p_ref, cos_ref, sin_ref, rd_ref, norm_ref, s0_ref, y_ref, s1_ref, hscr, sbd, cst = refs
    else:
        p_ref, rd_ref, norm_ref, s0_ref, y_ref, s1_ref, hscr, sbd, cst = refs
    f32 = jnp.float32
    _load_state(sbd, s0_ref)
    bdm = block_diag_mask()
    bd = bdm.astype(f32)
    r = lax.broadcasted_iota(jnp.int32, (CHUNK, GROUP_W), 0).astype(f32)
    cc = jnp.bitwise_and(lax.broadcasted_iota(jnp.int32, (CHUNK, GROUP_W), 1), HEAD_DIM - 1).astype(f32)
    lg_row = -jnp.exp(rd_ref[...])
    for z in range(2):
        incl_w = wide_masks(z == 1)[0]
        lg = colw(lg_row, [z * N_HEADS + h for h in range(N_HEADS)])
        if z == 0:
            cst[z, 0] = jnp.exp(jnp.where(incl_w, (r - cc) * lg, NEG_INF))
            cst[z, 1] = jnp.exp((r + 1.0) * lg)
            cst[z, 2] = jnp.exp((CHUNK - 1.0 - r) * lg)
        else:
            cst[z, 0] = jnp.exp(jnp.where(incl_w, (cc - r) * lg, NEG_INF))
            cst[z, 1] = jnp.exp((CHUNK - r) * lg)
            cst[z, 2] = jnp.exp(r * lg)
        cst[z, 3] = jnp.exp(CHUNK * lg + 0.0 * r)
    if rope:
        lane = lax.broadcasted_iota(jnp.int32, (CHUNK, GROUP_W), 1)
        lane_lo = jnp.bitwise_and(lane, 31) < 16

    def body(i, first):
        for z in range(2):
            c, t0 = _chunk_pos(i, z, nc)
            x = p_ref[pl.ds(t0, CHUNK), :]
            q = x[:, :GROUP_W] * HEAD_DIM ** -0.5
            k = x[:, GROUP_W:2 * GROUP_W]
            v = x[:, 2 * GROUP_W:3 * GROUP_W]
            if rope:
                cos = cos_ref[pl.ds(t0, CHUNK), :]
                sin = sin_ref[pl.ds(t0, CHUNK), :]
                q = _rope(q, cos, sin, lane_lo)
                k = _rope(k, cos, sin, lane_lo)
            o, s_new = retention_chunk(q, k, v, cst[z, 0], cst[z, 1], cst[z, 2], cst[z, 3][0:1, :], sbd[z], bdm)
            sbd[z] = s_new
            tot = _merge(first, hscr, t0, o)
            if not first:
                y_ref[pl.ds(t0, CHUNK), :] = silu(x[:, 3 * GROUP_W:]) * head_norm(tot, norm_ref[...], bd, True)

    _two_pass(nc, body)
    _store_state(s1_ref, sbd)


def rwkv_kernel(p_ref, mu_ref, w0_ref, w2_ref, a0_ref, a2_ref, g2_ref, vec_ref, s0_ref,
                y_ref, s1_ref, hscr, sbd, *, nc):
    _load_state(sbd, s0_ref)
    bdm = block_diag_mask()
    bd = bdm.astype(jnp.float32)
    masks = [wide_masks(False), wide_masks(True)]
    width = p_ref.shape[1]
    kk = vec_ref[0:1, :]
    ka = vec_ref[1:2, :]
    rk = vec_ref[2:3, :]
    gain = vec_ref[3:4, :]

    def lora_a(da, zz):
        return sigmoid(a0_ref[zz:zz + 1, :] + bmm(da[:, zz * RWKV_LORA:(zz + 1) * RWKV_LORA], a2_ref[zz]))

    def body(i, first):
        for z in range(2):
            incl_w, strict_w, mc, eye_w = masks[z]
            c, t0 = _chunk_pos(i, z, nc)
            ext = _halo(p_ref, c, t0, nc, width)
            cur = ext[8:8 + CHUNK]
            pdm = cur + mu_ref[...] * (0.5 * (ext[7:7 + CHUNK] + ext[9:9 + CHUNK]) - cur)
            r = pdm[:, :GROUP_W]
            k = pdm[:, GROUP_W:2 * GROUP_W]
            v = pdm[:, 2 * GROUP_W:3 * GROUP_W]
            dw = pdm[:, 3 * GROUP_W:3 * GROUP_W + 128]
            da = pdm[:, 3 * GROUP_W + 128:3 * GROUP_W + 256]
            a_z = lora_a(da, z)
            w_pre = w0_ref[z:z + 1, :] + bmm(jnp.tanh(dw[:, z * RWKV_LORA:(z + 1) * RWKV_LORA]), w2_ref[z])
            lw = -jnp.exp(-softplus(-w_pre) - 0.5)
            kh = l2norm_heads(k * kk, bd)
            kt_z = k * (1.0 + (a_z - 1.0) * ka)
            b = a_z * kh
            cw = cumsum_col(mc, lw)
            tot = jnp.sum(lw, axis=0, keepdims=True)
            en = jnp.exp(-cw)
            et = jnp.exp(tot - cw)
            o, s_new = rwkv_chunk(r * jnp.exp(cw), kh * jnp.exp(cw - lw), kt_z * en, b * en, kt_z * et, b * et,
                                  v, jnp.exp(tot), sbd[z], incl_w, strict_w, eye_w, bdm)
            sbd[z] = s_new
            tot_h = _merge(first, hscr, t0, o)
            if not first:
                kt_o = k * (1.0 + (lora_a(da, 1 - z) - 1.0) * ka)
                bonus = head_sum(r * (kt_z + kt_o) * rk, bd) * v
                g = bmm(sigmoid(pdm[:, 3 * GROUP_W + 256:]), g2_ref[...])
                y_ref[pl.ds(t0, CHUNK), :] = g * (head_norm(tot_h, gain, bd, True) + bonus)

    _two_pass(nc, body)
    _store_state(s1_ref, sbd)


def mod_kernel(c_ref, w_ref, b_ref, o_ref):
    o_ref[0] = bmm(silu(c_ref[...]), w_ref[0]) + b_ref[0]


def proj_kernel(x_ref, mod_ref, wa, wb, wc, wd, wg, wgt, pa, pb, pc, pd, pg, pgt):
    m = mod_ref[0]
    u = layer_norm(x_ref[...]) * (1.0 + m[1:2]) + m[0:1]
    ub = u.astype(BF)
    pa[...] = _dot(ub, wa[...], _NN)
    pb[...] = _dot(ub, wb[...], _NN)
    pc[...] = _dot(ub, wc[...], _NN)
    pd[...] = _dot(ub, wd[...], _NN)
    pg[...] = _dot(ub, wg[...], _NN)
    gt = _dot(wgt[...], ub, _NT)
    for j in range(ROW_TILE // CHUNK):
        pgt[j] = gt[:, j * CHUNK:(j + 1) * CHUNK]


def post_kernel(h_ref, ya, yb, yc, yd, mod_ref, wout, l1g, l1b, w1, w2, l2g, l2b, o_ref):
    m = mod_ref[0]
    y = jnp.concatenate([ya[...], yb[...], yc[...], yd[...]], axis=1).astype(BF)
    attn = _dot(y, wout[...], _NN)
    h1 = layer_norm(DEEPNORM_ALPHA * h_ref[...] + m[2:3] * attn) * l1g[...] + l1b[...]
    u2 = layer_norm(h1) * (1.0 + m[4:5]) + m[3:4]
    f = _dot(u2.astype(BF), w1[...], _NN)
    f = jnp.square(jnp.maximum(f, 0.0))
    f2 = _dot(f.astype(BF), w2[...], _NN)
    o_ref[...] = layer_norm(DEEPNORM_ALPHA * h1 + m[5:6] * f2) * l2g[...] + l2b[...]


def _params(n_grid_axes):
    return pltpu.CompilerParams(dimension_semantics=("arbitrary",) * n_grid_axes,
                                vmem_limit_bytes=VMEM_LIMIT)


def _const_spec(a):
    nd = a.ndim
    return pl.BlockSpec(a.shape, lambda *_: (0,) * nd)


def _f32(shape):
    return jax.ShapeDtypeStruct(shape, jnp.float32)


def _mod_call(c_all, w_mod, b_mod):
    depth, d, n = w_mod.shape
    tn = 512
    return pl.pallas_call(
        mod_kernel, grid=(depth, n // tn),
        in_specs=[pl.BlockSpec(c_all.shape, lambda l, j: (0, 0)),
                  pl.BlockSpec((1, d, tn), lambda l, j: (l, 0, j)),
                  pl.BlockSpec((1, 1, tn), lambda l, j: (l, 0, j))],
        out_specs=pl.BlockSpec((1, c_all.shape[0], tn), lambda l, j: (l, 0, j)),
        out_shape=_f32((depth, c_all.shape[0], n)),
        compiler_params=_params(2), name="mod")(c_all, w_mod, b_mod.reshape(depth, 1, n))


def _mod_spec(n_ptiles, tiles_per_dec):
    def index(i):
        return (jnp.where(i < n_ptiles, 0, 1 + (i - n_ptiles) // tiles_per_dec), 0, 0)
    return pl.BlockSpec((1, 6, D_MODEL), index)


def _proj_call(x_all, mod_l, ws, n_ptiles, tiles_per_dec):
    n_rows = x_all.shape[0]
    row = lambda i: (i, 0)
    widths = [w.shape[1] for w in ws[:5]]
    per_tile = ROW_TILE // CHUNK
    return pl.pallas_call(
        proj_kernel, grid=(n_rows // ROW_TILE,),
        in_specs=[pl.BlockSpec((ROW_TILE, D_MODEL), row), _mod_spec(n_ptiles, tiles_per_dec)]
                 + [_const_spec(w) for w in ws],
        out_specs=[pl.BlockSpec((ROW_TILE, w), row) for w in widths]
                  + [pl.BlockSpec((per_tile, N_GATES, CHUNK), lambda i: (i, 0, 0))],
        out_shape=[_f32((n_rows, w)) for w in widths] + [_f32((n_rows // CHUNK, N_GATES, CHUNK))],
        compiler_params=_params(1), name="proj")(x_all, mod_l, *ws)


def _post_call(h_all, ys, mod_l, ws, n_ptiles, tiles_per_dec):
    n_rows = h_all.shape[0]
    row = lambda i: (i, 0)
    return pl.pallas_call(
        post_kernel, grid=(n_rows // ROW_TILE,),
        in_specs=[pl.BlockSpec((ROW_TILE, D_MODEL), row)]
                 + [pl.BlockSpec((ROW_TILE, GROUP_W), row) for _ in ys]
                 + [_mod_spec(n_ptiles, tiles_per_dec)] + [_const_spec(w) for w in ws],
        out_specs=pl.BlockSpec((ROW_TILE, D_MODEL), row),
        out_shape=_f32((n_rows, D_MODEL)),
        compiler_params=_params(1), name="post")(h_all, *ys, mod_l, *ws)


def _state_spec(shape_tail, layer):
    nd = len(shape_tail)
    return pl.BlockSpec((1, 1) + shape_tail, lambda i: (i, layer) + (0,) * nd)


def _out_state_spec(shape_tail):
    nd = len(shape_tail)
    return pl.BlockSpec((1,) + shape_tail, lambda i: (i,) + (0,) * nd)


MAT = (2, N_HEADS, HEAD_DIM, HEAD_DIM)
VEC = (2, N_HEADS, 1, HEAD_DIM)
SCL = (2, N_HEADS, 1, 1)
BD_STATE = pltpu.VMEM((2, GROUP_W, GROUP_W), jnp.float32)


def _mixer_call(kernel, name, t_len, n_seq, blk_off, seq_inputs, const_inputs, state_inputs, layer,
                state_tails, scratch):
    nc = t_len // CHUNK
    in_specs = []
    for a in seq_inputs:
        if a.ndim == 2:
            in_specs.append(pl.BlockSpec((t_len, a.shape[1]), lambda i: (i + blk_off, 0)))
        else:
            in_specs.append(pl.BlockSpec((nc,) + a.shape[1:], lambda i: (i + blk_off, 0, 0)))
    in_specs += [_const_spec(a) for a in const_inputs]
    in_specs += [_state_spec(a.shape[2:], layer) for a in state_inputs]
    out_specs = [pl.BlockSpec((t_len, GROUP_W), lambda i: (i, 0))] + [_out_state_spec(t) for t in state_tails]
    out_shape = [_f32((n_seq * t_len, GROUP_W))] + [_f32((n_seq,) + t) for t in state_tails]
    return pl.pallas_call(
        kernel, grid=(n_seq,), in_specs=in_specs, out_specs=out_specs, out_shape=out_shape,
        scratch_shapes=[pltpu.VMEM((t_len, GROUP_W), jnp.float32)] + scratch,
        compiler_params=_params(1), name=name)(*seq_inputs, *const_inputs, *state_inputs)


def _rope_tables(t_len):
    nf = HEAD_DIM // 4
    inv = ROPE_BASE ** (-jnp.arange(nf, dtype=jnp.float32) / nf)
    t = jnp.arange(t_len)
    ang_r = (t // GRID_W).astype(jnp.float32)[:, None] * inv[None, :]
    ang_c = (t % GRID_W).astype(jnp.float32)[:, None] * inv[None, :]
    cos = jnp.concatenate([jnp.cos(ang_r)] * 2 + [jnp.cos(ang_c)] * 2, axis=1)
    sin = jnp.concatenate([-jnp.sin(ang_r), jnp.sin(ang_r), -jnp.sin(ang_c), jnp.sin(ang_c)], axis=1)
    return jnp.tile(cos, (1, N_HEADS)), jnp.tile(sin, (1, N_HEADS))


def _lane_row(vals, lo, n_rows=1, row=0, width=GATE_LANES):
    out = jnp.zeros((n_rows, width), jnp.float32)
    return out.at[row, lo:lo + vals.size].set(vals.reshape(-1))


def kernel(x_prompt, x_sample, c, state_mlstm_C, state_mlstm_n, state_mlstm_m, state_delta, state_ret,
           state_rwkv, c_ctx, w_mod, b_mod, w_in, w_out, ln1_g, ln1_b, ln2_g, ln2_b, w_ff1, w_ff2,
           mlstm_i_bias, mlstm_f_bias, mlstm_norm, delta_conv, delta_a_log, delta_dt_bias, delta_norm,
           ret_decay, ret_norm, rwkv_mu, rwkv_w0, rwkv_w2, rwkv_a0, rwkv_a2, rwkv_g2, rwkv_kk, rwkv_ka,
           rwkv_rk, rwkv_norm):
    f32 = jnp.float32
    batch, seq, d = x_prompt.shape
    dec_batch, dec_seq, _ = x_sample.shape
    assert d == D_MODEL and seq % ROW_TILE == 0 and dec_seq % ROW_TILE == 0
    assert (seq // CHUNK) % 2 == 0 and (dec_seq // CHUNK) % 2 == 0 and (batch * seq) % dec_seq == 0
    n_prows = batch * seq
    n_ptiles = n_prows // ROW_TILE
    tiles_per_dec = dec_seq // ROW_TILE

    c_all = jnp.zeros((8, d), f32).at[0].set(c_ctx).at[1:1 + dec_batch].set(c)
    mod = _mod_call(c_all, w_mod, b_mod).reshape(DEPTH, 8, 6, d)
    h_all = jnp.concatenate([x_prompt.reshape(n_prows, d), x_sample.reshape(dec_batch * dec_seq, d)], axis=0)
    cos_t, sin_t = _rope_tables(dec_seq)

    zmat = jnp.zeros((batch, 1) + MAT, f32)
    zn = jnp.zeros((batch, 1) + VEC, f32)
    zm = jnp.zeros((batch, 1) + SCL, f32)
    n_cache = state_mlstm_n.reshape(state_mlstm_n.shape[:-1] + (1, HEAD_DIM))
    m_cache = state_mlstm_m.reshape(state_mlstm_m.shape + (1, 1))
    groups = ((seq, batch, 0), (dec_seq, dec_batch, n_prows // dec_seq))

    new_states = []
    for l in range(DEPTH):
        wl = w_in[l]
        ws = [wl[:, 0:1024], wl[:, 1040:2064], wl[:, 2080:3104], wl[:, 3104:4256],
              jnp.pad(jnp.concatenate([wl[:, 1024:1040], wl[:, 2064:2080]], axis=1), ((0, 0), (0, GATE_LANES - N_GATES)))]
        ws = [w.astype(BF) for w in ws]
        ws.append(ws[4][:, :N_GATES].T)
        pa, pb, pc, pd, pg, pgt = _proj_call(h_all, mod[l], ws, n_ptiles, tiles_per_dec)

        bias_c = _lane_row(mlstm_i_bias[l], 0) + _lane_row(mlstm_f_bias[l], 8)
        bias_r = bias_c[0, :N_GATES].reshape(N_GATES, 1)
        dl_row = _lane_row(delta_a_log[l], 24, 2, 0) + _lane_row(delta_dt_bias[l], 24, 2, 1)
        dl_col = dl_row[:, :N_GATES].T
        rd_row = _lane_row(ret_decay[l], 0)
        rw_vec = jnp.stack([rwkv_kk[l], rwkv_ka[l], rwkv_rk[l].reshape(-1), rwkv_norm[l]], axis=0)
        row = lambda a: a.reshape(1, -1)

        ys = [[], [], [], []]
        layer_states = None
        for gi, (t_len, n_seq, off) in enumerate(groups):
            nc = t_len // CHUNK
            ctx = gi == 0
            lay = 0 if ctx else l
            ya, c1, n1, m1 = _mixer_call(
                functools.partial(mlstm_kernel, nc=nc), "mlstm", t_len, n_seq, off, [pa, pg, pgt],
                [bias_c, bias_r, row(mlstm_norm[l])],
                [zmat, zn, zm] if ctx else [state_mlstm_C, n_cache, m_cache], lay, [MAT, VEC, SCL],
                [BD_STATE, pltpu.VMEM((2, 1, GROUP_W), f32), pltpu.VMEM((2, 1, GROUP_W), f32)])
            yb, sb1 = _mixer_call(
                functools.partial(delta_kernel, nc=nc), "delta", t_len, n_seq, off, [pb, pg, pgt],
                [delta_conv[l], dl_row, dl_col, row(delta_norm[l])],
                [zmat if ctx else state_delta], lay, [MAT], [BD_STATE])
            ret_consts = [rd_row, row(ret_norm[l])]
            if not ctx:
                ret_consts = [cos_t, sin_t] + ret_consts
            yc, sc1 = _mixer_call(
                functools.partial(ret_kernel, nc=nc, rope=not ctx), "ret", t_len, n_seq, off, [pc],
                ret_consts, [zmat if ctx else state_ret], lay, [MAT],
                [BD_STATE, pltpu.VMEM((2, 4, CHUNK, GROUP_W), f32)])
            yd, sd1 = _mixer_call(
                functools.partial(rwkv_kernel, nc=nc), "rwkv", t_len, n_seq, off, [pd],
                [row(rwkv_mu[l]), rwkv_w0[l], rwkv_w2[l], rwkv_a0[l], rwkv_a2[l], rwkv_g2[l], rw_vec],
                [zmat if ctx else state_rwkv], lay, [MAT], [BD_STATE])
            for lst, y in zip(ys, (ya, yb, yc, yd)):
                lst.append(y)
            if ctx:
                layer_states = (c1, n1.reshape(n1.shape[:3] + (HEAD_DIM,)), m1.reshape(m1.shape[:3]), sb1, sc1, sd1)
        new_states.append(layer_states)
        ys = [jnp.concatenate(y, axis=0) for y in ys]
        post_ws = [w_out[l].astype(BF), row(ln1_g[l]), row(ln1_b[l]), w_ff1[l].astype(BF), w_ff2[l].astype(BF),
                   row(ln2_g[l]), row(ln2_b[l])]
        h_all = _post_call(h_all, ys, mod[l], post_ws, n_ptiles, tiles_per_dec)

    outs = [jnp.stack([st[i] for st in new_states], axis=1) for i in range(6)]
    return (h_all[:n_prows].reshape(batch, seq, d), h_all[n_prows:].reshape(dec_batch, dec_seq, d), *outs)
```

```python
import functools

import jax
import jax.numpy as jnp
from jax import lax
from jax.experimental import pallas as pl
from jax.experimental.pallas import tpu as pltpu

D_MODEL = 1024
DEPTH = 2
GRID_W = 64
HEAD_DIM = 64
N_HEADS = 4
GROUP_W = N_HEADS * HEAD_DIM
CHUNK = 64
CONV_K = 5
ROPE_BASE = 10000.0
LN_EPS = 1e-5
DEEPNORM_ALPHA = (2 * DEPTH) ** 0.25
RWKV_LORA = 64
GATE_LANES = 128
N_GATES = 32
ROW_TILE = 256
VMEM_LIMIT = 56 * 1024 * 1024
SCAN_STEPS = 2

NEG_INF = float("-inf")
BF = jnp.bfloat16
_NN = (((1,), (0,)), ((), ()))
_NT = (((1,), (1,)), ((), ()))
_TN = (((0,), (0,)), ((), ()))


def _dot(a, b, dims):
    return lax.dot_general(a, b, dims, preferred_element_type=jnp.result_type(a.dtype, jnp.float32))


def _split(x, n):
    parts = []
    for _ in range(n - 1):
        p = x.astype(BF)
        parts.append(p)
        x = x - p.astype(jnp.float32)
    parts.append(x.astype(BF))
    return parts


def bmm(a, b, dims=_NN):
    return _dot(a.astype(BF), b.astype(BF), dims)


def _expand(x, bdm):
    return jnp.where(bdm, jnp.concatenate([x] * N_HEADS, axis=0), jnp.zeros((), x.dtype))


def wmm(a, b, bdm, passes=1, nt=False):
    dims = _NT if nt else _NN
    if passes == 1:
        return _dot(a.astype(BF), _expand(b.astype(BF), bdm), dims)
    ah, al = _split(a, 2)
    bh, bl = _split(b, 2)
    ebh = _expand(bh, bdm)
    lhs = jnp.concatenate([ah, ah, al], axis=1)
    rhs = jnp.concatenate([ebh, _expand(bl, bdm), ebh], axis=1 if nt else 0)
    return _dot(lhs, rhs, dims)


def cumsum_col(mc, x):
    return _dot(jnp.concatenate([mc.astype(BF)] * 3, axis=1), jnp.concatenate(_split(x, 3), axis=0), _NN)


def cumsum_row(x, mc):
    return _dot(jnp.concatenate(_split(x, 3), axis=1), jnp.concatenate([mc.astype(BF)] * 3, axis=1), _NT)


def head_sum(x, bd):
    return _dot(jnp.concatenate(_split(x, 3), axis=1), jnp.concatenate([bd.astype(BF)] * 3, axis=0), _NN)


def sigmoid(x):
    return 1.0 / (1.0 + jnp.exp(-x))


def silu(x):
    return x * sigmoid(x)


def softplus(x):
    return jnp.maximum(x, 0.0) + jnp.log(1.0 + jnp.exp(-jnp.abs(x)))


def log_sigmoid(x):
    return -softplus(-x)


def head_norm(y, gain, bd, center):
    if center:
        y = y - head_sum(y, bd) * (1.0 / HEAD_DIM)
    ms = head_sum(y * y, bd) * (1.0 / HEAD_DIM)
    return y * lax.rsqrt(ms + LN_EPS) * gain


def l2norm_heads(x, bd):
    return x * lax.rsqrt(head_sum(x * x, bd) + 1e-6)


def layer_norm(x):
    mu = jnp.mean(x, axis=-1, keepdims=True)
    xc = x - mu
    var = jnp.mean(xc * xc, axis=-1, keepdims=True)
    return xc * lax.rsqrt(var + LN_EPS)


def colw(x, idx):
    m = x.shape[0]
    return jnp.concatenate([jnp.broadcast_to(x[:, j:j + 1], (m, HEAD_DIM)) for j in idx], axis=1)


def roww(x, idx):
    return jnp.concatenate([x[j:j + 1, :] for j in idx], axis=1)


def segmax(x):
    m = x.shape[0]
    return jnp.concatenate(
        [jnp.broadcast_to(jnp.max(x[:, h * HEAD_DIM:(h + 1) * HEAD_DIM], axis=1, keepdims=True), (m, HEAD_DIM))
         for h in range(N_HEADS)], axis=1)


def wide_masks(rev, dtype=jnp.float32):
    r = lax.broadcasted_iota(jnp.int32, (CHUNK, GROUP_W), 0)
    c = jnp.bitwise_and(lax.broadcasted_iota(jnp.int32, (CHUNK, GROUP_W), 1), HEAD_DIM - 1)
    if rev:
        incl_w, strict_w = r <= c, r < c
    else:
        incl_w, strict_w = r >= c, r > c
    mc = incl_w[:, :CHUNK].astype(dtype)
    return incl_w, strict_w, mc, (r == c).astype(dtype)


def block_diag_mask():
    r = jnp.right_shift(lax.broadcasted_iota(jnp.int32, (GROUP_W, GROUP_W), 0), 6)
    c = jnp.right_shift(lax.broadcasted_iota(jnp.int32, (GROUP_W, GROUP_W), 1), 6)
    return r == c


def inv_unit_tri_w(a, eye_w, bdm):
    p = -a
    t = eye_w + p
    for _ in range(5):
        p = wmm(p, p, bdm, 3)
        t = t + wmm(t, p, bdm, 3)
    return t


def mlstm_chunk(q, k, v, bcol_w, brow_w, icol_w, irow_w, bl_w, c_bd, n_row, m_w, incl_w, bdm, bd):
    logd = jnp.where(incl_w, bcol_w - brow_w + irow_w, NEG_INF)
    inter = bcol_w + m_w
    mt = jnp.maximum(inter, segmax(logd))
    s = wmm(q, k, bdm, 1, nt=True) * jnp.exp(logd - mt)
    ew = jnp.exp(inter - mt)
    num = wmm(s, v, bdm, 1) + bmm(q, c_bd) * ew
    den = head_sum(s, bd) + head_sum(q * n_row, bd) * ew
    h = num / jnp.maximum(jnp.abs(den), jnp.exp(-mt))
    wk = bl_w - bcol_w + icol_w
    m_new = jnp.maximum(bl_w + m_w, jnp.max(wk, axis=0, keepdims=True))
    sk = jnp.exp(wk - m_new)
    sp = jnp.exp(bl_w + m_w - m_new)
    ks = k * sk
    c_new = c_bd * sp + jnp.where(bdm, bmm(ks, v, _TN), 0.0)
    n_new = n_row * sp + jnp.sum(ks, axis=0, keepdims=True)
    return h, c_new, n_new, m_new


def delta_pre(q, k, v, beta_w, gcol_w, grow_w, gl_w, incl_w, strict_w, eye_w, bdm):
    decay = jnp.exp(jnp.where(incl_w, gcol_w - grow_w, NEG_INF))
    a = jnp.where(strict_w, wmm(k, k, bdm, 1, nt=True) * decay, 0.0) * beta_w
    t = inv_unit_tri_w(a, eye_w, bdm)
    eg = jnp.exp(gcol_w)
    u = wmm(t, v * beta_w, bdm, 3)
    w = wmm(t, k * (beta_w * eg), bdm, 3)
    qk = wmm(q, k, bdm, 1, nt=True) * decay
    return u, jnp.concatenate([w, q * eg], axis=0), qk, k * jnp.exp(gl_w - gcol_w), jnp.exp(gl_w)


def delta_step(u, wq, qk, kd, egl, s_bd, bdm):
    ws = bmm(wq, s_bd)
    v_new = u - ws[:CHUNK]
    o = ws[CHUNK:] + wmm(qk, v_new, bdm, 1)
    s_new = s_bd * egl + jnp.where(bdm, bmm(kd, v_new, _TN), 0.0)
    return o, s_new


def retention_chunk(q, k, v, dmat_w, xi_w, zeta_w, gl_w, s_bd, bdm):
    sc = wmm(q, k, bdm, 1, nt=True) * dmat_w
    o = wmm(sc, v, bdm, 1) + bmm(q, s_bd) * xi_w
    s_new = s_bd * gl_w + jnp.where(bdm, bmm(k * zeta_w, v, _TN), 0.0)
    return o, s_new


def rwkv_pre(rp, khp, ktn, bn, v, incl_w, strict_w, eye_w, bdm):
    lhs = jnp.concatenate([khp, rp], axis=0)
    ab_kt = wmm(lhs, ktn, bdm, 1, nt=True)
    ab_b = wmm(lhs, bn, bdm, 1, nt=True)
    a_kt = jnp.where(strict_w, ab_kt[:CHUNK], 0.0)
    b_kt = jnp.where(incl_w, ab_kt[CHUNK:], 0.0)
    a_b = jnp.where(strict_w, ab_b[:CHUNK], 0.0)
    b_b = jnp.where(incl_w, ab_b[CHUNK:], 0.0)
    t = inv_unit_tri_w(a_b, eye_w, bdm)
    tk = wmm(t, khp, bdm, 3)
    tav = wmm(t, wmm(a_kt, v, bdm, 1), bdm, 3)
    return jnp.concatenate([tk, rp], axis=0), tav, wmm(b_kt, v, bdm, 1), b_b


def rwkv_step(tkr, tav, o1, b_b, v, kbe, etot, s_bd, bdm):
    ls = bmm(tkr, s_bd, _NT)
    u = ls[:CHUNK] + tav
    o = ls[CHUNK:] + o1 - wmm(b_b, u, bdm, 1)
    upd = bmm(jnp.concatenate([v, -u], axis=0), kbe, _TN)
    s_new = s_bd * etot + jnp.where(bdm, upd, 0.0)
    return o, s_new


def _two_pass(nc, body):
    half = nc // 2

    def first(i, carry):
        body(i, True)
        return carry

    def second(i, carry):
        body(i, False)
        return carry

    lax.fori_loop(0, half, first, 0)
    lax.fori_loop(half, nc, second, 0)


def _chunk_pos(i, z, nc):
    c = i if z == 0 else nc - 1 - i
    return c, pl.multiple_of(c * CHUNK, CHUNK)


def _group_loop(nc, g, body):
    n_it = nc // g

    def run(first):
        def it(i, carry):
            body(i, first)
            return carry
        return it

    lax.fori_loop(0, n_it // 2, run(True), 0)
    lax.fori_loop(n_it // 2, n_it, run(False), 0)


def _group_chunks(i, g, nc):
    out = []
    for z in range(2):
        for j in range(g):
            step = i * g + j
            c = step if z == 0 else nc - 1 - step
            out.append((z, c, pl.multiple_of(c * CHUNK, CHUNK)))
    return out


def _group_masks(g):
    shape = (2 * g, CHUNK, GROUP_W)
    rev = lax.broadcasted_iota(jnp.int32, shape, 0) >= g
    r = lax.broadcasted_iota(jnp.int32, shape, 1)
    c = jnp.bitwise_and(lax.broadcasted_iota(jnp.int32, shape, 2), HEAD_DIM - 1)
    d = jnp.where(rev, c - r, r - c)
    return d >= 0, d > 0, (d == 0).astype(jnp.float32)


def _stack(items):
    return [jnp.stack(x) for x in zip(*items)]


def _pick(arrs, g, j):
    return [jnp.stack([a[j], a[g + j]]) for a in arrs]


def _halo(ref, c, t0, nc, width):
    t_len = nc * CHUNK
    ps = pl.multiple_of(jnp.maximum(t0 - 8, 0), 8)
    ns = pl.multiple_of(jnp.minimum(t0 + CHUNK, t_len - 8), 8)
    prev = ref[pl.ds(ps, 8), 0:width]
    nxt = ref[pl.ds(ns, 8), 0:width]
    prev = jnp.where(c > 0, prev, 0.0)
    nxt = jnp.where(c < nc - 1, nxt, 0.0)
    cur = ref[pl.ds(t0, CHUNK), 0:width]
    return jnp.concatenate([prev, cur, nxt], axis=0)


def _hs(h):
    return slice(h * HEAD_DIM, (h + 1) * HEAD_DIM)


def _load_state(sbd, s0_ref):
    for z in range(2):
        sbd[z] = jnp.zeros((GROUP_W, GROUP_W), jnp.float32)
        for h in range(N_HEADS):
            sbd[z, _hs(h), _hs(h)] = s0_ref[0, 0, z, h]


def _store_state(s1_ref, sbd):
    for z in range(2):
        for h in range(N_HEADS):
            s1_ref[0, z, h] = sbd[z, _hs(h), _hs(h)]


def _merge(first, hscr, t0, h):
    if first:
        hscr[pl.ds(t0, CHUNK), :] = h
        return None
    return hscr[pl.ds(t0, CHUNK), :] + h


def mlstm_kernel(p_ref, gc_ref, gr_ref, bc_ref, br_ref, norm_ref, c0_ref, n0_ref, m0_ref,
                 y_ref, c1_ref, n1_ref, m1_ref, hscr, cbd, nrow, mw, *, nc):
    _load_state(cbd, c0_ref)
    for z in range(2):
        for h in range(N_HEADS):
            nrow[z, :, _hs(h)] = n0_ref[0, 0, z, h]
            mw[z, :, _hs(h)] = jnp.broadcast_to(m0_ref[0, 0, z, h], (1, HEAD_DIM))
    bdm = block_diag_mask()
    bd = bdm.astype(jnp.float32)
    mcs = [wide_masks(False)[2], wide_masks(True)[2]]
    incl2 = _group_masks(1)[0]

    def prep(z, c, t0):
        x = p_ref[pl.ds(t0, CHUNK), 0:3 * GROUP_W]
        g_c = gc_ref[pl.ds(t0, CHUNK), :] + bc_ref[...]
        g_r = gr_ref[c] + br_ref[...]
        lf_c = log_sigmoid(g_c)
        idx_i = [z * N_HEADS + h for h in range(N_HEADS)]
        idx_f = [8 + j for j in idx_i]
        return (x[:, :GROUP_W], x[:, GROUP_W:2 * GROUP_W] * HEAD_DIM ** -0.5, x[:, 2 * GROUP_W:],
                colw(cumsum_col(mcs[z], lf_c), idx_f), roww(cumsum_row(log_sigmoid(g_r), mcs[z]), idx_f),
                colw(g_c, idx_i), roww(g_r, idx_i), colw(jnp.sum(lf_c, axis=0, keepdims=True), idx_f))

    def body(i, first):
        chunks = _group_chunks(i, 1, nc)
        args = _stack([prep(z, c, t0) for z, c, t0 in chunks])
        hh, c_new, n_new, m_new = jax.vmap(mlstm_chunk, in_axes=(0,) * 12 + (None, None))(
            *args, cbd[...], nrow[...], mw[...], incl2, bdm, bd)
        cbd[...] = c_new
        nrow[...] = n_new
        mw[...] = m_new
        for z, _, t0 in chunks:
            tot = _merge(first, hscr, t0, hh[z])
            if not first:
                og = p_ref[pl.ds(t0, CHUNK), 3 * GROUP_W:]
                y_ref[pl.ds(t0, CHUNK), :] = sigmoid(og) * head_norm(tot, norm_ref[...], bd, True)

    _two_pass(nc, body)
    _store_state(c1_ref, cbd)
    for z in range(2):
        for h in range(N_HEADS):
            n1_ref[0, z, h] = nrow[z, :, _hs(h)]
            m1_ref[0, z, h] = mw[z, :, h * HEAD_DIM:h * HEAD_DIM + 1]


def delta_kernel(p_ref, gc_ref, gr_ref, conv_ref, prow_ref, pcol_ref, norm_ref, s0_ref,
                 y_ref, s1_ref, hscr, sbd, *, nc, g):
    _load_state(sbd, s0_ref)
    bdm = block_diag_mask()
    bd = bdm.astype(jnp.float32)
    mcs = [wide_masks(False)[2], wide_masks(True)[2]]
    gmasks = _group_masks(g)
    nega_row = -jnp.exp(prow_ref[0:1, :])
    dtb_row = prow_ref[1:2, :]
    nega_col = -jnp.exp(pcol_ref[:, 0:1])
    dtb_col = pcol_ref[:, 1:2]
    qkv_w = 3 * GROUP_W

    def prep(z, c, t0):
        ext = _halo(p_ref, c, t0, nc, qkv_w)
        acc = ext[6:6 + CHUNK] * conv_ref[0:1, :]
        for j in range(1, CONV_K):
            acc = acc + ext[6 + j:6 + j + CHUNK] * conv_ref[j:j + 1, :]
        qkv = silu(acc)
        q = l2norm_heads(qkv[:, :GROUP_W], bd) * HEAD_DIM ** -0.5
        k = l2norm_heads(qkv[:, GROUP_W:2 * GROUP_W], bd)
        g_c = gc_ref[pl.ds(t0, CHUNK), :]
        gg_c = nega_row * softplus(g_c + dtb_row)
        gg_r = nega_col * softplus(gr_ref[c] + dtb_col)
        idx_b = [16 + z * N_HEADS + h for h in range(N_HEADS)]
        idx_a = [8 + j for j in idx_b]
        return (q, k, qkv[:, 2 * GROUP_W:], colw(sigmoid(g_c), idx_b), colw(cumsum_col(mcs[z], gg_c), idx_a),
                roww(cumsum_row(gg_r, mcs[z]), idx_a), colw(jnp.sum(gg_c, axis=0, keepdims=True), idx_a))

    def body(i, first):
        chunks = _group_chunks(i, g, nc)
        pre_in = _stack([prep(z, c, t0) for z, c, t0 in chunks])
        pre = jax.vmap(delta_pre, in_axes=(0,) * 10 + (None,))(*pre_in, *gmasks, bdm)
        s = sbd[...]
        for j in range(g):
            o, s = jax.vmap(delta_step, in_axes=(0,) * 6 + (None,))(*_pick(pre, g, j), s, bdm)
            for z in range(2):
                t0 = chunks[z * g + j][2]
                tot = _merge(first, hscr, t0, o[z])
                if not first:
                    zg = p_ref[pl.ds(t0, CHUNK), qkv_w:]
                    y_ref[pl.ds(t0, CHUNK), :] = head_norm(tot, norm_ref[...], bd, False) * silu(zg)
        sbd[...] = s

    _group_loop(nc, g, body)
    _store_state(s1_ref, sbd)


def _rope(x, cos, sin, lane_lo):
    n = x.shape[1]
    swapped = jnp.where(lane_lo, pltpu.roll(x, n - 16, 1), pltpu.roll(x, 16, 1))
    return x * cos + swapped * sin


def ret_kernel(*refs, nc, rope):
    if rope:
        p_ref, cos_ref, sin_ref, rd_ref, norm_ref, s0_ref, y_ref, s1_ref, hscr, sbd, cst = refs
    else:
        p_ref, rd_ref, norm_ref, s0_ref, y_ref, s1_ref, hscr, sbd, cst = refs
    f32 = jnp.float32
    _load_state(sbd, s0_ref)
    bdm = block_diag_mask()
    bd = bdm.astype(f32)
    r = lax.broadcasted_iota(jnp.int32, (CHUNK, GROUP_W), 0).astype(f32)
    cc = jnp.bitwise_and(lax.broadcasted_iota(jnp.int32, (CHUNK, GROUP_W), 1), HEAD_DIM - 1).astype(f32)
    lg_row = -jnp.exp(rd_ref[...])
    for z in range(2):
        incl_w = wide_masks(z == 1)[0]
        lg = colw(lg_row, [z * N_HEADS + h for h in range(N_HEADS)])
        if z == 0:
            cst[z, 0] = jnp.exp(jnp.where(incl_w, (r - cc) * lg, NEG_INF))
            cst[z, 1] = jnp.exp((r + 1.0) * lg)
            cst[z, 2] = jnp.exp((CHUNK - 1.0 - r) * lg)
        else:
            cst[z, 0] = jnp.exp(jnp.where(incl_w, (cc - r) * lg, NEG_INF))
            cst[z, 1] = jnp.exp((CHUNK - r) * lg)
            cst[z, 2] = jnp.exp(r * lg)
        cst[z, 3] = jnp.exp(CHUNK * lg + 0.0 * r)
    if rope:
        lane = lax.broadcasted_iota(jnp.int32, (CHUNK, GROUP_W), 1)
        lane_lo = jnp.bitwise_and(lane, 31) < 16

    def prep(t0):
        x = p_ref[pl.ds(t0, CHUNK), 0:3 * GROUP_W]
        q = x[:, :GROUP_W] * HEAD_DIM ** -0.5
        k = x[:, GROUP_W:2 * GROUP_W]
        if rope:
            cos = cos_ref[pl.ds(t0, CHUNK), :]
            sin = sin_ref[pl.ds(t0, CHUNK), :]
            q = _rope(q, cos, sin, lane_lo)
            k = _rope(k, cos, sin, lane_lo)
        return q, k, x[:, 2 * GROUP_W:]

    def body(i, first):
        chunks = _group_chunks(i, 1, nc)
        args = _stack([prep(t0) for _, _, t0 in chunks])
        o, s_new = jax.vmap(retention_chunk, in_axes=(0,) * 8 + (None,))(
            *args, cst[:, 0], cst[:, 1], cst[:, 2], cst[:, 3, 0:1, :], sbd[...], bdm)
        sbd[...] = s_new
        for z, _, t0 in chunks:
            tot = _merge(first, hscr, t0, o[z])
            if not first:
                gate = p_ref[pl.ds(t0, CHUNK), 3 * GROUP_W:]
                y_ref[pl.ds(t0, CHUNK), :] = silu(gate) * head_norm(tot, norm_ref[...], bd, True)

    _two_pass(nc, body)
    _store_state(s1_ref, sbd)


def rwkv_kernel(p_ref, mu_ref, w0_ref, w2_ref, a0_ref, a2_ref, g2_ref, vec_ref, s0_ref,
                y_ref, s1_ref, hscr, sbd, *, nc, g):
    _load_state(sbd, s0_ref)
    bdm = block_diag_mask()
    bd = bdm.astype(jnp.float32)
    mcs = [wide_masks(False)[2], wide_masks(True)[2]]
    gmasks = _group_masks(g)
    width = p_ref.shape[1]
    kk = vec_ref[0:1, :]
    ka = vec_ref[1:2, :]
    rk = vec_ref[2:3, :]
    gain = vec_ref[3:4, :]

    def lora_a(da, zz):
        return sigmoid(a0_ref[zz:zz + 1, :] + bmm(da[:, zz * RWKV_LORA:(zz + 1) * RWKV_LORA], a2_ref[zz]))

    def prep(z, c, t0, first):
        ext = _halo(p_ref, c, t0, nc, width)
        cur = ext[8:8 + CHUNK]
        pdm = cur + mu_ref[...] * (0.5 * (ext[7:7 + CHUNK] + ext[9:9 + CHUNK]) - cur)
        r = pdm[:, :GROUP_W]
        k = pdm[:, GROUP_W:2 * GROUP_W]
        v = pdm[:, 2 * GROUP_W:3 * GROUP_W]
        dw = pdm[:, 3 * GROUP_W:3 * GROUP_W + 128]
        da = pdm[:, 3 * GROUP_W + 128:3 * GROUP_W + 256]
        a_z = lora_a(da, z)
        w_pre = w0_ref[z:z + 1, :] + bmm(jnp.tanh(dw[:, z * RWKV_LORA:(z + 1) * RWKV_LORA]), w2_ref[z])
        lw = -jnp.exp(-softplus(-w_pre) - 0.5)
        kh = l2norm_heads(k * kk, bd)
        kt_z = k * (1.0 + (a_z - 1.0) * ka)
        b = a_z * kh
        cw = cumsum_col(mcs[z], lw)
        tot = jnp.sum(lw, axis=0, keepdims=True)
        en = jnp.exp(-cw)
        et = jnp.exp(tot - cw)
        fin = ()
        if not first:
            kt_o = k * (1.0 + (lora_a(da, 1 - z) - 1.0) * ka)
            bonus = head_sum(r * (kt_z + kt_o) * rk, bd) * v
            fin = (bmm(sigmoid(pdm[:, 3 * GROUP_W + 256:]), g2_ref[...]), bonus)
        return ((r * jnp.exp(cw), kh * jnp.exp(cw - lw), kt_z * en, b * en, v),
                (v, jnp.concatenate([kt_z * et, b * et], axis=0), jnp.exp(tot)), fin)

    def body(i, first):
        chunks = _group_chunks(i, g, nc)
        preps = [prep(z, c, t0, first) for z, c, t0 in chunks]
        pre = jax.vmap(rwkv_pre, in_axes=(0,) * 8 + (None,))(*_stack([p[0] for p in preps]), *gmasks, bdm)
        step_in = _stack([p[1] for p in preps])
        s = sbd[...]
        for j in range(g):
            o, s = jax.vmap(rwkv_step, in_axes=(0,) * 8 + (None,))(*_pick(pre, g, j), *_pick(step_in, g, j), s, bdm)
            for z in range(2):
                t0 = chunks[z * g + j][2]
                tot_h = _merge(first, hscr, t0, o[z])
                if not first:
                    gate, bonus = preps[z * g + j][2]
                    y_ref[pl.ds(t0, CHUNK), :] = gate * (head_norm(tot_h, gain, bd, True) + bonus)
        sbd[...] = s

    _group_loop(nc, g, body)
    _store_state(s1_ref, sbd)


def mod_kernel(c_ref, w_ref, b_ref, o_ref):
    o_ref[0] = bmm(silu(c_ref[...]), w_ref[0]) + b_ref[0]


def proj_kernel(x_ref, mod_ref, wa, wb, wc, wd, wg, wgt, pa, pb, pc, pd, pg, pgt):
    m = mod_ref[0]
    u = layer_norm(x_ref[...]) * (1.0 + m[1:2]) + m[0:1]
    ub = u.astype(BF)
    pa[...] = _dot(ub, wa[...], _NN)
    pb[...] = _dot(ub, wb[...], _NN)
    pc[...] = _dot(ub, wc[...], _NN)
    pd[...] = _dot(ub, wd[...], _NN)
    pg[...] = _dot(ub, wg[...], _NN)
    gt = _dot(wgt[...], ub, _NT)
    for j in range(ROW_TILE // CHUNK):
        pgt[j] = gt[:, j * CHUNK:(j + 1) * CHUNK]


def post_kernel(h_ref, ya, yb, yc, yd, mod_ref, wout, l1g, l1b, w1, w2, l2g, l2b, o_ref):
    m = mod_ref[0]
    y = jnp.concatenate([ya[...], yb[...], yc[...], yd[...]], axis=1).astype(BF)
    attn = _dot(y, wout[...], _NN)
    h1 = layer_norm(DEEPNORM_ALPHA * h_ref[...] + m[2:3] * attn) * l1g[...] + l1b[...]
    u2 = layer_norm(h1) * (1.0 + m[4:5]) + m[3:4]
    f = _dot(u2.astype(BF), w1[...], _NN)
    f = jnp.square(jnp.maximum(f, 0.0))
    f2 = _dot(f.astype(BF), w2[...], _NN)
    o_ref[...] = layer_norm(DEEPNORM_ALPHA * h1 + m[5:6] * f2) * l2g[...] + l2b[...]


def _params(n_grid_axes):
    return pltpu.CompilerParams(dimension_semantics=("arbitrary",) * n_grid_axes,
                                vmem_limit_bytes=VMEM_LIMIT)


def _const_spec(a):
    nd = a.ndim
    return pl.BlockSpec(a.shape, lambda *_: (0,) * nd)


def _f32(shape):
    return jax.ShapeDtypeStruct(shape, jnp.float32)


def _mod_call(c_all, w_mod, b_mod):
    depth, d, n = w_mod.shape
    tn = 512
    return pl.pallas_call(
        mod_kernel, grid=(depth, n // tn),
        in_specs=[pl.BlockSpec(c_all.shape, lambda l, j: (0, 0)),
                  pl.BlockSpec((1, d, tn), lambda l, j: (l, 0, j)),
                  pl.BlockSpec((1, 1, tn), lambda l, j: (l, 0, j))],
        out_specs=pl.BlockSpec((1, c_all.shape[0], tn), lambda l, j: (l, 0, j)),
        out_shape=_f32((depth, c_all.shape[0], n)),
        compiler_params=_params(2), name="mod")(c_all, w_mod, b_mod.reshape(depth, 1, n))


def _mod_spec(n_ptiles, tiles_per_dec):
    def index(i):
        return (jnp.where(i < n_ptiles, 0, 1 + (i - n_ptiles) // tiles_per_dec), 0, 0)
    return pl.BlockSpec((1, 6, D_MODEL), index)


def _proj_call(x_all, mod_l, ws, n_ptiles, tiles_per_dec):
    n_rows = x_all.shape[0]
    row = lambda i: (i, 0)
    widths = [w.shape[1] for w in ws[:5]]
    per_tile = ROW_TILE // CHUNK
    return pl.pallas_call(
        proj_kernel, grid=(n_rows // ROW_TILE,),
        in_specs=[pl.BlockSpec((ROW_TILE, D_MODEL), row), _mod_spec(n_ptiles, tiles_per_dec)]
                 + [_const_spec(w) for w in ws],
        out_specs=[pl.BlockSpec((ROW_TILE, w), row) for w in widths]
                  + [pl.BlockSpec((per_tile, N_GATES, CHUNK), lambda i: (i, 0, 0))],
        out_shape=[_f32((n_rows, w)) for w in widths] + [_f32((n_rows // CHUNK, N_GATES, CHUNK))],
        compiler_params=_params(1), name="proj")(x_all, mod_l, *ws)


def _post_call(h_all, ys, mod_l, ws, n_ptiles, tiles_per_dec):
    n_rows = h_all.shape[0]
    row = lambda i: (i, 0)
    return pl.pallas_call(
        post_kernel, grid=(n_rows // ROW_TILE,),
        in_specs=[pl.BlockSpec((ROW_TILE, D_MODEL), row)]
                 + [pl.BlockSpec((ROW_TILE, GROUP_W), row) for _ in ys]
                 + [_mod_spec(n_ptiles, tiles_per_dec)] + [_const_spec(w) for w in ws],
        out_specs=pl.BlockSpec((ROW_TILE, D_MODEL), row),
        out_shape=_f32((n_rows, D_MODEL)),
        compiler_params=_params(1), name="post")(h_all, *ys, mod_l, *ws)


def _state_spec(shape_tail, layer):
    nd = len(shape_tail)
    return pl.BlockSpec((1, 1) + shape_tail, lambda i: (i, layer) + (0,) * nd)


def _out_state_spec(shape_tail):
    nd = len(shape_tail)
    return pl.BlockSpec((1,) + shape_tail, lambda i: (i,) + (0,) * nd)


MAT = (2, N_HEADS, HEAD_DIM, HEAD_DIM)
VEC = (2, N_HEADS, 1, HEAD_DIM)
SCL = (2, N_HEADS, 1, 1)
BD_STATE = pltpu.VMEM((2, GROUP_W, GROUP_W), jnp.float32)


def _mixer_call(kernel, name, t_len, n_seq, blk_off, seq_inputs, const_inputs, state_inputs, layer,
                state_tails, scratch):
    nc = t_len // CHUNK
    in_specs = []
    for a in seq_inputs:
        if a.ndim == 2:
            in_specs.append(pl.BlockSpec((t_len, a.shape[1]), lambda i: (i + blk_off, 0)))
        else:
            in_specs.append(pl.BlockSpec((nc,) + a.shape[1:], lambda i: (i + blk_off, 0, 0)))
    in_specs += [_const_spec(a) for a in const_inputs]
    in_specs += [_state_spec(a.shape[2:], layer) for a in state_inputs]
    out_specs = [pl.BlockSpec((t_len, GROUP_W), lambda i: (i, 0))] + [_out_state_spec(t) for t in state_tails]
    out_shape = [_f32((n_seq * t_len, GROUP_W))] + [_f32((n_seq,) + t) for t in state_tails]
    return pl.pallas_call(
        kernel, grid=(n_seq,), in_specs=in_specs, out_specs=out_specs, out_shape=out_shape,
        scratch_shapes=[pltpu.VMEM((t_len, GROUP_W), jnp.float32)] + scratch,
        compiler_params=_params(1), name=name)(*seq_inputs, *const_inputs, *state_inputs)


def _rope_tables(t_len):
    nf = HEAD_DIM // 4
    inv = ROPE_BASE ** (-jnp.arange(nf, dtype=jnp.float32) / nf)
    t = jnp.arange(t_len)
    ang_r = (t // GRID_W).astype(jnp.float32)[:, None] * inv[None, :]
    ang_c = (t % GRID_W).astype(jnp.float32)[:, None] * inv[None, :]
    cos = jnp.concatenate([jnp.cos(ang_r)] * 2 + [jnp.cos(ang_c)] * 2, axis=1)
    sin = jnp.concatenate([-jnp.sin(ang_r), jnp.sin(ang_r), -jnp.sin(ang_c), jnp.sin(ang_c)], axis=1)
    return jnp.tile(cos, (1, N_HEADS)), jnp.tile(sin, (1, N_HEADS))


def _lane_row(vals, lo, n_rows=1, row=0, width=GATE_LANES):
    out = jnp.zeros((n_rows, width), jnp.float32)
    return out.at[row, lo:lo + vals.size].set(vals.reshape(-1))


def kernel(x_prompt, x_sample, c, state_mlstm_C, state_mlstm_n, state_mlstm_m, state_delta, state_ret,
           state_rwkv, c_ctx, w_mod, b_mod, w_in, w_out, ln1_g, ln1_b, ln2_g, ln2_b, w_ff1, w_ff2,
           mlstm_i_bias, mlstm_f_bias, mlstm_norm, delta_conv, delta_a_log, delta_dt_bias, delta_norm,
           ret_decay, ret_norm, rwkv_mu, rwkv_w0, rwkv_w2, rwkv_a0, rwkv_a2, rwkv_g2, rwkv_kk, rwkv_ka,
           rwkv_rk, rwkv_norm):
    f32 = jnp.float32
    batch, seq, d = x_prompt.shape
    dec_batch, dec_seq, _ = x_sample.shape
    assert d == D_MODEL and seq % ROW_TILE == 0 and dec_seq % ROW_TILE == 0
    assert (seq // CHUNK) % 2 == 0 and (dec_seq // CHUNK) % 2 == 0 and (batch * seq) % dec_seq == 0
    n_prows = batch * seq
    n_ptiles = n_prows // ROW_TILE
    tiles_per_dec = dec_seq // ROW_TILE

    c_all = jnp.zeros((8, d), f32).at[0].set(c_ctx).at[1:1 + dec_batch].set(c)
    mod = _mod_call(c_all, w_mod, b_mod).reshape(DEPTH, 8, 6, d)
    h_all = jnp.concatenate([x_prompt.reshape(n_prows, d), x_sample.reshape(dec_batch * dec_seq, d)], axis=0)
    cos_t, sin_t = _rope_tables(dec_seq)

    zmat = jnp.zeros((batch, 1) + MAT, f32)
    zn = jnp.zeros((batch, 1) + VEC, f32)
    zm = jnp.zeros((batch, 1) + SCL, f32)
    n_cache = state_mlstm_n.reshape(state_mlstm_n.shape[:-1] + (1, HEAD_DIM))
    m_cache = state_mlstm_m.reshape(state_mlstm_m.shape + (1, 1))
    groups = ((seq, batch, 0), (dec_seq, dec_batch, n_prows // dec_seq))

    new_states = []
    for l in range(DEPTH):
        wl = w_in[l]
        ws = [wl[:, 0:1024], wl[:, 1040:2064], wl[:, 2080:3104], wl[:, 3104:4256],
              jnp.pad(jnp.concatenate([wl[:, 1024:1040], wl[:, 2064:2080]], axis=1), ((0, 0), (0, GATE_LANES - N_GATES)))]
        ws = [w.astype(BF) for w in ws]
        ws.append(ws[4][:, :N_GATES].T)
        pa, pb, pc, pd, pg, pgt = _proj_call(h_all, mod[l], ws, n_ptiles, tiles_per_dec)

        bias_c = _lane_row(mlstm_i_bias[l], 0) + _lane_row(mlstm_f_bias[l], 8)
        bias_r = bias_c[0, :N_GATES].reshape(N_GATES, 1)
        dl_row = _lane_row(delta_a_log[l], 24, 2, 0) + _lane_row(delta_dt_bias[l], 24, 2, 1)
        dl_col = dl_row[:, :N_GATES].T
        rd_row = _lane_row(ret_decay[l], 0)
        rw_vec = jnp.stack([rwkv_kk[l], rwkv_ka[l], rwkv_rk[l].reshape(-1), rwkv_norm[l]], axis=0)
        row = lambda a: a.reshape(1, -1)

        ys = [[], [], [], []]
        layer_states = None
        for gi, (t_len, n_seq, off) in enumerate(groups):
            nc = t_len // CHUNK
            ctx = gi == 0
            lay = 0 if ctx else l
            ya, c1, n1, m1 = _mixer_call(
                functools.partial(mlstm_kernel, nc=nc), "mlstm", t_len, n_seq, off, [pa, pg, pgt],
                [bias_c, bias_r, row(mlstm_norm[l])],
                [zmat, zn, zm] if ctx else [state_mlstm_C, n_cache, m_cache], lay, [MAT, VEC, SCL],
                [BD_STATE, pltpu.VMEM((2, 1, GROUP_W), f32), pltpu.VMEM((2, 1, GROUP_W), f32)])
            yb, sb1 = _mixer_call(
                functools.partial(delta_kernel, nc=nc, g=SCAN_STEPS), "delta", t_len, n_seq, off, [pb, pg, pgt],
                [delta_conv[l], dl_row, dl_col, row(delta_norm[l])],
                [zmat if ctx else state_delta], lay, [MAT], [BD_STATE])
            ret_consts = [rd_row, row(ret_norm[l])]
            if not ctx:
                ret_consts = [cos_t, sin_t] + ret_consts
            yc, sc1 = _mixer_call(
                functools.partial(ret_kernel, nc=nc, rope=not ctx), "ret", t_len, n_seq, off, [pc],
                ret_consts, [zmat if ctx else state_ret], lay, [MAT],
                [BD_STATE, pltpu.VMEM((2, 4, CHUNK, GROUP_W), f32)])
            yd, sd1 = _mixer_call(
                functools.partial(rwkv_kernel, nc=nc, g=SCAN_STEPS), "rwkv", t_len, n_seq, off, [pd],
                [row(rwkv_mu[l]), rwkv_w0[l], rwkv_w2[l], rwkv_a0[l], rwkv_a2[l], rwkv_g2[l], rw_vec],
                [zmat if ctx else state_rwkv], lay, [MAT], [BD_STATE])
            for lst, y in zip(ys, (ya, yb, yc, yd)):
                lst.append(y)
            if ctx:
                layer_states = (c1, n1.reshape(n1.shape[:3] + (HEAD_DIM,)), m1.reshape(m1.shape[:3]), sb1, sc1, sd1)
        new_states.append(layer_states)
        ys = [jnp.concatenate(y, axis=0) for y in ys]
        post_ws = [w_out[l].astype(BF), row(ln1_g[l]), row(ln1_b[l]), w_ff1[l].astype(BF), w_ff2[l].astype(BF),
                   row(ln2_g[l]), row(ln2_b[l])]
        h_all = _post_call(h_all, ys, mod[l], post_ws, n_ptiles, tiles_per_dec)

    outs = [jnp.stack([st[i] for st in new_states], axis=1) for i in range(6)]
    return (h_all[:n_prows].reshape(batch, seq, d), h_all[n_prows:].reshape(dec_batch, dec_seq, d), *outs)
```

```python
import functools

import jax
import jax.numpy as jnp
from jax import lax
from jax.experimental import pallas as pl
from jax.experimental.pallas import tpu as pltpu

D_MODEL = 1024
DEPTH = 2
GRID_W = 64
HEAD_DIM = 64
N_HEADS = 4
GROUP_W = N_HEADS * HEAD_DIM
CHUNK = 64
CONV_K = 5
ROPE_BASE = 10000.0
LN_EPS = 1e-5
DEEPNORM_ALPHA = (2 * DEPTH) ** 0.25
RWKV_LORA = 64
GATE_LANES = 128
N_GATES = 32
ROW_TILE = 256
VMEM_LIMIT = 56 * 1024 * 1024
MAX_SCAN_STEPS = 4

NEG_INF = float("-inf")
BF = jnp.bfloat16
_NN = (((1,), (0,)), ((), ()))
_NT = (((1,), (1,)), ((), ()))
_TN = (((0,), (0,)), ((), ()))


def _dot(a, b, dims):
    return lax.dot_general(a, b, dims, preferred_element_type=jnp.result_type(a.dtype, jnp.float32))


def _bf(x):
    return x.astype(BF)


def _split(x, n):
    parts = []
    for _ in range(n - 1):
        p = _bf(x)
        parts.append(p)
        x = x - p.astype(x.dtype)
    parts.append(_bf(x))
    return parts


def bmm(a, b, dims=_NN):
    return _dot(_bf(a), _bf(b), dims)


def _expand(x, bdm):
    return jnp.concatenate([x] * N_HEADS, axis=0) * bdm[0].astype(x.dtype)


def wmm(a, b, bdm, passes=1, nt=False):
    dims = _NT if nt else _NN
    if passes == 1:
        return _dot(_bf(a), _expand(_bf(b), bdm), dims)
    ah, al = _split(a, 2)
    bh, bl = _split(b, 2)
    ebh = _expand(bh, bdm)
    lhs = jnp.concatenate([ah, ah, al], axis=1)
    rhs = jnp.concatenate([ebh, _expand(bl, bdm), ebh], axis=1 if nt else 0)
    return _dot(lhs, rhs, dims)


def cumsum_col(mc, x):
    return _dot(jnp.concatenate([_bf(mc)] * 3, axis=1), jnp.concatenate(_split(x, 3), axis=0), _NN)


def cumsum_row(x, mc):
    return _dot(jnp.concatenate(_split(x, 3), axis=1), jnp.concatenate([_bf(mc)] * 3, axis=1), _NT)


def head_sum(x, bd):
    return _dot(jnp.concatenate(_split(x, 2), axis=1), jnp.concatenate([_bf(bd)] * 2, axis=0), _NN)


def sigmoid(x):
    return 1.0 / (1.0 + jnp.exp(-x))


def silu(x):
    return x * sigmoid(x)


def softplus(x):
    return jnp.maximum(x, 0.0) + jnp.log(1.0 + jnp.exp(-jnp.abs(x)))


def log_sigmoid(x):
    return -softplus(-x)


def head_norm(y, gain, bd, center):
    if center:
        y = y - head_sum(y, bd) * (1.0 / HEAD_DIM)
    ms = head_sum(y * y, bd) * (1.0 / HEAD_DIM)
    return y * lax.rsqrt(ms + LN_EPS) * gain


def l2norm_heads(x, bd):
    return x * lax.rsqrt(head_sum(x * x, bd) + 1e-6)


def layer_norm(x):
    mu = jnp.mean(x, axis=-1, keepdims=True)
    xc = x - mu
    var = jnp.mean(xc * xc, axis=-1, keepdims=True)
    return xc * lax.rsqrt(var + LN_EPS)


def colw(x, idx):
    m = x.shape[0]
    return jnp.concatenate([jnp.broadcast_to(x[:, j:j + 1], (m, HEAD_DIM)) for j in idx], axis=1)


def roww(x, idx):
    return jnp.concatenate([x[j:j + 1, :] for j in idx], axis=1)


def segmax(x):
    m = x.shape[0]
    return jnp.concatenate(
        [jnp.broadcast_to(jnp.max(x[:, h * HEAD_DIM:(h + 1) * HEAD_DIM], axis=1, keepdims=True), (m, HEAD_DIM))
         for h in range(N_HEADS)], axis=1)


def wide_masks(rev, dtype=jnp.float32):
    r = lax.broadcasted_iota(jnp.int32, (CHUNK, GROUP_W), 0)
    c = jnp.bitwise_and(lax.broadcasted_iota(jnp.int32, (CHUNK, GROUP_W), 1), HEAD_DIM - 1)
    if rev:
        incl_w, strict_w = r <= c, r < c
    else:
        incl_w, strict_w = r >= c, r > c
    mc = incl_w[:, :CHUNK].astype(dtype)
    return incl_w, strict_w, mc, (r == c).astype(dtype)


def block_diag_mask():
    r = jnp.right_shift(lax.broadcasted_iota(jnp.int32, (GROUP_W, GROUP_W), 0), 6)
    c = jnp.right_shift(lax.broadcasted_iota(jnp.int32, (GROUP_W, GROUP_W), 1), 6)
    m = (r == c).astype(jnp.float32)
    return _bf(m), m


def inv_unit_tri_w(a, eye_w, bdm):
    p = _bf(-a)
    e = _expand(p, bdm)
    t = eye_w - a
    for _ in range(5):
        p = _bf(_dot(p, e, _NN))
        e = _expand(p, bdm)
        t = t + _dot(_bf(t), e, _NN)
    r = eye_w - t - wmm(a, t, bdm, 3)
    return t + wmm(t, r, bdm, 1)


def mlstm_chunk(q, k, v, bcol_w, brow_w, icol_w, irow_w, bl_w, c_bd, n_row, m_w, incl_w, bdm, bd):
    logd = jnp.where(incl_w, bcol_w - brow_w + irow_w, NEG_INF)
    inter = bcol_w + m_w
    mt = jnp.maximum(inter, segmax(logd))
    s = wmm(q, k, bdm, 1, nt=True) * jnp.exp(logd - mt)
    ew = jnp.exp(inter - mt)
    num = wmm(s, v, bdm, 1) + bmm(q, c_bd) * ew
    den = head_sum(s, bd) + head_sum(q * n_row, bd) * ew
    h = num / jnp.maximum(jnp.abs(den), jnp.exp(-mt))
    wk = bl_w - bcol_w + icol_w
    m_new = jnp.maximum(bl_w + m_w, jnp.max(wk, axis=0, keepdims=True))
    sk = jnp.exp(wk - m_new)
    sp = jnp.exp(bl_w + m_w - m_new)
    ks = k * sk
    c_new = c_bd * sp + bmm(ks, v, _TN) * bdm[1]
    n_new = n_row * sp + jnp.sum(ks, axis=0, keepdims=True)
    return h, c_new, n_new, m_new


def delta_pre(q, k, v, beta_w, gcol_w, grow_w, gl_w, incl_w, strict_w, eye_w, bdm):
    decay = jnp.exp(jnp.where(incl_w, gcol_w - grow_w, NEG_INF))
    a = jnp.where(strict_w, wmm(k, k, bdm, 1, nt=True) * decay, 0.0) * beta_w
    t = inv_unit_tri_w(a, eye_w, bdm)
    eg = jnp.exp(gcol_w)
    u = wmm(t, v * beta_w, bdm, 1)
    w = wmm(t, k * (beta_w * eg), bdm, 1)
    qk = wmm(q, k, bdm, 1, nt=True) * decay
    return u, jnp.concatenate([w, q * eg], axis=0), qk, k * jnp.exp(gl_w - gcol_w), jnp.exp(gl_w)


def delta_step(u, wq, qk, kd, egl, s_bd, bdm):
    ws = bmm(wq, s_bd)
    v_new = u - ws[:CHUNK]
    o = ws[CHUNK:] + wmm(qk, v_new, bdm, 1)
    s_new = s_bd * egl + bmm(kd, v_new, _TN) * bdm[1]
    return o, s_new


def retention_chunk(q, k, v, dmat_w, xi_w, zeta_w, gl_w, s_bd, bdm):
    sc = wmm(q, k, bdm, 1, nt=True) * dmat_w
    o = wmm(sc, v, bdm, 1) + bmm(q, s_bd) * xi_w
    s_new = s_bd * gl_w + bmm(k * zeta_w, v, _TN) * bdm[1]
    return o, s_new


def rwkv_pre(rp, khp, ktn, bn, v, incl_w, strict_w, eye_w, bdm):
    lhs = jnp.concatenate([khp, rp], axis=0)
    ab_kt = wmm(lhs, ktn, bdm, 1, nt=True)
    ab_b = wmm(lhs, bn, bdm, 1, nt=True)
    a_kt = jnp.where(strict_w, ab_kt[:CHUNK], 0.0)
    b_kt = jnp.where(incl_w, ab_kt[CHUNK:], 0.0)
    a_b = jnp.where(strict_w, ab_b[:CHUNK], 0.0)
    b_b = jnp.where(incl_w, ab_b[CHUNK:], 0.0)
    t = inv_unit_tri_w(a_b, eye_w, bdm)
    tk = wmm(t, khp, bdm, 1)
    tav = wmm(t, wmm(a_kt, v, bdm, 1), bdm, 1)
    return jnp.concatenate([tk, rp], axis=0), tav, wmm(b_kt, v, bdm, 1), b_b


def rwkv_step(tkr, tav, o1, b_b, v, kbe, etot, s_bd, bdm):
    ls = bmm(tkr, s_bd, _NT)
    u = ls[:CHUNK] + tav
    o = ls[CHUNK:] + o1 - wmm(b_b, u, bdm, 1)
    upd = bmm(jnp.concatenate([v, -u], axis=0), kbe, _TN)
    s_new = s_bd * etot + upd * bdm[1]
    return o, s_new


def _two_pass(nc, body):
    half = nc // 2

    def first(i, carry):
        body(i, True)
        return carry

    def second(i, carry):
        body(i, False)
        return carry

    lax.fori_loop(0, half, first, 0)
    lax.fori_loop(half, nc, second, 0)


def _chunk_pos(i, z, nc):
    c = i if z == 0 else nc - 1 - i
    return c, pl.multiple_of(c * CHUNK, CHUNK)


def _group_loop(nc, g, body):
    n_it = nc // g

    def run(first):
        def it(i, carry):
            body(i, first)
            return carry
        return it

    lax.fori_loop(0, n_it // 2, run(True), 0)
    lax.fori_loop(n_it // 2, n_it, run(False), 0)


def _scan_steps(nc):
    return max(g for g in range(1, MAX_SCAN_STEPS + 1) if nc % (2 * g) == 0)


def _group_chunks(i, g, nc):
    out = []
    for z in range(2):
        for j in range(g):
            step = i * g + j
            c = step if z == 0 else nc - 1 - step
            out.append((z, c, pl.multiple_of(c * CHUNK, CHUNK)))
    return out


def _group_masks(g):
    shape = (2 * g, CHUNK, GROUP_W)
    rev = lax.broadcasted_iota(jnp.int32, shape, 0) >= g
    r = lax.broadcasted_iota(jnp.int32, shape, 1)
    c = jnp.bitwise_and(lax.broadcasted_iota(jnp.int32, shape, 2), HEAD_DIM - 1)
    d = jnp.where(rev, c - r, r - c)
    return d >= 0, d > 0, (d == 0).astype(jnp.float32)


def _stack(items):
    return [jnp.stack(x) for x in zip(*items)]


def _pick(arrs, g, j):
    return [jnp.stack([a[j], a[g + j]]) for a in arrs]


def _halo(ebuf, ref, c, t0, nc):
    t_len = nc * CHUNK
    width = ebuf.shape[1]
    ps = pl.multiple_of(jnp.maximum(t0 - 8, 0), 8)
    ns = pl.multiple_of(jnp.minimum(t0 + CHUNK, t_len - 8), 8)
    ebuf[0:8, :] = jnp.where(c > 0, ref[pl.ds(ps, 8), 0:width], 0.0)
    ebuf[8:8 + CHUNK, :] = ref[pl.ds(t0, CHUNK), 0:width]
    ebuf[8 + CHUNK:, :] = jnp.where(c < nc - 1, ref[pl.ds(ns, 8), 0:width], 0.0)


def _hs(h):
    return slice(h * HEAD_DIM, (h + 1) * HEAD_DIM)


def _load_state(sbd, s0_ref):
    for z in range(2):
        sbd[z] = jnp.zeros((GROUP_W, GROUP_W), jnp.float32)
        for h in range(N_HEADS):
            sbd[z, _hs(h), _hs(h)] = s0_ref[0, 0, z, h]


def _store_state(s1_ref, sbd):
    for z in range(2):
        for h in range(N_HEADS):
            s1_ref[0, z, h] = sbd[z, _hs(h), _hs(h)]


def _merge(first, hscr, t0, h):
    if first:
        hscr[pl.ds(t0, CHUNK), :] = h
        return None
    return hscr[pl.ds(t0, CHUNK), :] + h


def mlstm_kernel(p_ref, gc_ref, gr_ref, bc_ref, br_ref, norm_ref, c0_ref, n0_ref, m0_ref,
                 y_ref, c1_ref, n1_ref, m1_ref, hscr, cbd, nrow, mw, *, nc):
    _load_state(cbd, c0_ref)
    for z in range(2):
        for h in range(N_HEADS):
            nrow[z, :, _hs(h)] = n0_ref[0, 0, z, h]
            mw[z, :, _hs(h)] = jnp.broadcast_to(m0_ref[0, 0, z, h], (1, HEAD_DIM))
    bdm = block_diag_mask()
    bd = bdm[1]
    mcs = [wide_masks(False)[2], wide_masks(True)[2]]
    incl2 = _group_masks(1)[0]

    def prep(z, c, t0):
        x = p_ref[pl.ds(t0, CHUNK), 0:3 * GROUP_W]
        g_c = gc_ref[pl.ds(t0, CHUNK), :] + bc_ref[...]
        g_r = gr_ref[c] + br_ref[...]
        lf_c = log_sigmoid(g_c)
        idx_i = [z * N_HEADS + h for h in range(N_HEADS)]
        idx_f = [8 + j for j in idx_i]
        return (x[:, :GROUP_W], x[:, GROUP_W:2 * GROUP_W] * HEAD_DIM ** -0.5, x[:, 2 * GROUP_W:],
                colw(cumsum_col(mcs[z], lf_c), idx_f), roww(cumsum_row(log_sigmoid(g_r), mcs[z]), idx_f),
                colw(g_c, idx_i), roww(g_r, idx_i), colw(jnp.sum(lf_c, axis=0, keepdims=True), idx_f))

    def body(i, first):
        chunks = _group_chunks(i, 1, nc)
        args = _stack([prep(z, c, t0) for z, c, t0 in chunks])
        hh, c_new, n_new, m_new = jax.vmap(mlstm_chunk, in_axes=(0,) * 12 + (None, None))(
            *args, cbd[...], nrow[...], mw[...], incl2, bdm, bd)
        cbd[...] = c_new
        nrow[...] = n_new
        mw[...] = m_new
        for z, _, t0 in chunks:
            tot = _merge(first, hscr, t0, hh[z])
            if not first:
                og = p_ref[pl.ds(t0, CHUNK), 3 * GROUP_W:]
                y_ref[pl.ds(t0, CHUNK), :] = sigmoid(og) * head_norm(tot, norm_ref[...], bd, True)

    _two_pass(nc, body)
    _store_state(c1_ref, cbd)
    for z in range(2):
        for h in range(N_HEADS):
            n1_ref[0, z, h] = nrow[z, :, _hs(h)]
            m1_ref[0, z, h] = mw[z, :, h * HEAD_DIM:h * HEAD_DIM + 1]


def delta_kernel(p_ref, gc_ref, gr_ref, conv_ref, prow_ref, pcol_ref, norm_ref, s0_ref,
                 y_ref, s1_ref, hscr, sbd, ebuf, *, nc, g):
    _load_state(sbd, s0_ref)
    bdm = block_diag_mask()
    bd = bdm[1]
    mcs = [wide_masks(False)[2], wide_masks(True)[2]]
    gmasks = _group_masks(g)
    nega_row = -jnp.exp(prow_ref[0:1, :])
    dtb_row = prow_ref[1:2, :]
    nega_col = -jnp.exp(pcol_ref[:, 0:1])
    dtb_col = pcol_ref[:, 1:2]
    qkv_w = 3 * GROUP_W

    def prep(slot, z, c, t0):
        eb = ebuf.at[slot]
        _halo(eb, p_ref, c, t0, nc)
        acc = eb[6:6 + CHUNK, :] * conv_ref[0:1, :]
        for j in range(1, CONV_K):
            acc = acc + eb[6 + j:6 + j + CHUNK, :] * conv_ref[j:j + 1, :]
        qkv = silu(acc)
        q = l2norm_heads(qkv[:, :GROUP_W], bd) * HEAD_DIM ** -0.5
        k = l2norm_heads(qkv[:, GROUP_W:2 * GROUP_W], bd)
        g_c = gc_ref[pl.ds(t0, CHUNK), :]
        gg_c = nega_row * softplus(g_c + dtb_row)
        gg_r = nega_col * softplus(gr_ref[c] + dtb_col)
        idx_b = [16 + z * N_HEADS + h for h in range(N_HEADS)]
        idx_a = [8 + j for j in idx_b]
        return (q, k, qkv[:, 2 * GROUP_W:], colw(sigmoid(g_c), idx_b), colw(cumsum_col(mcs[z], gg_c), idx_a),
                roww(cumsum_row(gg_r, mcs[z]), idx_a), colw(jnp.sum(gg_c, axis=0, keepdims=True), idx_a))

    def body(i, first):
        chunks = _group_chunks(i, g, nc)
        pre_in = _stack([prep(slot, z, c, t0) for slot, (z, c, t0) in enumerate(chunks)])
        pre = jax.vmap(delta_pre, in_axes=(0,) * 10 + (None,))(*pre_in, *gmasks, bdm)
        s = sbd[...]
        for j in range(g):
            o, s = jax.vmap(delta_step, in_axes=(0,) * 6 + (None,))(*_pick(pre, g, j), s, bdm)
            for z in range(2):
                t0 = chunks[z * g + j][2]
                tot = _merge(first, hscr, t0, o[z])
                if not first:
                    zg = p_ref[pl.ds(t0, CHUNK), qkv_w:]
                    y_ref[pl.ds(t0, CHUNK), :] = head_norm(tot, norm_ref[...], bd, False) * silu(zg)
        sbd[...] = s

    _group_loop(nc, g, body)
    _store_state(s1_ref, sbd)


def _rope(x, cos, sin, lane_lo):
    n = x.shape[1]
    swapped = jnp.where(lane_lo, pltpu.roll(x, n - 16, 1), pltpu.roll(x, 16, 1))
    return x * cos + swapped * sin


def ret_kernel(*refs, nc, rope):
    if rope:
        p_ref, cos_ref, sin_ref, rd_ref, norm_ref, s0_ref, y_ref, s1_ref, hscr, sbd, cst = refs
    else:
        p_ref, rd_ref, norm_ref, s0_ref, y_ref, s1_ref, hscr, sbd, cst = refs
    f32 = jnp.float32
    _load_state(sbd, s0_ref)
    bdm = block_diag_mask()
    bd = bdm[1]
    r = lax.broadcasted_iota(jnp.int32, (CHUNK, GROUP_W), 0).astype(f32)
    cc = jnp.bitwise_and(lax.broadcasted_iota(jnp.int32, (CHUNK, GROUP_W), 1), HEAD_DIM - 1).astype(f32)
    lg_row = -jnp.exp(rd_ref[...])
    for z in range(2):
        incl_w = wide_masks(z == 1)[0]
        lg = colw(lg_row, [z * N_HEADS + h for h in range(N_HEADS)])
        if z == 0:
            cst[z, 0] = jnp.exp(jnp.where(incl_w, (r - cc) * lg, NEG_INF))
            cst[z, 1] = jnp.exp((r + 1.0) * lg)
            cst[z, 2] = jnp.exp((CHUNK - 1.0 - r) * lg)
        else:
            cst[z, 0] = jnp.exp(jnp.where(incl_w, (cc - r) * lg, NEG_INF))
            cst[z, 1] = jnp.exp((CHUNK - r) * lg)
            cst[z, 2] = jnp.exp(r * lg)
        cst[z, 3] = jnp.exp(CHUNK * lg + 0.0 * r)
    if rope:
        lane = lax.broadcasted_iota(jnp.int32, (CHUNK, GROUP_W), 1)
        lane_lo = jnp.bitwise_and(lane, 31) < 16

    def prep(t0):
        x = p_ref[pl.ds(t0, CHUNK), 0:3 * GROUP_W]
        q = x[:, :GROUP_W] * HEAD_DIM ** -0.5
        k = x[:, GROUP_W:2 * GROUP_W]
        if rope:
            cos = cos_ref[pl.ds(t0, CHUNK), :]
            sin = sin_ref[pl.ds(t0, CHUNK), :]
            q = _rope(q, cos, sin, lane_lo)
            k = _rope(k, cos, sin, lane_lo)
        return q, k, x[:, 2 * GROUP_W:]

    def body(i, first):
        chunks = _group_chunks(i, 1, nc)
        args = _stack([prep(t0) for _, _, t0 in chunks])
        o, s_new = jax.vmap(retention_chunk, in_axes=(0,) * 8 + (None,))(
            *args, cst[:, 0], cst[:, 1], cst[:, 2], cst[:, 3, 0:1, :], sbd[...], bdm)
        sbd[...] = s_new
        for z, _, t0 in chunks:
            tot = _merge(first, hscr, t0, o[z])
            if not first:
                gate = p_ref[pl.ds(t0, CHUNK), 3 * GROUP_W:]
                y_ref[pl.ds(t0, CHUNK), :] = silu(gate) * head_norm(tot, norm_ref[...], bd, True)

    _two_pass(nc, body)
    _store_state(s1_ref, sbd)


def rwkv_kernel(p_ref, mu_ref, w0_ref, w2_ref, a0_ref, a2_ref, g2_ref, vec_ref, s0_ref,
                y_ref, s1_ref, hscr, sbd, ebuf, *, nc, g):
    _load_state(sbd, s0_ref)
    bdm = block_diag_mask()
    bd = bdm[1]
    mcs = [wide_masks(False)[2], wide_masks(True)[2]]
    gmasks = _group_masks(g)
    kk = vec_ref[0:1, :]
    ka = vec_ref[1:2, :]
    rk = vec_ref[2:3, :]
    gain = vec_ref[3:4, :]

    def lora_a(da, zz):
        return sigmoid(a0_ref[zz:zz + 1, :] + bmm(da[:, zz * RWKV_LORA:(zz + 1) * RWKV_LORA], a2_ref[zz]))

    def prep(slot, z, c, t0, first):
        eb = ebuf.at[slot]
        _halo(eb, p_ref, c, t0, nc)
        cur = eb[8:8 + CHUNK, :]
        pdm = cur + mu_ref[...] * (0.5 * (eb[7:7 + CHUNK, :] + eb[9:9 + CHUNK, :]) - cur)
        r = pdm[:, :GROUP_W]
        k = pdm[:, GROUP_W:2 * GROUP_W]
        v = pdm[:, 2 * GROUP_W:3 * GROUP_W]
        dw = pdm[:, 3 * GROUP_W:3 * GROUP_W + 128]
        da = pdm[:, 3 * GROUP_W + 128:3 * GROUP_W + 256]
        a_z = lora_a(da, z)
        w_pre = w0_ref[z:z + 1, :] + bmm(jnp.tanh(dw[:, z * RWKV_LORA:(z + 1) * RWKV_LORA]), w2_ref[z])
        lw = -jnp.exp(-softplus(-w_pre) - 0.5)
        kh = l2norm_heads(k * kk, bd)
        kt_z = k * (1.0 + (a_z - 1.0) * ka)
        b = a_z * kh
        cw = cumsum_col(mcs[z], lw)
        tot = jnp.sum(lw, axis=0, keepdims=True)
        en = jnp.exp(-cw)
        et = jnp.exp(tot - cw)
        fin = ()
        if not first:
            kt_o = k * (1.0 + (lora_a(da, 1 - z) - 1.0) * ka)
            bonus = head_sum(r * (kt_z + kt_o) * rk, bd) * v
            fin = (bmm(sigmoid(pdm[:, 3 * GROUP_W + 256:]), g2_ref[...]), bonus)
        return ((r * jnp.exp(cw), kh * jnp.exp(cw - lw), kt_z * en, b * en, v),
                (v, jnp.concatenate([kt_z * et, b * et], axis=0), jnp.exp(tot)), fin)

    def body(i, first):
        chunks = _group_chunks(i, g, nc)
        preps = [prep(slot, z, c, t0, first) for slot, (z, c, t0) in enumerate(chunks)]
        pre = jax.vmap(rwkv_pre, in_axes=(0,) * 8 + (None,))(*_stack([p[0] for p in preps]), *gmasks, bdm)
        step_in = _stack([p[1] for p in preps])
        s = sbd[...]
        for j in range(g):
            o, s = jax.vmap(rwkv_step, in_axes=(0,) * 8 + (None,))(*_pick(pre, g, j), *_pick(step_in, g, j), s, bdm)
            for z in range(2):
                t0 = chunks[z * g + j][2]
                tot_h = _merge(first, hscr, t0, o[z])
                if not first:
                    gate, bonus = preps[z * g + j][2]
                    y_ref[pl.ds(t0, CHUNK), :] = gate * (head_norm(tot_h, gain, bd, True) + bonus)
        sbd[...] = s

    _group_loop(nc, g, body)
    _store_state(s1_ref, sbd)


def mod_kernel(c_ref, w_ref, b_ref, o_ref):
    o_ref[0] = bmm(silu(c_ref[...]), w_ref[0]) + b_ref[0]


def proj_kernel(x_ref, mod_ref, wa, wb, wc, wd, wg, wgt, pa, pb, pc, pd, pg, pgt):
    m = mod_ref[0]
    u = layer_norm(x_ref[...]) * (1.0 + m[1:2]) + m[0:1]
    ub = u.astype(BF)
    pa[...] = _dot(ub, wa[...], _NN)
    pb[...] = _dot(ub, wb[...], _NN)
    pc[...] = _dot(ub, wc[...], _NN)
    pd[...] = _dot(ub, wd[...], _NN)
    pg[...] = _dot(ub, wg[...], _NN)
    gt = _dot(wgt[...], ub, _NT)
    for j in range(ROW_TILE // CHUNK):
        pgt[j] = gt[:, j * CHUNK:(j + 1) * CHUNK]


def post_kernel(h_ref, ya, yb, yc, yd, mod_ref, wout, l1g, l1b, w1, w2, l2g, l2b, o_ref):
    m = mod_ref[0]
    y = jnp.concatenate([ya[...], yb[...], yc[...], yd[...]], axis=1).astype(BF)
    attn = _dot(y, wout[...], _NN)
    h1 = layer_norm(DEEPNORM_ALPHA * h_ref[...] + m[2:3] * attn) * l1g[...] + l1b[...]
    u2 = layer_norm(h1) * (1.0 + m[4:5]) + m[3:4]
    f = _dot(u2.astype(BF), w1[...], _NN)
    f = jnp.square(jnp.maximum(f, 0.0))
    f2 = _dot(f.astype(BF), w2[...], _NN)
    o_ref[...] = layer_norm(DEEPNORM_ALPHA * h1 + m[5:6] * f2) * l2g[...] + l2b[...]


def _params(n_grid_axes):
    return pltpu.CompilerParams(dimension_semantics=("arbitrary",) * n_grid_axes,
                                vmem_limit_bytes=VMEM_LIMIT)


def _const_spec(a):
    nd = a.ndim
    return pl.BlockSpec(a.shape, lambda *_: (0,) * nd)


def _f32(shape):
    return jax.ShapeDtypeStruct(shape, jnp.float32)


def _mod_call(c_all, w_mod, b_mod):
    depth, d, n = w_mod.shape
    tn = 512
    return pl.pallas_call(
        mod_kernel, grid=(depth, n // tn),
        in_specs=[pl.BlockSpec(c_all.shape, lambda l, j: (0, 0)),
                  pl.BlockSpec((1, d, tn), lambda l, j: (l, 0, j)),
                  pl.BlockSpec((1, 1, tn), lambda l, j: (l, 0, j))],
        out_specs=pl.BlockSpec((1, c_all.shape[0], tn), lambda l, j: (l, 0, j)),
        out_shape=_f32((depth, c_all.shape[0], n)),
        compiler_params=_params(2), name="mod")(c_all, w_mod, b_mod.reshape(depth, 1, n))


def _mod_spec(n_ptiles, tiles_per_dec):
    def index(i):
        return (jnp.where(i < n_ptiles, 0, 1 + (i - n_ptiles) // tiles_per_dec), 0, 0)
    return pl.BlockSpec((1, 6, D_MODEL), index)


def _proj_call(x_all, mod_l, ws, n_ptiles, tiles_per_dec):
    n_rows = x_all.shape[0]
    row = lambda i: (i, 0)
    widths = [w.shape[1] for w in ws[:5]]
    per_tile = ROW_TILE // CHUNK
    return pl.pallas_call(
        proj_kernel, grid=(n_rows // ROW_TILE,),
        in_specs=[pl.BlockSpec((ROW_TILE, D_MODEL), row), _mod_spec(n_ptiles, tiles_per_dec)]
                 + [_const_spec(w) for w in ws],
        out_specs=[pl.BlockSpec((ROW_TILE, w), row) for w in widths]
                  + [pl.BlockSpec((per_tile, N_GATES, CHUNK), lambda i: (i, 0, 0))],
        out_shape=[_f32((n_rows, w)) for w in widths] + [_f32((n_rows // CHUNK, N_GATES, CHUNK))],
        compiler_params=_params(1), name="proj")(x_all, mod_l, *ws)


def _post_call(h_all, ys, mod_l, ws, n_ptiles, tiles_per_dec):
    n_rows = h_all.shape[0]
    row = lambda i: (i, 0)
    return pl.pallas_call(
        post_kernel, grid=(n_rows // ROW_TILE,),
        in_specs=[pl.BlockSpec((ROW_TILE, D_MODEL), row)]
                 + [pl.BlockSpec((ROW_TILE, GROUP_W), row) for _ in ys]
                 + [_mod_spec(n_ptiles, tiles_per_dec)] + [_const_spec(w) for w in ws],
        out_specs=pl.BlockSpec((ROW_TILE, D_MODEL), row),
        out_shape=_f32((n_rows, D_MODEL)),
        compiler_params=_params(1), name="post")(h_all, *ys, mod_l, *ws)


def _state_spec(shape_tail, layer):
    nd = len(shape_tail)
    return pl.BlockSpec((1, 1) + shape_tail, lambda i: (i, layer) + (0,) * nd)


def _out_state_spec(shape_tail):
    nd = len(shape_tail)
    return pl.BlockSpec((1,) + shape_tail, lambda i: (i,) + (0,) * nd)


MAT = (2, N_HEADS, HEAD_DIM, HEAD_DIM)
VEC = (2, N_HEADS, 1, HEAD_DIM)
SCL = (2, N_HEADS, 1, 1)
BD_STATE = pltpu.VMEM((2, GROUP_W, GROUP_W), jnp.float32)


def _mixer_call(kernel, name, t_len, n_seq, blk_off, seq_inputs, const_inputs, state_inputs, layer,
                state_tails, scratch):
    nc = t_len // CHUNK
    in_specs = []
    for a in seq_inputs:
        if a.ndim == 2:
            in_specs.append(pl.BlockSpec((t_len, a.shape[1]), lambda i: (i + blk_off, 0)))
        else:
            in_specs.append(pl.BlockSpec((nc,) + a.shape[1:], lambda i: (i + blk_off, 0, 0)))
    in_specs += [_const_spec(a) for a in const_inputs]
    in_specs += [_state_spec(a.shape[2:], layer) for a in state_inputs]
    out_specs = [pl.BlockSpec((t_len, GROUP_W), lambda i: (i, 0))] + [_out_state_spec(t) for t in state_tails]
    out_shape = [_f32((n_seq * t_len, GROUP_W))] + [_f32((n_seq,) + t) for t in state_tails]
    return pl.pallas_call(
        kernel, grid=(n_seq,), in_specs=in_specs, out_specs=out_specs, out_shape=out_shape,
        scratch_shapes=[pltpu.VMEM((t_len, GROUP_W), jnp.float32)] + scratch,
        compiler_params=_params(1), name=name)(*seq_inputs, *const_inputs, *state_inputs)


def _rope_tables(t_len):
    nf = HEAD_DIM // 4
    inv = ROPE_BASE ** (-jnp.arange(nf, dtype=jnp.float32) / nf)
    t = jnp.arange(t_len)
    ang_r = (t // GRID_W).astype(jnp.float32)[:, None] * inv[None, :]
    ang_c = (t % GRID_W).astype(jnp.float32)[:, None] * inv[None, :]
    cos = jnp.concatenate([jnp.cos(ang_r)] * 2 + [jnp.cos(ang_c)] * 2, axis=1)
    sin = jnp.concatenate([-jnp.sin(ang_r), jnp.sin(ang_r), -jnp.sin(ang_c), jnp.sin(ang_c)], axis=1)
    return jnp.tile(cos, (1, N_HEADS)), jnp.tile(sin, (1, N_HEADS))


def _lane_row(vals, lo, n_rows=1, row=0, width=GATE_LANES):
    out = jnp.zeros((n_rows, width), jnp.float32)
    return out.at[row, lo:lo + vals.size].set(vals.reshape(-1))


def kernel(x_prompt, x_sample, c, state_mlstm_C, state_mlstm_n, state_mlstm_m, state_delta, state_ret,
           state_rwkv, c_ctx, w_mod, b_mod, w_in, w_out, ln1_g, ln1_b, ln2_g, ln2_b, w_ff1, w_ff2,
           mlstm_i_bias, mlstm_f_bias, mlstm_norm, delta_conv, delta_a_log, delta_dt_bias, delta_norm,
           ret_decay, ret_norm, rwkv_mu, rwkv_w0, rwkv_w2, rwkv_a0, rwkv_a2, rwkv_g2, rwkv_kk, rwkv_ka,
           rwkv_rk, rwkv_norm):
    f32 = jnp.float32
    batch, seq, d = x_prompt.shape
    dec_batch, dec_seq, _ = x_sample.shape
    assert d == D_MODEL and seq % ROW_TILE == 0 and dec_seq % ROW_TILE == 0
    assert (seq // CHUNK) % 2 == 0 and (dec_seq // CHUNK) % 2 == 0 and (batch * seq) % dec_seq == 0
    n_prows = batch * seq
    n_ptiles = n_prows // ROW_TILE
    tiles_per_dec = dec_seq // ROW_TILE

    c_all = jnp.zeros((8, d), f32).at[0].set(c_ctx).at[1:1 + dec_batch].set(c)
    mod = _mod_call(c_all, w_mod, b_mod).reshape(DEPTH, 8, 6, d)
    h_all = jnp.concatenate([x_prompt.reshape(n_prows, d), x_sample.reshape(dec_batch * dec_seq, d)], axis=0)
    cos_t, sin_t = _rope_tables(dec_seq)

    zmat = jnp.zeros((batch, 1) + MAT, f32)
    zn = jnp.zeros((batch, 1) + VEC, f32)
    zm = jnp.zeros((batch, 1) + SCL, f32)
    n_cache = state_mlstm_n.reshape(state_mlstm_n.shape[:-1] + (1, HEAD_DIM))
    m_cache = state_mlstm_m.reshape(state_mlstm_m.shape + (1, 1))
    groups = ((seq, batch, 0), (dec_seq, dec_batch, n_prows // dec_seq))

    new_states = []
    for l in range(DEPTH):
        wl = w_in[l]
        ws = [wl[:, 0:1024], wl[:, 1040:2064], wl[:, 2080:3104], wl[:, 3104:4256],
              jnp.pad(jnp.concatenate([wl[:, 1024:1040], wl[:, 2064:2080]], axis=1), ((0, 0), (0, GATE_LANES - N_GATES)))]
        ws = [w.astype(BF) for w in ws]
        ws.append(ws[4][:, :N_GATES].T)
        pa, pb, pc, pd, pg, pgt = _proj_call(h_all, mod[l], ws, n_ptiles, tiles_per_dec)

        bias_c = _lane_row(mlstm_i_bias[l], 0) + _lane_row(mlstm_f_bias[l], 8)
        bias_r = bias_c[0, :N_GATES].reshape(N_GATES, 1)
        dl_row = _lane_row(delta_a_log[l], 24, 2, 0) + _lane_row(delta_dt_bias[l], 24, 2, 1)
        dl_col = dl_row[:, :N_GATES].T
        rd_row = _lane_row(ret_decay[l], 0)
        rw_vec = jnp.stack([rwkv_kk[l], rwkv_ka[l], rwkv_rk[l].reshape(-1), rwkv_norm[l]], axis=0)
        row = lambda a: a.reshape(1, -1)

        ys = [[], [], [], []]
        layer_states = None
        for gi, (t_len, n_seq, off) in enumerate(groups):
            nc = t_len // CHUNK
            g = _scan_steps(nc)
            ctx = gi == 0
            lay = 0 if ctx else l
            ya, c1, n1, m1 = _mixer_call(
                functools.partial(mlstm_kernel, nc=nc), "mlstm", t_len, n_seq, off, [pa, pg, pgt],
                [bias_c, bias_r, row(mlstm_norm[l])],
                [zmat, zn, zm] if ctx else [state_mlstm_C, n_cache, m_cache], lay, [MAT, VEC, SCL],
                [BD_STATE, pltpu.VMEM((2, 1, GROUP_W), f32), pltpu.VMEM((2, 1, GROUP_W), f32)])
            yb, sb1 = _mixer_call(
                functools.partial(delta_kernel, nc=nc, g=g), "delta", t_len, n_seq, off, [pb, pg, pgt],
                [delta_conv[l], dl_row, dl_col, row(delta_norm[l])],
                [zmat if ctx else state_delta], lay, [MAT],
                [BD_STATE, pltpu.VMEM((2 * g, CHUNK + 16, 3 * GROUP_W), f32)])
            ret_consts = [rd_row, row(ret_norm[l])]
            if not ctx:
                ret_consts = [cos_t, sin_t] + ret_consts
            yc, sc1 = _mixer_call(
                functools.partial(ret_kernel, nc=nc, rope=not ctx), "ret", t_len, n_seq, off, [pc],
                ret_consts, [zmat if ctx else state_ret], lay, [MAT],
                [BD_STATE, pltpu.VMEM((2, 4, CHUNK, GROUP_W), f32)])
            yd, sd1 = _mixer_call(
                functools.partial(rwkv_kernel, nc=nc, g=g), "rwkv", t_len, n_seq, off, [pd],
                [row(rwkv_mu[l]), rwkv_w0[l], rwkv_w2[l], rwkv_a0[l], rwkv_a2[l], rwkv_g2[l], rw_vec],
                [zmat if ctx else state_rwkv], lay, [MAT],
                [BD_STATE, pltpu.VMEM((2 * g, CHUNK + 16, pd.shape[1]), f32)])
            for lst, y in zip(ys, (ya, yb, yc, yd)):
                lst.append(y)
            if ctx:
                layer_states = (c1, n1.reshape(n1.shape[:3] + (HEAD_DIM,)), m1.reshape(m1.shape[:3]), sb1, sc1, sd1)
        new_states.append(layer_states)
        ys = [jnp.concatenate(y, axis=0) for y in ys]
        post_ws = [w_out[l].astype(BF), row(ln1_g[l]), row(ln1_b[l]), w_ff1[l].astype(BF), w_ff2[l].astype(BF),
                   row(ln2_g[l]), row(ln2_b[l])]
        h_all = _post_call(h_all, ys, mod[l], post_ws, n_ptiles, tiles_per_dec)

    outs = [jnp.stack([st[i] for st in new_states], axis=1) for i in range(6)]
    return (h_all[:n_prows].reshape(batch, seq, d), h_all[n_prows:].reshape(dec_batch, dec_seq, d), *outs)
```

```python
import functools

import jax
import jax.numpy as jnp
from jax import lax
from jax.experimental import pallas as pl
from jax.experimental.pallas import tpu as pltpu

D_MODEL = 1024
DEPTH = 2
GRID_W = 64
HEAD_DIM = 64
N_HEADS = 4
GROUP_W = N_HEADS * HEAD_DIM
CHUNK = 64
CONV_K = 5
ROPE_BASE = 10000.0
LN_EPS = 1e-5
DEEPNORM_ALPHA = (2 * DEPTH) ** 0.25
RWKV_LORA = 64
GATE_LANES = 128
N_GATES = 32
ROW_TILE = 512
VMEM_LIMIT = 56 * 1024 * 1024
MAX_SCAN_STEPS = 4

NEG_INF = float("-inf")
BF = jnp.bfloat16
_NN = (((1,), (0,)), ((), ()))
_NT = (((1,), (1,)), ((), ()))
_TN = (((0,), (0,)), ((), ()))


def _dot(a, b, dims):
    return lax.dot_general(a, b, dims, preferred_element_type=jnp.result_type(a.dtype, jnp.float32))


def _bf(x):
    return x.astype(BF)


def _split(x, n):
    parts = []
    for _ in range(n - 1):
        p = _bf(x)
        parts.append(p)
        x = x - p.astype(x.dtype)
    parts.append(_bf(x))
    return parts


def bmm(a, b, dims=_NN):
    return _dot(_bf(a), _bf(b), dims)


def _expand(x, bdm):
    return jnp.concatenate([x] * N_HEADS, axis=0) * bdm[0].astype(x.dtype)


def wmm(a, b, bdm, passes=1, nt=False):
    dims = _NT if nt else _NN
    if passes == 1:
        return _dot(_bf(a), _expand(_bf(b), bdm), dims)
    ah, al = _split(a, 2)
    bh, bl = _split(b, 2)
    ebh = _expand(bh, bdm)
    lhs = jnp.concatenate([ah, ah, al], axis=1)
    rhs = jnp.concatenate([ebh, _expand(bl, bdm), ebh], axis=1 if nt else 0)
    return _dot(lhs, rhs, dims)


def cumsum_col(mc, x):
    return _dot(jnp.concatenate([_bf(mc)] * 3, axis=1), jnp.concatenate(_split(x, 3), axis=0), _NN)


def cumsum_row(x, mc):
    return _dot(jnp.concatenate(_split(x, 3), axis=1), jnp.concatenate([_bf(mc)] * 3, axis=1), _NT)


def head_sum(x, bd):
    return _dot(jnp.concatenate(_split(x, 2), axis=1), jnp.concatenate([_bf(bd)] * 2, axis=0), _NN)


def sigmoid(x):
    return 1.0 / (1.0 + jnp.exp(-x))


def silu(x):
    return x * sigmoid(x)


def softplus(x):
    return jnp.maximum(x, 0.0) + jnp.log(1.0 + jnp.exp(-jnp.abs(x)))


def log_sigmoid(x):
    return -softplus(-x)


def head_norm(y, gain, bd, center):
    if center:
        y = y - head_sum(y, bd) * (1.0 / HEAD_DIM)
    ms = head_sum(y * y, bd) * (1.0 / HEAD_DIM)
    return y * lax.rsqrt(ms + LN_EPS) * gain


def l2norm_heads(x, bd):
    return x * lax.rsqrt(head_sum(x * x, bd) + 1e-6)


def layer_norm(x):
    mu = jnp.mean(x, axis=-1, keepdims=True)
    xc = x - mu
    var = jnp.mean(xc * xc, axis=-1, keepdims=True)
    return xc * lax.rsqrt(var + LN_EPS)


def colw(x, idx):
    m = x.shape[0]
    return jnp.concatenate([jnp.broadcast_to(x[:, j:j + 1], (m, HEAD_DIM)) for j in idx], axis=1)


def roww(x, idx):
    return jnp.concatenate([x[j:j + 1, :] for j in idx], axis=1)


def segmax(x):
    m = x.shape[0]
    return jnp.concatenate(
        [jnp.broadcast_to(jnp.max(x[:, h * HEAD_DIM:(h + 1) * HEAD_DIM], axis=1, keepdims=True), (m, HEAD_DIM))
         for h in range(N_HEADS)], axis=1)


def wide_masks(rev, dtype=jnp.float32):
    r = lax.broadcasted_iota(jnp.int32, (CHUNK, GROUP_W), 0)
    c = jnp.bitwise_and(lax.broadcasted_iota(jnp.int32, (CHUNK, GROUP_W), 1), HEAD_DIM - 1)
    if rev:
        incl_w, strict_w = r <= c, r < c
    else:
        incl_w, strict_w = r >= c, r > c
    mc = incl_w[:, :CHUNK].astype(dtype)
    return incl_w, strict_w, mc, (r == c).astype(dtype)


def block_diag_mask():
    r = jnp.right_shift(lax.broadcasted_iota(jnp.int32, (GROUP_W, GROUP_W), 0), 6)
    c = jnp.right_shift(lax.broadcasted_iota(jnp.int32, (GROUP_W, GROUP_W), 1), 6)
    m = (r == c).astype(jnp.float32)
    return _bf(m), m


def inv_unit_tri_w(a, eye_w, bdm):
    p = _bf(-a)
    e = _expand(p, bdm)
    t = eye_w - a
    for _ in range(5):
        p = _bf(_dot(p, e, _NN))
        e = _expand(p, bdm)
        t = t + _dot(_bf(t), e, _NN)
    r = eye_w - t - wmm(a, t, bdm, 3)
    return t + wmm(t, r, bdm, 1)


def mlstm_chunk(q, k, v, bcol_w, brow_w, icol_w, irow_w, bl_w, c_bd, n_row, m_w, incl_w, bdm, bd):
    logd = jnp.where(incl_w, bcol_w - brow_w + irow_w, NEG_INF)
    inter = bcol_w + m_w
    mt = jnp.maximum(inter, segmax(logd))
    s = wmm(q, k, bdm, 1, nt=True) * jnp.exp(logd - mt)
    ew = jnp.exp(inter - mt)
    num = wmm(s, v, bdm, 1) + bmm(q, c_bd) * ew
    den = head_sum(s, bd) + head_sum(q * n_row, bd) * ew
    h = num / jnp.maximum(jnp.abs(den), jnp.exp(-mt))
    wk = bl_w - bcol_w + icol_w
    m_new = jnp.maximum(bl_w + m_w, jnp.max(wk, axis=0, keepdims=True))
    sk = jnp.exp(wk - m_new)
    sp = jnp.exp(bl_w + m_w - m_new)
    ks = k * sk
    c_new = c_bd * sp + bmm(ks, v, _TN) * bdm[1]
    n_new = n_row * sp + jnp.sum(ks, axis=0, keepdims=True)
    return h, c_new, n_new, m_new


def delta_pre(q, k, v, beta_w, gcol_w, grow_w, gl_w, incl_w, strict_w, eye_w, bdm):
    decay = jnp.exp(jnp.where(incl_w, gcol_w - grow_w, NEG_INF))
    a = jnp.where(strict_w, wmm(k, k, bdm, 1, nt=True) * decay, 0.0) * beta_w
    t = inv_unit_tri_w(a, eye_w, bdm)
    eg = jnp.exp(gcol_w)
    u = wmm(t, v * beta_w, bdm, 1)
    w = wmm(t, k * (beta_w * eg), bdm, 1)
    qk = wmm(q, k, bdm, 1, nt=True) * decay
    return u, jnp.concatenate([w, q * eg], axis=0), qk, k * jnp.exp(gl_w - gcol_w), jnp.exp(gl_w)


def delta_step(u, wq, qk, kd, egl, s_bd, bdm):
    ws = bmm(wq, s_bd)
    v_new = u - ws[:CHUNK]
    o = ws[CHUNK:] + wmm(qk, v_new, bdm, 1)
    s_new = s_bd * egl + bmm(kd, v_new, _TN) * bdm[1]
    return o, s_new


def retention_chunk(q, k, v, dmat_w, xi_w, zeta_w, gl_w, s_bd, bdm):
    sc = wmm(q, k, bdm, 1, nt=True) * dmat_w
    o = wmm(sc, v, bdm, 1) + bmm(q, s_bd) * xi_w
    s_new = s_bd * gl_w + bmm(k * zeta_w, v, _TN) * bdm[1]
    return o, s_new


def rwkv_pre(rp, khp, ktn, bn, v, incl_w, strict_w, eye_w, bdm):
    lhs = jnp.concatenate([khp, rp], axis=0)
    ab_kt = wmm(lhs, ktn, bdm, 1, nt=True)
    ab_b = wmm(lhs, bn, bdm, 1, nt=True)
    a_kt = jnp.where(strict_w, ab_kt[:CHUNK], 0.0)
    b_kt = jnp.where(incl_w, ab_kt[CHUNK:], 0.0)
    a_b = jnp.where(strict_w, ab_b[:CHUNK], 0.0)
    b_b = jnp.where(incl_w, ab_b[CHUNK:], 0.0)
    t = inv_unit_tri_w(a_b, eye_w, bdm)
    tk = wmm(t, khp, bdm, 1)
    tav = wmm(t, wmm(a_kt, v, bdm, 1), bdm, 1)
    return jnp.concatenate([tk, rp], axis=0), tav, wmm(b_kt, v, bdm, 1), b_b


def rwkv_step(tkr, tav, o1, b_b, v, kbe, etot, s_bd, bdm):
    ls = bmm(tkr, s_bd, _NT)
    u = ls[:CHUNK] + tav
    o = ls[CHUNK:] + o1 - wmm(b_b, u, bdm, 1)
    upd = bmm(jnp.concatenate([v, -u], axis=0), kbe, _TN)
    s_new = s_bd * etot + upd * bdm[1]
    return o, s_new


def _two_pass(nc, body):
    half = nc // 2

    def first(i, carry):
        body(i, True)
        return carry

    def second(i, carry):
        body(i, False)
        return carry

    lax.fori_loop(0, half, first, 0)
    lax.fori_loop(half, nc, second, 0)


def _chunk_pos(i, z, nc):
    c = i if z == 0 else nc - 1 - i
    return c, pl.multiple_of(c * CHUNK, CHUNK)


def _group_loop(nc, g, body):
    n_it = nc // g

    def run(first):
        def it(i, carry):
            body(i, first)
            return carry
        return it

    lax.fori_loop(0, n_it // 2, run(True), 0)
    lax.fori_loop(n_it // 2, n_it, run(False), 0)


def _scan_steps(nc):
    return max(g for g in range(1, MAX_SCAN_STEPS + 1) if nc % (2 * g) == 0)


def _group_chunks(i, g, nc):
    out = []
    for z in range(2):
        for j in range(g):
            step = i * g + j
            c = step if z == 0 else nc - 1 - step
            out.append((z, c, pl.multiple_of(c * CHUNK, CHUNK)))
    return out


def _group_masks(g):
    shape = (2 * g, CHUNK, GROUP_W)
    rev = lax.broadcasted_iota(jnp.int32, shape, 0) >= g
    r = lax.broadcasted_iota(jnp.int32, shape, 1)
    c = jnp.bitwise_and(lax.broadcasted_iota(jnp.int32, shape, 2), HEAD_DIM - 1)
    d = jnp.where(rev, c - r, r - c)
    return d >= 0, d > 0, (d == 0).astype(jnp.float32)


def _stack(items):
    return [jnp.stack(x) for x in zip(*items)]


def _pick(arrs, g, j):
    return [jnp.stack([a[j], a[g + j]]) for a in arrs]


def _halo(ebuf, ref, c, t0, nc):
    t_len = nc * CHUNK
    width = ebuf.shape[1]
    ps = pl.multiple_of(jnp.maximum(t0 - 8, 0), 8)
    ns = pl.multiple_of(jnp.minimum(t0 + CHUNK, t_len - 8), 8)
    ebuf[0:8, :] = jnp.where(c > 0, ref[pl.ds(ps, 8), 0:width], 0.0)
    ebuf[8:8 + CHUNK, :] = ref[pl.ds(t0, CHUNK), 0:width]
    ebuf[8 + CHUNK:, :] = jnp.where(c < nc - 1, ref[pl.ds(ns, 8), 0:width], 0.0)


def _hs(h):
    return slice(h * HEAD_DIM, (h + 1) * HEAD_DIM)


def _load_state(sbd, s0_ref):
    for z in range(2):
        sbd[z] = jnp.zeros((GROUP_W, GROUP_W), jnp.float32)
        for h in range(N_HEADS):
            sbd[z, _hs(h), _hs(h)] = s0_ref[0, 0, z, h]


def _store_state(s1_ref, sbd):
    for z in range(2):
        for h in range(N_HEADS):
            s1_ref[0, z, h] = sbd[z, _hs(h), _hs(h)]


def _merge(first, hscr, t0, h):
    if first:
        hscr[pl.ds(t0, CHUNK), :] = h
        return None
    return hscr[pl.ds(t0, CHUNK), :] + h


def mlstm_kernel(p_ref, gc_ref, gr_ref, bc_ref, br_ref, norm_ref, c0_ref, n0_ref, m0_ref,
                 y_ref, c1_ref, n1_ref, m1_ref, hscr, cbd, nrow, mw, *, nc):
    _load_state(cbd, c0_ref)
    for z in range(2):
        for h in range(N_HEADS):
            nrow[z, :, _hs(h)] = n0_ref[0, 0, z, h]
            mw[z, :, _hs(h)] = jnp.broadcast_to(m0_ref[0, 0, z, h], (1, HEAD_DIM))
    bdm = block_diag_mask()
    bd = bdm[1]
    mcs = [wide_masks(False)[2], wide_masks(True)[2]]
    incl2 = _group_masks(1)[0]

    def prep(z, c, t0):
        x = p_ref[pl.ds(t0, CHUNK), 0:3 * GROUP_W]
        g_c = gc_ref[pl.ds(t0, CHUNK), :] + bc_ref[...]
        g_r = gr_ref[c] + br_ref[...]
        lf_c = log_sigmoid(g_c)
        idx_i = [z * N_HEADS + h for h in range(N_HEADS)]
        idx_f = [8 + j for j in idx_i]
        return (x[:, :GROUP_W], x[:, GROUP_W:2 * GROUP_W] * HEAD_DIM ** -0.5, x[:, 2 * GROUP_W:],
                colw(cumsum_col(mcs[z], lf_c), idx_f), roww(cumsum_row(log_sigmoid(g_r), mcs[z]), idx_f),
                colw(g_c, idx_i), roww(g_r, idx_i), colw(jnp.sum(lf_c, axis=0, keepdims=True), idx_f))

    def body(i, first):
        chunks = _group_chunks(i, 1, nc)
        args = _stack([prep(z, c, t0) for z, c, t0 in chunks])
        hh, c_new, n_new, m_new = jax.vmap(mlstm_chunk, in_axes=(0,) * 12 + (None, None))(
            *args, cbd[...], nrow[...], mw[...], incl2, bdm, bd)
        cbd[...] = c_new
        nrow[...] = n_new
        mw[...] = m_new
        for z, _, t0 in chunks:
            tot = _merge(first, hscr, t0, hh[z])
            if not first:
                og = p_ref[pl.ds(t0, CHUNK), 3 * GROUP_W:]
                y_ref[pl.ds(t0, CHUNK), :] = sigmoid(og) * head_norm(tot, norm_ref[...], bd, True)

    _two_pass(nc, body)
    _store_state(c1_ref, cbd)
    for z in range(2):
        for h in range(N_HEADS):
            n1_ref[0, z, h] = nrow[z, :, _hs(h)]
            m1_ref[0, z, h] = mw[z, :, h * HEAD_DIM:h * HEAD_DIM + 1]


def delta_kernel(p_ref, gc_ref, gr_ref, conv_ref, prow_ref, pcol_ref, norm_ref, s0_ref,
                 y_ref, s1_ref, hscr, sbd, ebuf, *, nc, g):
    _load_state(sbd, s0_ref)
    bdm = block_diag_mask()
    bd = bdm[1]
    mcs = [wide_masks(False)[2], wide_masks(True)[2]]
    gmasks = _group_masks(g)
    nega_row = -jnp.exp(prow_ref[0:1, :])
    dtb_row = prow_ref[1:2, :]
    nega_col = -jnp.exp(pcol_ref[:, 0:1])
    dtb_col = pcol_ref[:, 1:2]
    qkv_w = 3 * GROUP_W

    def prep(slot, z, c, t0):
        eb = ebuf.at[slot]
        _halo(eb, p_ref, c, t0, nc)
        acc = eb[6:6 + CHUNK, :] * conv_ref[0:1, :]
        for j in range(1, CONV_K):
            acc = acc + eb[6 + j:6 + j + CHUNK, :] * conv_ref[j:j + 1, :]
        qkv = silu(acc)
        q = l2norm_heads(qkv[:, :GROUP_W], bd) * HEAD_DIM ** -0.5
        k = l2norm_heads(qkv[:, GROUP_W:2 * GROUP_W], bd)
        g_c = gc_ref[pl.ds(t0, CHUNK), :]
        gg_c = nega_row * softplus(g_c + dtb_row)
        gg_r = nega_col * softplus(gr_ref[c] + dtb_col)
        idx_b = [16 + z * N_HEADS + h for h in range(N_HEADS)]
        idx_a = [8 + j for j in idx_b]
        return (q, k, qkv[:, 2 * GROUP_W:], colw(sigmoid(g_c), idx_b), colw(cumsum_col(mcs[z], gg_c), idx_a),
                roww(cumsum_row(gg_r, mcs[z]), idx_a), colw(jnp.sum(gg_c, axis=0, keepdims=True), idx_a))

    def body(i, first):
        chunks = _group_chunks(i, g, nc)
        pre_in = _stack([prep(slot, z, c, t0) for slot, (z, c, t0) in enumerate(chunks)])
        pre = jax.vmap(delta_pre, in_axes=(0,) * 10 + (None,))(*pre_in, *gmasks, bdm)
        s = sbd[...]
        for j in range(g):
            o, s = jax.vmap(delta_step, in_axes=(0,) * 6 + (None,))(*_pick(pre, g, j), s, bdm)
            for z in range(2):
                t0 = chunks[z * g + j][2]
                tot = _merge(first, hscr, t0, o[z])
                if not first:
                    zg = p_ref[pl.ds(t0, CHUNK), qkv_w:]
                    y_ref[pl.ds(t0, CHUNK), :] = head_norm(tot, norm_ref[...], bd, False) * silu(zg)
        sbd[...] = s

    _group_loop(nc, g, body)
    _store_state(s1_ref, sbd)


def _rope(x, cos, sin, lane_lo):
    n = x.shape[1]
    swapped = jnp.where(lane_lo, pltpu.roll(x, n - 16, 1), pltpu.roll(x, 16, 1))
    return x * cos + swapped * sin


def ret_kernel(*refs, nc, rope):
    if rope:
        p_ref, cos_ref, sin_ref, rd_ref, norm_ref, s0_ref, y_ref, s1_ref, hscr, sbd, cst = refs
    else:
        p_ref, rd_ref, norm_ref, s0_ref, y_ref, s1_ref, hscr, sbd, cst = refs
    f32 = jnp.float32
    _load_state(sbd, s0_ref)
    bdm = block_diag_mask()
    bd = bdm[1]
    r = lax.broadcasted_iota(jnp.int32, (CHUNK, GROUP_W), 0).astype(f32)
    cc = jnp.bitwise_and(lax.broadcasted_iota(jnp.int32, (CHUNK, GROUP_W), 1), HEAD_DIM - 1).astype(f32)
    lg_row = -jnp.exp(rd_ref[...])
    for z in range(2):
        incl_w = wide_masks(z == 1)[0]
        lg = colw(lg_row, [z * N_HEADS + h for h in range(N_HEADS)])
        if z == 0:
            cst[z, 0] = jnp.exp(jnp.where(incl_w, (r - cc) * lg, NEG_INF))
            cst[z, 1] = jnp.exp((r + 1.0) * lg)
            cst[z, 2] = jnp.exp((CHUNK - 1.0 - r) * lg)
        else:
            cst[z, 0] = jnp.exp(jnp.where(incl_w, (cc - r) * lg, NEG_INF))
            cst[z, 1] = jnp.exp((CHUNK - r) * lg)
            cst[z, 2] = jnp.exp(r * lg)
        cst[z, 3] = jnp.exp(CHUNK * lg + 0.0 * r)
    if rope:
        lane = lax.broadcasted_iota(jnp.int32, (CHUNK, GROUP_W), 1)
        lane_lo = jnp.bitwise_and(lane, 31) < 16

    def prep(t0):
        x = p_ref[pl.ds(t0, CHUNK), 0:3 * GROUP_W]
        q = x[:, :GROUP_W] * HEAD_DIM ** -0.5
        k = x[:, GROUP_W:2 * GROUP_W]
        if rope:
            cos = cos_ref[pl.ds(t0, CHUNK), :]
            sin = sin_ref[pl.ds(t0, CHUNK), :]
            q = _rope(q, cos, sin, lane_lo)
            k = _rope(k, cos, sin, lane_lo)
        return q, k, x[:, 2 * GROUP_W:]

    def body(i, first):
        chunks = _group_chunks(i, 1, nc)
        args = _stack([prep(t0) for _, _, t0 in chunks])
        o, s_new = jax.vmap(retention_chunk, in_axes=(0,) * 8 + (None,))(
            *args, cst[:, 0], cst[:, 1], cst[:, 2], cst[:, 3, 0:1, :], sbd[...], bdm)
        sbd[...] = s_new
        for z, _, t0 in chunks:
            tot = _merge(first, hscr, t0, o[z])
            if not first:
                gate = p_ref[pl.ds(t0, CHUNK), 3 * GROUP_W:]
                y_ref[pl.ds(t0, CHUNK), :] = silu(gate) * head_norm(tot, norm_ref[...], bd, True)

    _two_pass(nc, body)
    _store_state(s1_ref, sbd)


def rwkv_kernel(p_ref, mu_ref, w0_ref, w2_ref, a0_ref, a2_ref, g2_ref, vec_ref, s0_ref,
                y_ref, s1_ref, hscr, sbd, ebuf, *, nc, g):
    _load_state(sbd, s0_ref)
    bdm = block_diag_mask()
    bd = bdm[1]
    mcs = [wide_masks(False)[2], wide_masks(True)[2]]
    gmasks = _group_masks(g)
    kk = vec_ref[0:1, :]
    ka = vec_ref[1:2, :]
    rk = vec_ref[2:3, :]
    gain = vec_ref[3:4, :]

    def lora_a(da, zz):
        return sigmoid(a0_ref[zz:zz + 1, :] + bmm(da[:, zz * RWKV_LORA:(zz + 1) * RWKV_LORA], a2_ref[zz]))

    def prep(slot, z, c, t0, first):
        eb = ebuf.at[slot]
        _halo(eb, p_ref, c, t0, nc)
        cur = eb[8:8 + CHUNK, :]
        pdm = cur + mu_ref[...] * (0.5 * (eb[7:7 + CHUNK, :] + eb[9:9 + CHUNK, :]) - cur)
        r = pdm[:, :GROUP_W]
        k = pdm[:, GROUP_W:2 * GROUP_W]
        v = pdm[:, 2 * GROUP_W:3 * GROUP_W]
        dw = pdm[:, 3 * GROUP_W:3 * GROUP_W + 128]
        da = pdm[:, 3 * GROUP_W + 128:3 * GROUP_W + 256]
        a_z = lora_a(da, z)
        w_pre = w0_ref[z:z + 1, :] + bmm(jnp.tanh(dw[:, z * RWKV_LORA:(z + 1) * RWKV_LORA]), w2_ref[z])
        lw = -jnp.exp(-softplus(-w_pre) - 0.5)
        kh = l2norm_heads(k * kk, bd)
        kt_z = k * (1.0 + (a_z - 1.0) * ka)
        b = a_z * kh
        cw = cumsum_col(mcs[z], lw)
        tot = jnp.sum(lw, axis=0, keepdims=True)
        en = jnp.exp(-cw)
        et = jnp.exp(tot - cw)
        fin = ()
        if not first:
            kt_o = k * (1.0 + (lora_a(da, 1 - z) - 1.0) * ka)
            bonus = head_sum(r * (kt_z + kt_o) * rk, bd) * v
            fin = (bmm(sigmoid(pdm[:, 3 * GROUP_W + 256:]), g2_ref[...]), bonus)
        return ((r * jnp.exp(cw), kh * jnp.exp(cw - lw), kt_z * en, b * en, v),
                (v, jnp.concatenate([kt_z * et, b * et], axis=0), jnp.exp(tot)), fin)

    def body(i, first):
        chunks = _group_chunks(i, g, nc)
        preps = [prep(slot, z, c, t0, first) for slot, (z, c, t0) in enumerate(chunks)]
        pre = jax.vmap(rwkv_pre, in_axes=(0,) * 8 + (None,))(*_stack([p[0] for p in preps]), *gmasks, bdm)
        step_in = _stack([p[1] for p in preps])
        s = sbd[...]
        for j in range(g):
            o, s = jax.vmap(rwkv_step, in_axes=(0,) * 8 + (None,))(*_pick(pre, g, j), *_pick(step_in, g, j), s, bdm)
            for z in range(2):
                t0 = chunks[z * g + j][2]
                tot_h = _merge(first, hscr, t0, o[z])
                if not first:
                    gate, bonus = preps[z * g + j][2]
                    y_ref[pl.ds(t0, CHUNK), :] = gate * (head_norm(tot_h, gain, bd, True) + bonus)
        sbd[...] = s

    _group_loop(nc, g, body)
    _store_state(s1_ref, sbd)


def mod_kernel(c_ref, w_ref, b_ref, o_ref):
    o_ref[0] = bmm(silu(c_ref[...]), w_ref[0]) + b_ref[0]


def _rows(is_ctx, ctx_ref, lat_ref):
    return jnp.where(is_ctx, ctx_ref[...], lat_ref[...])


def proj_kernel(xp_ref, xs_ref, mod_ref, wa, wb, wc, wd, wg, wgt, pa, pb, pc, pd, pg, pgt, *, n_ptiles):
    m = mod_ref[0]
    x = _rows(pl.program_id(0) < n_ptiles, xp_ref, xs_ref)
    u = layer_norm(x) * (1.0 + m[1:2]) + m[0:1]
    ub = u.astype(BF)
    pa[...] = _dot(ub, wa[...], _NN)
    pb[...] = _dot(ub, wb[...], _NN)
    pc[...] = _dot(ub, wc[...], _NN)
    pd[...] = _dot(ub, wd[...], _NN)
    pg[...] = _dot(ub, wg[...], _NN)
    gt = _dot(wgt[...], ub, _NT)
    for j in range(ROW_TILE // CHUNK):
        pgt[j] = gt[:, j * CHUNK:(j + 1) * CHUNK]


def post_kernel(hp, hs, yap, yas, ybp, ybs, ycp, ycs, ydp, yds, mod_ref, wout, l1g, l1b, w1, w2, l2g, l2b,
                *o_refs, n_ptiles):
    m = mod_ref[0]
    is_ctx = pl.program_id(0) < n_ptiles
    y = jnp.concatenate([_rows(is_ctx, yap, yas), _rows(is_ctx, ybp, ybs), _rows(is_ctx, ycp, ycs),
                         _rows(is_ctx, ydp, yds)], axis=1).astype(BF)
    attn = _dot(y, wout[...], _NN)
    h1 = layer_norm(DEEPNORM_ALPHA * _rows(is_ctx, hp, hs) + m[2:3] * attn) * l1g[...] + l1b[...]
    u2 = layer_norm(h1) * (1.0 + m[4:5]) + m[3:4]
    f = _dot(u2.astype(BF), w1[...], _NN)
    f = jnp.square(jnp.maximum(f, 0.0))
    f2 = _dot(f.astype(BF), w2[...], _NN)
    out = layer_norm(DEEPNORM_ALPHA * h1 + m[5:6] * f2) * l2g[...] + l2b[...]
    if len(o_refs) == 1:
        o_refs[0][...] = out
    else:
        @pl.when(is_ctx)
        def _():
            o_refs[0][...] = out

        @pl.when(jnp.logical_not(is_ctx))
        def _():
            o_refs[1][...] = out


def _params(n_grid_axes):
    return pltpu.CompilerParams(dimension_semantics=("arbitrary",) * n_grid_axes,
                                vmem_limit_bytes=VMEM_LIMIT)


def _const_spec(a):
    nd = a.ndim
    return pl.BlockSpec(a.shape, lambda *_: (0,) * nd, pipeline_mode=pl.Buffered(1))


def _f32(shape):
    return jax.ShapeDtypeStruct(shape, jnp.float32)


def _mod_call(c_all, w_mod, b_mod):
    depth, d, n = w_mod.shape
    tn = 512
    return pl.pallas_call(
        mod_kernel, grid=(depth, n // tn),
        in_specs=[pl.BlockSpec(c_all.shape, lambda l, j: (0, 0)),
                  pl.BlockSpec((1, d, tn), lambda l, j: (l, 0, j)),
                  pl.BlockSpec((1, 1, tn), lambda l, j: (l, 0, j))],
        out_specs=pl.BlockSpec((1, c_all.shape[0], tn), lambda l, j: (l, 0, j)),
        out_shape=_f32((depth, c_all.shape[0], n)),
        compiler_params=_params(2), name="mod")(c_all, w_mod, b_mod.reshape(depth, 1, n))


def _mod_spec(n_ptiles, tiles_per_dec):
    def index(i):
        return (jnp.where(i < n_ptiles, 0, 1 + (i - n_ptiles) // tiles_per_dec), 0, 0)
    return pl.BlockSpec((1, 6, D_MODEL), index)


def _pair_specs(pair, n_ptiles):
    width = pair[0].shape[1]
    off = n_ptiles if pair[0] is pair[1] else 0
    return [pl.BlockSpec((ROW_TILE, width), lambda i: (jnp.minimum(i, n_ptiles - 1), 0)),
            pl.BlockSpec((ROW_TILE, width), lambda i: (jnp.maximum(i - n_ptiles, 0) + off, 0))]


def _proj_call(x_pair, n_rows, mod_l, ws, n_ptiles, tiles_per_dec):
    row = lambda i: (i, 0)
    widths = [w.shape[1] for w in ws[:5]]
    per_tile = ROW_TILE // CHUNK
    return pl.pallas_call(
        functools.partial(proj_kernel, n_ptiles=n_ptiles), grid=(n_rows // ROW_TILE,),
        in_specs=_pair_specs(x_pair, n_ptiles) + [_mod_spec(n_ptiles, tiles_per_dec)]
                 + [_const_spec(w) for w in ws],
        out_specs=[pl.BlockSpec((ROW_TILE, w), row) for w in widths]
                  + [pl.BlockSpec((per_tile, N_GATES, CHUNK), lambda i: (i, 0, 0))],
        out_shape=[_f32((n_rows, w)) for w in widths] + [_f32((n_rows // CHUNK, N_GATES, CHUNK))],
        compiler_params=_params(1), name="proj")(*x_pair, mod_l, *ws)


def _post_call(h_pair, y_pairs, n_rows, mod_l, ws, n_ptiles, tiles_per_dec, split_out):
    pairs = [h_pair] + y_pairs
    if split_out:
        n_prows = n_ptiles * ROW_TILE
        out_specs = _pair_specs((_f32((n_prows, D_MODEL)), _f32((n_rows - n_prows, D_MODEL))), n_ptiles)
        out_shape = [_f32((n_prows, D_MODEL)), _f32((n_rows - n_prows, D_MODEL))]
    else:
        out_specs = pl.BlockSpec((ROW_TILE, D_MODEL), lambda i: (i, 0))
        out_shape = _f32((n_rows, D_MODEL))
    return pl.pallas_call(
        functools.partial(post_kernel, n_ptiles=n_ptiles), grid=(n_rows // ROW_TILE,),
        in_specs=[spec for p in pairs for spec in _pair_specs(p, n_ptiles)]
                 + [_mod_spec(n_ptiles, tiles_per_dec)] + [_const_spec(w) for w in ws],
        out_specs=out_specs, out_shape=out_shape,
        compiler_params=_params(1), name="post")(*[a for p in pairs for a in p], mod_l, *ws)


def _state_spec(shape_tail, layer):
    nd = len(shape_tail)
    return pl.BlockSpec((1, 1) + shape_tail, lambda i: (i, layer) + (0,) * nd)


def _out_state_spec(shape_tail):
    nd = len(shape_tail)
    return pl.BlockSpec((1,) + shape_tail, lambda i: (i,) + (0,) * nd)


MAT = (2, N_HEADS, HEAD_DIM, HEAD_DIM)
VEC = (2, N_HEADS, 1, HEAD_DIM)
SCL = (2, N_HEADS, 1, 1)
BD_STATE = pltpu.VMEM((2, GROUP_W, GROUP_W), jnp.float32)


def _mixer_call(kernel, name, t_len, n_seq, blk_off, seq_inputs, const_inputs, state_inputs, layer,
                state_tails, scratch):
    nc = t_len // CHUNK
    in_specs = []
    for a in seq_inputs:
        if a.ndim == 2:
            in_specs.append(pl.BlockSpec((t_len, a.shape[1]), lambda i: (i + blk_off, 0)))
        else:
            in_specs.append(pl.BlockSpec((nc,) + a.shape[1:], lambda i: (i + blk_off, 0, 0)))
    in_specs += [_const_spec(a) for a in const_inputs]
    in_specs += [_state_spec(a.shape[2:], layer) for a in state_inputs]
    out_specs = [pl.BlockSpec((t_len, GROUP_W), lambda i: (i, 0))] + [_out_state_spec(t) for t in state_tails]
    out_shape = [_f32((n_seq * t_len, GROUP_W))] + [_f32((n_seq,) + t) for t in state_tails]
    return pl.pallas_call(
        kernel, grid=(n_seq,), in_specs=in_specs, out_specs=out_specs, out_shape=out_shape,
        scratch_shapes=[pltpu.VMEM((t_len, GROUP_W), jnp.float32)] + scratch,
        compiler_params=_params(1), name=name)(*seq_inputs, *const_inputs, *state_inputs)


def _rope_tables(t_len):
    nf = HEAD_DIM // 4
    inv = ROPE_BASE ** (-jnp.arange(nf, dtype=jnp.float32) / nf)
    t = jnp.arange(t_len)
    ang_r = (t // GRID_W).astype(jnp.float32)[:, None] * inv[None, :]
    ang_c = (t % GRID_W).astype(jnp.float32)[:, None] * inv[None, :]
    cos = jnp.concatenate([jnp.cos(ang_r)] * 2 + [jnp.cos(ang_c)] * 2, axis=1)
    sin = jnp.concatenate([-jnp.sin(ang_r), jnp.sin(ang_r), -jnp.sin(ang_c), jnp.sin(ang_c)], axis=1)
    return jnp.tile(cos, (1, N_HEADS)), jnp.tile(sin, (1, N_HEADS))


def _lane_row(vals, lo, n_rows=1, row=0, width=GATE_LANES):
    out = jnp.zeros((n_rows, width), jnp.float32)
    return out.at[row, lo:lo + vals.size].set(vals.reshape(-1))


def kernel(x_prompt, x_sample, c, state_mlstm_C, state_mlstm_n, state_mlstm_m, state_delta, state_ret,
           state_rwkv, c_ctx, w_mod, b_mod, w_in, w_out, ln1_g, ln1_b, ln2_g, ln2_b, w_ff1, w_ff2,
           mlstm_i_bias, mlstm_f_bias, mlstm_norm, delta_conv, delta_a_log, delta_dt_bias, delta_norm,
           ret_decay, ret_norm, rwkv_mu, rwkv_w0, rwkv_w2, rwkv_a0, rwkv_a2, rwkv_g2, rwkv_kk, rwkv_ka,
           rwkv_rk, rwkv_norm):
    f32 = jnp.float32
    batch, seq, d = x_prompt.shape
    dec_batch, dec_seq, _ = x_sample.shape
    assert d == D_MODEL and (batch * seq) % ROW_TILE == 0 and dec_seq % ROW_TILE == 0 and seq % CHUNK == 0
    assert (seq // CHUNK) % 2 == 0 and (dec_seq // CHUNK) % 2 == 0 and (batch * seq) % dec_seq == 0
    n_prows = batch * seq
    n_ptiles = n_prows // ROW_TILE
    tiles_per_dec = dec_seq // ROW_TILE

    c_all = jnp.zeros((8, d), f32).at[0].set(c_ctx).at[1:1 + dec_batch].set(c)
    mod = _mod_call(c_all, w_mod, b_mod).reshape(DEPTH, 8, 6, d)
    n_rows = n_prows + dec_batch * dec_seq
    h_pair = (x_prompt.reshape(n_prows, d), x_sample.reshape(dec_batch * dec_seq, d))
    cos_t, sin_t = _rope_tables(dec_seq)

    zmat = jnp.zeros((batch, 1) + MAT, f32)
    zn = jnp.zeros((batch, 1) + VEC, f32)
    zm = jnp.zeros((batch, 1) + SCL, f32)
    n_cache = state_mlstm_n.reshape(state_mlstm_n.shape[:-1] + (1, HEAD_DIM))
    m_cache = state_mlstm_m.reshape(state_mlstm_m.shape + (1, 1))
    groups = ((seq, batch, 0), (dec_seq, dec_batch, n_prows // dec_seq))

    new_states = []
    for l in range(DEPTH):
        wl = w_in[l]
        ws = [wl[:, 0:1024], wl[:, 1040:2064], wl[:, 2080:3104], wl[:, 3104:4256],
              jnp.pad(jnp.concatenate([wl[:, 1024:1040], wl[:, 2064:2080]], axis=1), ((0, 0), (0, GATE_LANES - N_GATES)))]
        ws = [w.astype(BF) for w in ws]
        ws.append(ws[4][:, :N_GATES].T)
        pa, pb, pc, pd, pg, pgt = _proj_call(h_pair, n_rows, mod[l], ws, n_ptiles, tiles_per_dec)

        bias_c = _lane_row(mlstm_i_bias[l], 0) + _lane_row(mlstm_f_bias[l], 8)
        bias_r = bias_c[0, :N_GATES].reshape(N_GATES, 1)
        dl_row = _lane_row(delta_a_log[l], 24, 2, 0) + _lane_row(delta_dt_bias[l], 24, 2, 1)
        dl_col = dl_row[:, :N_GATES].T
        rd_row = _lane_row(ret_decay[l], 0)
        rw_vec = jnp.stack([rwkv_kk[l], rwkv_ka[l], rwkv_rk[l].reshape(-1), rwkv_norm[l]], axis=0)
        row = lambda a: a.reshape(1, -1)

        ys = [[], [], [], []]
        layer_states = None
        for gi, (t_len, n_seq, off) in enumerate(groups):
            nc = t_len // CHUNK
            g = _scan_steps(nc)
            ctx = gi == 0
            lay = 0 if ctx else l
            ya, c1, n1, m1 = _mixer_call(
                functools.partial(mlstm_kernel, nc=nc), "mlstm", t_len, n_seq, off, [pa, pg, pgt],
                [bias_c, bias_r, row(mlstm_norm[l])],
                [zmat, zn, zm] if ctx else [state_mlstm_C, n_cache, m_cache], lay, [MAT, VEC, SCL],
                [BD_STATE, pltpu.VMEM((2, 1, GROUP_W), f32), pltpu.VMEM((2, 1, GROUP_W), f32)])
            yb, sb1 = _mixer_call(
                functools.partial(delta_kernel, nc=nc, g=g), "delta", t_len, n_seq, off, [pb, pg, pgt],
                [delta_conv[l], dl_row, dl_col, row(delta_norm[l])],
                [zmat if ctx else state_delta], lay, [MAT],
                [BD_STATE, pltpu.VMEM((2 * g, CHUNK + 16, 3 * GROUP_W), f32)])
            ret_consts = [rd_row, row(ret_norm[l])]
            if not ctx:
                ret_consts = [cos_t, sin_t] + ret_consts
            yc, sc1 = _mixer_call(
                functools.partial(ret_kernel, nc=nc, rope=not ctx), "ret", t_len, n_seq, off, [pc],
                ret_consts, [zmat if ctx else state_ret], lay, [MAT],
                [BD_STATE, pltpu.VMEM((2, 4, CHUNK, GROUP_W), f32)])
            yd, sd1 = _mixer_call(
                functools.partial(rwkv_kernel, nc=nc, g=g), "rwkv", t_len, n_seq, off, [pd],
                [row(rwkv_mu[l]), rwkv_w0[l], rwkv_w2[l], rwkv_a0[l], rwkv_a2[l], rwkv_g2[l], rw_vec],
                [zmat if ctx else state_rwkv], lay, [MAT],
                [BD_STATE, pltpu.VMEM((2 * g, CHUNK + 16, pd.shape[1]), f32)])
            for lst, y in zip(ys, (ya, yb, yc, yd)):
                lst.append(y)
            if ctx:
                layer_states = (c1, n1.reshape(n1.shape[:3] + (HEAD_DIM,)), m1.reshape(m1.shape[:3]), sb1, sc1, sd1)
        new_states.append(layer_states)
        post_ws = [w_out[l].astype(BF), row(ln1_g[l]), row(ln1_b[l]), w_ff1[l].astype(BF), w_ff2[l].astype(BF),
                   row(ln2_g[l]), row(ln2_b[l])]
        last = l == DEPTH - 1
        h_new = _post_call(h_pair, [tuple(y) for y in ys], n_rows, mod[l], post_ws, n_ptiles, tiles_per_dec, last)
        h_pair = tuple(h_new) if last else (h_new, h_new)

    outs = [jnp.stack([st[i] for st in new_states], axis=1) for i in range(6)]
    return (h_pair[0].reshape(batch, seq, d), h_pair[1].reshape(dec_batch, dec_seq, d), *outs)
```

```python
import functools

import jax
import jax.numpy as jnp
from jax import lax
from jax.experimental import pallas as pl
from jax.experimental.pallas import tpu as pltpu

D_MODEL = 1024
DEPTH = 2
GRID_W = 64
HEAD_DIM = 64
N_HEADS = 4
GROUP_W = N_HEADS * HEAD_DIM
CHUNK = 64
CONV_K = 5
ROPE_BASE = 10000.0
LN_EPS = 1e-5
DEEPNORM_ALPHA = (2 * DEPTH) ** 0.25
RWKV_LORA = 64
GATE_LANES = 128
N_GATES = 32
ROW_TILE = 512
VMEM_LIMIT = 56 * 1024 * 1024
MAX_SCAN_STEPS = 4
PRE_BATCH = 8
HALO = CHUNK + 16

NEG_INF = float("-inf")
BF = jnp.bfloat16
_NN = (((1,), (0,)), ((), ()))
_NT = (((1,), (1,)), ((), ()))
_TN = (((0,), (0,)), ((), ()))


def _dot(a, b, dims):
    return lax.dot_general(a, b, dims, preferred_element_type=jnp.result_type(a.dtype, jnp.float32))


def _bf(x):
    return x.astype(BF)


def _split(x, n):
    parts = []
    for _ in range(n - 1):
        p = _bf(x)
        parts.append(p)
        x = x - p.astype(x.dtype)
    parts.append(_bf(x))
    return parts


def bmm(a, b, dims=_NN):
    return _dot(_bf(a), _bf(b), dims)


def _expand(x, bdm):
    return jnp.concatenate([x] * N_HEADS, axis=0) * bdm[0].astype(x.dtype)


def wmm(a, b, bdm, passes=1, nt=False):
    dims = _NT if nt else _NN
    if passes == 1:
        return _dot(_bf(a), _expand(_bf(b), bdm), dims)
    ah, al = _split(a, 2)
    bh, bl = _split(b, 2)
    ebh = _expand(bh, bdm)
    lhs = jnp.concatenate([ah, ah, al], axis=1)
    rhs = jnp.concatenate([ebh, _expand(bl, bdm), ebh], axis=1 if nt else 0)
    return _dot(lhs, rhs, dims)


def cumsum_col(mc, x):
    return _dot(jnp.concatenate([_bf(mc)] * 3, axis=1), jnp.concatenate(_split(x, 3), axis=0), _NN)


def cumsum_row(x, mc):
    return _dot(jnp.concatenate(_split(x, 3), axis=1), jnp.concatenate([_bf(mc)] * 3, axis=1), _NT)


def head_sum(x, bd):
    return _dot(jnp.concatenate(_split(x, 2), axis=1), jnp.concatenate([_bf(bd)] * 2, axis=0), _NN)


def sigmoid(x):
    return 0.5 + 0.5 * jnp.tanh(0.5 * x)


def silu(x):
    return x * sigmoid(x)


def softplus(x):
    return jnp.maximum(x, 0.0) + jnp.log(1.0 + jnp.exp(-jnp.abs(x)))


def log_sigmoid(x):
    return -softplus(-x)


def head_norm(y, gain, bd, center):
    if center:
        y = y - head_sum(y, bd) * (1.0 / HEAD_DIM)
    ms = head_sum(y * y, bd) * (1.0 / HEAD_DIM)
    return y * lax.rsqrt(ms + LN_EPS) * gain


def l2norm_heads(x, bd):
    return x * lax.rsqrt(head_sum(x * x, bd) + 1e-6)


def layer_norm(x):
    mu = jnp.mean(x, axis=-1, keepdims=True)
    xc = x - mu
    var = jnp.mean(xc * xc, axis=-1, keepdims=True)
    return xc * lax.rsqrt(var + LN_EPS)


def colw(x, idx):
    m = x.shape[0]
    return jnp.concatenate([jnp.broadcast_to(x[:, j:j + 1], (m, HEAD_DIM)) for j in idx], axis=1)


def roww(x, idx):
    return jnp.concatenate([x[j:j + 1, :] for j in idx], axis=1)


def segmax(x):
    m = x.shape[0]
    return jnp.concatenate(
        [jnp.broadcast_to(jnp.max(x[:, h * HEAD_DIM:(h + 1) * HEAD_DIM], axis=1, keepdims=True), (m, HEAD_DIM))
         for h in range(N_HEADS)], axis=1)


def wide_masks(rev, dtype=jnp.float32):
    r = lax.broadcasted_iota(jnp.int32, (CHUNK, GROUP_W), 0)
    c = jnp.bitwise_and(lax.broadcasted_iota(jnp.int32, (CHUNK, GROUP_W), 1), HEAD_DIM - 1)
    if rev:
        incl_w, strict_w = r <= c, r < c
    else:
        incl_w, strict_w = r >= c, r > c
    mc = incl_w[:, :CHUNK].astype(dtype)
    return incl_w, strict_w, mc, (r == c).astype(dtype)


def block_diag_mask():
    r = jnp.right_shift(lax.broadcasted_iota(jnp.int32, (GROUP_W, GROUP_W), 0), 6)
    c = jnp.right_shift(lax.broadcasted_iota(jnp.int32, (GROUP_W, GROUP_W), 1), 6)
    m = (r == c).astype(jnp.float32)
    return _bf(m), m


def inv_unit_tri_w(a, eye_w, bdm):
    p = _bf(-a)
    e = _expand(p, bdm)
    t = eye_w - a
    for _ in range(5):
        p = _bf(_dot(p, e, _NN))
        e = _expand(p, bdm)
        t = t + _dot(_bf(t), e, _NN)
    r = eye_w - t - wmm(a, t, bdm, 3)
    return t + wmm(t, r, bdm, 1)


def mlstm_pre(q, k, v, bcol_w, brow_w, icol_w, irow_w, bl_w, incl_w, bdm):
    logd = jnp.where(incl_w, bcol_w - brow_w + irow_w, NEG_INF)
    mt0 = segmax(logd)
    s0 = wmm(q, k, bdm, 1, nt=True) * jnp.exp(logd - mt0)
    wk = bl_w - bcol_w + icol_w
    wkmax = jnp.max(wk, axis=0, keepdims=True)
    ks0 = k * jnp.exp(wk - wkmax)
    return (q, bcol_w, bl_w, mt0, wmm(s0, v, bdm, 1), head_sum(s0, bdm[1]), wkmax,
            bmm(ks0, v, _TN) * bdm[1], jnp.sum(ks0, axis=0, keepdims=True))


def mlstm_step(q, bcol_w, bl_w, mt0, num0, den0, wkmax, kv0, ksum0, c_bd, n_row, m_w, bdm):
    inter = bcol_w + m_w
    mt = jnp.maximum(inter, mt0)
    f = jnp.exp(mt0 - mt)
    ew = jnp.exp(inter - mt)
    num = num0 * f + bmm(q, c_bd) * ew
    den = den0 * f + head_sum(q * n_row, bdm[1]) * ew
    h = num / jnp.maximum(jnp.abs(den), jnp.exp(-mt))
    m_new = jnp.maximum(bl_w + m_w, wkmax)
    sp = jnp.exp(bl_w + m_w - m_new)
    sc = jnp.exp(wkmax - m_new)
    return h, c_bd * sp + kv0 * sc, n_row * sp + ksum0 * sc, m_new


def delta_pre(q, k, v, beta_w, gcol_w, grow_w, gl_w, incl_w, strict_w, eye_w, bdm):
    decay = jnp.exp(jnp.where(incl_w, gcol_w - grow_w, NEG_INF))
    a = jnp.where(strict_w, wmm(k, k, bdm, 1, nt=True) * decay, 0.0) * beta_w
    t = inv_unit_tri_w(a, eye_w, bdm)
    eg = jnp.exp(gcol_w)
    u = wmm(t, v * beta_w, bdm, 1)
    w = wmm(t, k * (beta_w * eg), bdm, 1)
    qk = wmm(q, k, bdm, 1, nt=True) * decay
    return u, jnp.concatenate([w, q * eg], axis=0), qk, k * jnp.exp(gl_w - gcol_w), jnp.exp(gl_w)


def delta_step(u, wq, qk, kd, egl, s_bd, bdm):
    ws = bmm(wq, s_bd)
    v_new = u - ws[:CHUNK]
    o = ws[CHUNK:] + wmm(qk, v_new, bdm, 1)
    s_new = s_bd * egl + bmm(kd, v_new, _TN) * bdm[1]
    return o, s_new


def ret_pre(q, k, v, dmat_w, zeta_w, bdm):
    sc = wmm(q, k, bdm, 1, nt=True) * dmat_w
    return q, wmm(sc, v, bdm, 1), bmm(k * zeta_w, v, _TN) * bdm[1]


def ret_step(q, o1, kv, xi_w, gl_w, s_bd):
    return o1 + bmm(q, s_bd) * xi_w, s_bd * gl_w + kv


def rwkv_pre(rp, khp, ktn, bn, v, incl_w, strict_w, eye_w, bdm):
    lhs = jnp.concatenate([khp, rp], axis=0)
    ab_kt = wmm(lhs, ktn, bdm, 1, nt=True)
    ab_b = wmm(lhs, bn, bdm, 1, nt=True)
    a_kt = jnp.where(strict_w, ab_kt[:CHUNK], 0.0)
    b_kt = jnp.where(incl_w, ab_kt[CHUNK:], 0.0)
    a_b = jnp.where(strict_w, ab_b[:CHUNK], 0.0)
    b_b = jnp.where(incl_w, ab_b[CHUNK:], 0.0)
    t = inv_unit_tri_w(a_b, eye_w, bdm)
    tk = wmm(t, khp, bdm, 1)
    tav = wmm(t, wmm(a_kt, v, bdm, 1), bdm, 1)
    return jnp.concatenate([tk, rp], axis=0), tav, wmm(b_kt, v, bdm, 1), b_b


def rwkv_step(tkr, tav, o1, b_b, v, kbe, etot, s_bd, bdm):
    ls = bmm(tkr, s_bd, _NT)
    u = ls[:CHUNK] + tav
    o = ls[CHUNK:] + o1 - wmm(b_b, u, bdm, 1)
    upd = bmm(jnp.concatenate([v, -u], axis=0), kbe, _TN)
    s_new = s_bd * etot + upd * bdm[1]
    return o, s_new


def _group_loop(nc, g, body):
    n_it = nc // g

    def run(first):
        def it(i, carry):
            body(i, first)
            return carry
        return it

    lax.fori_loop(0, n_it // 2, run(True), 0)
    lax.fori_loop(n_it // 2, n_it, run(False), 0)


def _scan_steps(nc):
    return max(g for g in (1, 2, 4, 8) if g <= MAX_SCAN_STEPS and nc % (2 * g) == 0)


def _group_chunks(i, g, nc, ns):
    out = []
    for s in range(ns):
        for z in range(2):
            for j in range(g):
                step = i * g + j
                c = s * nc + (step if z == 0 else nc - 1 - step)
                out.append((s, z, c, pl.multiple_of(c * CHUNK, CHUNK)))
    return out


def _group_masks(g, ns):
    shape = (2 * g * ns, CHUNK, GROUP_W)
    lane = jnp.right_shift(lax.broadcasted_iota(jnp.int32, shape, 0), g.bit_length() - 1)
    r = lax.broadcasted_iota(jnp.int32, shape, 1)
    c = jnp.bitwise_and(lax.broadcasted_iota(jnp.int32, shape, 2), HEAD_DIM - 1)
    d = jnp.where(jnp.bitwise_and(lane, 1) == 1, c - r, r - c)
    return d >= 0, d > 0, (d == 0).astype(jnp.float32)


def _stack(items):
    return [jnp.stack(x) for x in zip(*items)]


def _pick(arrs, g, j):
    return [jnp.stack([a[k] for k in range(j, a.shape[0], g)]) for a in arrs]


def _halo(ref, s, c, t0, nc):
    rows = ref.shape[0]
    ps = pl.multiple_of(jnp.maximum(t0 - 8, 0), 8)
    nx = pl.multiple_of(jnp.minimum(t0 + CHUNK, rows - 8), 8)
    prev = jnp.where(c > s * nc, ref[pl.ds(ps, 8), :], 0.0)
    nxt = jnp.where(c < s * nc + nc - 1, ref[pl.ds(nx, 8), :], 0.0)
    return jnp.concatenate([prev, ref[pl.ds(t0, CHUNK), :], nxt], axis=0)


def _shift_matrix(offsets):
    rows = len(offsets) * CHUNK
    r = lax.broadcasted_iota(jnp.int32, (rows, 2 * HALO), 0)
    c = lax.broadcasted_iota(jnp.int32, (rows, 2 * HALO), 1)
    c = jnp.where(c >= HALO, c - HALO, c)
    t = jnp.bitwise_and(r, CHUNK - 1)
    target = t + 8
    for j, off in enumerate(offsets):
        target = jnp.where(jnp.right_shift(r, 6) == j, target + off, target)
    return _bf((c == target).astype(jnp.float32))


def _hs(h):
    return slice(h * HEAD_DIM, (h + 1) * HEAD_DIM)


def _load_state(sbd, s0_ref, ns):
    for s in range(ns):
        for z in range(2):
            sbd[2 * s + z] = jnp.zeros((GROUP_W, GROUP_W), jnp.float32)
            for h in range(N_HEADS):
                sbd[2 * s + z, _hs(h), _hs(h)] = s0_ref[s, 0, z, h]


def _store_state(s1_ref, sbd, ns):
    for s in range(ns):
        for z in range(2):
            for h in range(N_HEADS):
                s1_ref[s, z, h] = sbd[2 * s + z, _hs(h), _hs(h)]


def _merge(first, hscr, t0, h):
    if first:
        hscr[pl.ds(t0, CHUNK), :] = h
        return None
    return hscr[pl.ds(t0, CHUNK), :] + h


def _scan(nc, g, ns, prep, pre_fn, n_masks, step_fn, state_refs, hscr, finish):
    bdm = block_diag_mask()
    masks = _group_masks(g, ns)[:n_masks]
    n_state = len(state_refs)

    def body(i, first):
        chunks = _group_chunks(i, g, nc, ns)
        preps = [prep(s, z, c, t0, first) for s, z, c, t0 in chunks]
        n_in = len(preps[0][0])
        pre = jax.vmap(pre_fn, in_axes=(0,) * (n_in + n_masks) + (None,))(
            *_stack([p[0] for p in preps]), *masks, bdm)
        extra = _stack([p[1] for p in preps]) if preps[0][1] else []
        states = [ref[...] for ref in state_refs]
        for j in range(g):
            args = _pick(list(pre), g, j) + _pick(extra, g, j) + states
            res = jax.vmap(step_fn, in_axes=(0,) * len(args) + (None,))(*args, bdm)
            states = list(res[1:])
            for lane in range(2 * ns):
                k = lane * g + j
                tot = _merge(first, hscr, chunks[k][3], res[0][lane])
                if not first:
                    finish(chunks[k][3], tot, preps[k][2])
        for ref, val in zip(state_refs, states):
            ref[...] = val

    _group_loop(nc, g, body)


def mlstm_kernel(p_ref, gc_ref, gr_ref, bc_ref, br_ref, norm_ref, c0_ref, n0_ref, m0_ref,
                 y_ref, c1_ref, n1_ref, m1_ref, hscr, cbd, nrow, mw, *, nc, g, ns):
    _load_state(cbd, c0_ref, ns)
    for s in range(ns):
        for z in range(2):
            for h in range(N_HEADS):
                nrow[2 * s + z, :, _hs(h)] = n0_ref[s, 0, z, h]
                mw[2 * s + z, :, _hs(h)] = jnp.broadcast_to(m0_ref[s, 0, z, h], (1, HEAD_DIM))
    bd = block_diag_mask()[1]
    mcs = [wide_masks(False)[2], wide_masks(True)[2]]

    def prep(s, z, c, t0, first):
        x = p_ref[pl.ds(t0, CHUNK), 0:3 * GROUP_W]
        g_c = gc_ref[pl.ds(t0, CHUNK), :] + bc_ref[...]
        g_r = gr_ref[c] + br_ref[...]
        lf_c = log_sigmoid(g_c)
        idx_i = [z * N_HEADS + h for h in range(N_HEADS)]
        idx_f = [8 + j for j in idx_i]
        return ((x[:, :GROUP_W], x[:, GROUP_W:2 * GROUP_W] * HEAD_DIM ** -0.5, x[:, 2 * GROUP_W:],
                 colw(cumsum_col(mcs[z], lf_c), idx_f), roww(cumsum_row(log_sigmoid(g_r), mcs[z]), idx_f),
                 colw(g_c, idx_i), roww(g_r, idx_i), colw(jnp.sum(lf_c, axis=0, keepdims=True), idx_f)), (), ())

    def finish(t0, tot, _):
        og = p_ref[pl.ds(t0, CHUNK), 3 * GROUP_W:]
        y_ref[pl.ds(t0, CHUNK), :] = sigmoid(og) * head_norm(tot, norm_ref[...], bd, True)

    _scan(nc, g, ns, prep, mlstm_pre, 1, mlstm_step, [cbd, nrow, mw], hscr, finish)
    _store_state(c1_ref, cbd, ns)
    for s in range(ns):
        for z in range(2):
            for h in range(N_HEADS):
                n1_ref[s, z, h] = nrow[2 * s + z, :, _hs(h)]
                m1_ref[s, z, h] = mw[2 * s + z, :, h * HEAD_DIM:h * HEAD_DIM + 1]


def delta_kernel(p_ref, gc_ref, gr_ref, conv_ref, prow_ref, pcol_ref, norm_ref, s0_ref,
                 y_ref, s1_ref, hscr, sbd, *, nc, g, ns):
    _load_state(sbd, s0_ref, ns)
    bd = block_diag_mask()[1]
    mcs = [wide_masks(False)[2], wide_masks(True)[2]]
    nega_row = -jnp.exp(prow_ref[0:1, :])
    dtb_row = prow_ref[1:2, :]
    nega_col = -jnp.exp(pcol_ref[:, 0:1])
    dtb_col = pcol_ref[:, 1:2]
    qkv_w = 3 * GROUP_W
    taps = _shift_matrix([j - CONV_K // 2 for j in range(CONV_K)])

    def prep(s, z, c, t0, first):
        ext = _halo(p_ref.at[:, 0:qkv_w], s, c, t0, nc)
        shifted = _dot(taps, jnp.concatenate(_split(ext, 2), axis=0), _NN)
        acc = shifted[0:CHUNK] * conv_ref[0:1, :]
        for j in range(1, CONV_K):
            acc = acc + shifted[j * CHUNK:(j + 1) * CHUNK] * conv_ref[j:j + 1, :]
        qkv = silu(acc)
        q = l2norm_heads(qkv[:, :GROUP_W], bd) * HEAD_DIM ** -0.5
        k = l2norm_heads(qkv[:, GROUP_W:2 * GROUP_W], bd)
        g_c = gc_ref[pl.ds(t0, CHUNK), :]
        gg_c = nega_row * softplus(g_c + dtb_row)
        gg_r = nega_col * softplus(gr_ref[c] + dtb_col)
        idx_b = [16 + z * N_HEADS + h for h in range(N_HEADS)]
        idx_a = [8 + j for j in idx_b]
        return ((q, k, qkv[:, 2 * GROUP_W:], colw(sigmoid(g_c), idx_b), colw(cumsum_col(mcs[z], gg_c), idx_a),
                 roww(cumsum_row(gg_r, mcs[z]), idx_a), colw(jnp.sum(gg_c, axis=0, keepdims=True), idx_a)), (), ())

    def finish(t0, tot, _):
        zg = p_ref[pl.ds(t0, CHUNK), qkv_w:]
        y_ref[pl.ds(t0, CHUNK), :] = head_norm(tot, norm_ref[...], bd, False) * silu(zg)

    _scan(nc, g, ns, prep, delta_pre, 3, delta_step, [sbd], hscr, finish)
    _store_state(s1_ref, sbd, ns)


def _rope(x, cos, sin, lane_lo):
    n = x.shape[1]
    swapped = jnp.where(lane_lo, pltpu.roll(x, n - 16, 1), pltpu.roll(x, 16, 1))
    return x * cos + swapped * sin


def ret_kernel(*refs, nc, g, ns, rope):
    if rope:
        p_ref, cos_ref, sin_ref, rd_ref, norm_ref, s0_ref, y_ref, s1_ref, hscr, sbd, cst = refs
    else:
        p_ref, rd_ref, norm_ref, s0_ref, y_ref, s1_ref, hscr, sbd, cst = refs
    f32 = jnp.float32
    _load_state(sbd, s0_ref, ns)
    bd = block_diag_mask()[1]
    r = lax.broadcasted_iota(jnp.int32, (CHUNK, GROUP_W), 0).astype(f32)
    cc = jnp.bitwise_and(lax.broadcasted_iota(jnp.int32, (CHUNK, GROUP_W), 1), HEAD_DIM - 1).astype(f32)
    lg_row = -jnp.exp(rd_ref[...])
    for z in range(2):
        incl_w = wide_masks(z == 1)[0]
        lg = colw(lg_row, [z * N_HEADS + h for h in range(N_HEADS)])
        if z == 0:
            cst[z, 0] = jnp.exp(jnp.where(incl_w, (r - cc) * lg, NEG_INF))
            cst[z, 1] = jnp.exp((r + 1.0) * lg)
            cst[z, 2] = jnp.exp((CHUNK - 1.0 - r) * lg)
        else:
            cst[z, 0] = jnp.exp(jnp.where(incl_w, (cc - r) * lg, NEG_INF))
            cst[z, 1] = jnp.exp((CHUNK - r) * lg)
            cst[z, 2] = jnp.exp(r * lg)
        cst[z, 3] = jnp.exp(CHUNK * lg + 0.0 * r)
    if rope:
        lane = lax.broadcasted_iota(jnp.int32, (CHUNK, GROUP_W), 1)
        lane_lo = jnp.bitwise_and(lane, 31) < 16

    def prep(s, z, c, t0, first):
        x = p_ref[pl.ds(t0, CHUNK), 0:3 * GROUP_W]
        q = x[:, :GROUP_W] * HEAD_DIM ** -0.5
        k = x[:, GROUP_W:2 * GROUP_W]
        if rope:
            tr = pl.multiple_of(t0 - s * nc * CHUNK, CHUNK)
            cos = cos_ref[pl.ds(tr, CHUNK), :]
            sin = sin_ref[pl.ds(tr, CHUNK), :]
            q = _rope(q, cos, sin, lane_lo)
            k = _rope(k, cos, sin, lane_lo)
        return (q, k, x[:, 2 * GROUP_W:], cst[z, 0], cst[z, 2]), (cst[z, 1], cst[z, 3, 0:1, :]), ()

    def finish(t0, tot, _):
        gate = p_ref[pl.ds(t0, CHUNK), 3 * GROUP_W:]
        y_ref[pl.ds(t0, CHUNK), :] = silu(gate) * head_norm(tot, norm_ref[...], bd, True)

    _scan(nc, g, ns, prep, ret_pre, 0, lambda *a: ret_step(*a[:-1]), [sbd], hscr, finish)
    _store_state(s1_ref, sbd, ns)


def rwkv_kernel(p_ref, mu_ref, w0_ref, w2_ref, a0_ref, a2_ref, g2_ref, vec_ref, s0_ref,
                y_ref, s1_ref, hscr, sbd, *, nc, g, ns):
    _load_state(sbd, s0_ref, ns)
    bd = block_diag_mask()[1]
    mcs = [wide_masks(False)[2], wide_masks(True)[2]]
    kk = vec_ref[0:1, :]
    ka = vec_ref[1:2, :]
    rk = vec_ref[2:3, :]
    gain = vec_ref[3:4, :]

    def lora_a(da, zz):
        return sigmoid(a0_ref[zz:zz + 1, :] + bmm(da[:, zz * RWKV_LORA:(zz + 1) * RWKV_LORA], a2_ref[zz]))

    def prep(s, z, c, t0, first):
        ext = _halo(p_ref, s, c, t0, nc)
        cur = ext[8:8 + CHUNK]
        pdm = cur + mu_ref[...] * (0.5 * (ext[7:7 + CHUNK] + ext[9:9 + CHUNK]) - cur)
        r = pdm[:, :GROUP_W]
        k = pdm[:, GROUP_W:2 * GROUP_W]
        v = pdm[:, 2 * GROUP_W:3 * GROUP_W]
        dw = pdm[:, 3 * GROUP_W:3 * GROUP_W + 128]
        da = pdm[:, 3 * GROUP_W + 128:3 * GROUP_W + 256]
        a_z = lora_a(da, z)
        w_pre = w0_ref[z:z + 1, :] + bmm(jnp.tanh(dw[:, z * RWKV_LORA:(z + 1) * RWKV_LORA]), w2_ref[z])
        lw = -jnp.exp(-softplus(-w_pre) - 0.5)
        kh = l2norm_heads(k * kk, bd)
        kt_z = k * (1.0 + (a_z - 1.0) * ka)
        b = a_z * kh
        cw = cumsum_col(mcs[z], lw)
        tot = jnp.sum(lw, axis=0, keepdims=True)
        en = jnp.exp(-cw)
        et = jnp.exp(tot - cw)
        fin = ()
        if not first:
            kt_o = k * (1.0 + (lora_a(da, 1 - z) - 1.0) * ka)
            bonus = head_sum(r * (kt_z + kt_o) * rk, bd) * v
            fin = (bmm(sigmoid(pdm[:, 3 * GROUP_W + 256:]), g2_ref[...]), bonus)
        return ((r * jnp.exp(cw), kh * jnp.exp(cw - lw), kt_z * en, b * en, v),
                (v, jnp.concatenate([kt_z * et, b * et], axis=0), jnp.exp(tot)), fin)

    def finish(t0, tot_h, fin):
        gate, bonus = fin
        y_ref[pl.ds(t0, CHUNK), :] = gate * (head_norm(tot_h, gain, bd, True) + bonus)

    _scan(nc, g, ns, prep, rwkv_pre, 3, rwkv_step, [sbd], hscr, finish)
    _store_state(s1_ref, sbd, ns)


def mod_kernel(c_ref, w_ref, b_ref, o_ref):
    o_ref[0] = bmm(silu(c_ref[...]), w_ref[0]) + b_ref[0]


def _rows(is_ctx, ctx_ref, lat_ref):
    return jnp.where(is_ctx, ctx_ref[...], lat_ref[...])


def proj_kernel(xp_ref, xs_ref, mod_ref, wa, wb, wc, wd, wg, wgt, pa, pb, pc, pd, pg, pgt, *, n_ptiles):
    m = mod_ref[0]
    x = _rows(pl.program_id(0) < n_ptiles, xp_ref, xs_ref)
    u = layer_norm(x) * (1.0 + m[1:2]) + m[0:1]
    ub = u.astype(BF)
    pa[...] = _dot(ub, wa[...], _NN)
    pb[...] = _dot(ub, wb[...], _NN)
    pc[...] = _dot(ub, wc[...], _NN)
    pd[...] = _dot(ub, wd[...], _NN)
    pg[...] = _dot(ub, wg[...], _NN)
    gt = _dot(wgt[...], ub, _NT)
    for j in range(ROW_TILE // CHUNK):
        pgt[j] = gt[:, j * CHUNK:(j + 1) * CHUNK]


def post_kernel(hp, hs, yap, yas, ybp, ybs, ycp, ycs, ydp, yds, mod_ref, wout, l1g, l1b, w1, w2, l2g, l2b,
                *o_refs, n_ptiles):
    m = mod_ref[0]
    is_ctx = pl.program_id(0) < n_ptiles
    y = jnp.concatenate([_rows(is_ctx, yap, yas), _rows(is_ctx, ybp, ybs), _rows(is_ctx, ycp, ycs),
                         _rows(is_ctx, ydp, yds)], axis=1).astype(BF)
    attn = _dot(y, wout[...], _NN)
    h1 = layer_norm(DEEPNORM_ALPHA * _rows(is_ctx, hp, hs) + m[2:3] * attn) * l1g[...] + l1b[...]
    u2 = layer_norm(h1) * (1.0 + m[4:5]) + m[3:4]
    f = _dot(u2.astype(BF), w1[...], _NN)
    f = jnp.square(jnp.maximum(f, 0.0))
    f2 = _dot(f.astype(BF), w2[...], _NN)
    out = layer_norm(DEEPNORM_ALPHA * h1 + m[5:6] * f2) * l2g[...] + l2b[...]
    if len(o_refs) == 1:
        o_refs[0][...] = out
    else:
        @pl.when(is_ctx)
        def _():
            o_refs[0][...] = out

        @pl.when(jnp.logical_not(is_ctx))
        def _():
            o_refs[1][...] = out


def _params(n_grid_axes):
    return pltpu.CompilerParams(dimension_semantics=("arbitrary",) * n_grid_axes,
                                vmem_limit_bytes=VMEM_LIMIT)


def _const_spec(a):
    nd = a.ndim
    return pl.BlockSpec(a.shape, lambda *_: (0,) * nd, pipeline_mode=pl.Buffered(1))


def _f32(shape):
    return jax.ShapeDtypeStruct(shape, jnp.float32)


def _mod_call(c_all, w_mod, b_mod):
    depth, d, n = w_mod.shape
    tn = 512
    return pl.pallas_call(
        mod_kernel, grid=(depth, n // tn),
        in_specs=[pl.BlockSpec(c_all.shape, lambda l, j: (0, 0)),
                  pl.BlockSpec((1, d, tn), lambda l, j: (l, 0, j)),
                  pl.BlockSpec((1, 1, tn), lambda l, j: (l, 0, j))],
        out_specs=pl.BlockSpec((1, c_all.shape[0], tn), lambda l, j: (l, 0, j)),
        out_shape=_f32((depth, c_all.shape[0], n)),
        compiler_params=_params(2), name="mod")(c_all, w_mod, b_mod.reshape(depth, 1, n))


def _mod_spec(n_ptiles, tiles_per_dec):
    def index(i):
        return (jnp.where(i < n_ptiles, 0, 1 + (i - n_ptiles) // tiles_per_dec), 0, 0)
    return pl.BlockSpec((1, 6, D_MODEL), index)


def _pair_specs(pair, n_ptiles):
    width = pair[0].shape[1]
    off = n_ptiles if pair[0] is pair[1] else 0
    return [pl.BlockSpec((ROW_TILE, width), lambda i: (jnp.minimum(i, n_ptiles - 1), 0)),
            pl.BlockSpec((ROW_TILE, width), lambda i: (jnp.maximum(i - n_ptiles, 0) + off, 0))]


def _proj_call(x_pair, n_rows, mod_l, ws, n_ptiles, tiles_per_dec):
    row = lambda i: (i, 0)
    widths = [w.shape[1] for w in ws[:5]]
    per_tile = ROW_TILE // CHUNK
    return pl.pallas_call(
        functools.partial(proj_kernel, n_ptiles=n_ptiles), grid=(n_rows // ROW_TILE,),
        in_specs=_pair_specs(x_pair, n_ptiles) + [_mod_spec(n_ptiles, tiles_per_dec)]
                 + [_const_spec(w) for w in ws],
        out_specs=[pl.BlockSpec((ROW_TILE, w), row) for w in widths]
                  + [pl.BlockSpec((per_tile, N_GATES, CHUNK), lambda i: (i, 0, 0))],
        out_shape=[_f32((n_rows, w)) for w in widths] + [_f32((n_rows // CHUNK, N_GATES, CHUNK))],
        compiler_params=_params(1), name="proj")(*x_pair, mod_l, *ws)


def _post_call(h_pair, y_pairs, n_rows, mod_l, ws, n_ptiles, tiles_per_dec, split_out):
    pairs = [h_pair] + y_pairs
    if split_out:
        n_prows = n_ptiles * ROW_TILE
        out_specs = _pair_specs((_f32((n_prows, D_MODEL)), _f32((n_rows - n_prows, D_MODEL))), n_ptiles)
        out_shape = [_f32((n_prows, D_MODEL)), _f32((n_rows - n_prows, D_MODEL))]
    else:
        out_specs = pl.BlockSpec((ROW_TILE, D_MODEL), lambda i: (i, 0))
        out_shape = _f32((n_rows, D_MODEL))
    return pl.pallas_call(
        functools.partial(post_kernel, n_ptiles=n_ptiles), grid=(n_rows // ROW_TILE,),
        in_specs=[spec for p in pairs for spec in _pair_specs(p, n_ptiles)]
                 + [_mod_spec(n_ptiles, tiles_per_dec)] + [_const_spec(w) for w in ws],
        out_specs=out_specs, out_shape=out_shape,
        compiler_params=_params(1), name="post")(*[a for p in pairs for a in p], mod_l, *ws)


MAT = (2, N_HEADS, HEAD_DIM, HEAD_DIM)
VEC = (2, N_HEADS, 1, HEAD_DIM)
SCL = (2, N_HEADS, 1, 1)


def _mixer_call(kernel, name, t_len, n_seq, row_off, seq_inputs, const_inputs, state_inputs, layer,
                state_tails, scratch):
    nc = t_len // CHUNK
    g = _scan_steps(nc)
    ns = max(1, min(n_seq, PRE_BATCH // (2 * g)))
    assert n_seq % ns == 0 and row_off % (ns * t_len) == 0
    blk_off = row_off // (ns * t_len)
    in_specs = []
    for a in seq_inputs:
        if a.ndim == 2:
            in_specs.append(pl.BlockSpec((ns * t_len, a.shape[1]), lambda i: (i + blk_off, 0)))
        else:
            in_specs.append(pl.BlockSpec((ns * nc,) + a.shape[1:], lambda i: (i + blk_off, 0, 0)))
    in_specs += [_const_spec(a) for a in const_inputs]
    for a in state_inputs:
        tail = a.shape[2:]
        in_specs.append(pl.BlockSpec((ns, 1) + tail, lambda i, nd=len(tail): (i, layer) + (0,) * nd))
    out_specs = [pl.BlockSpec((ns * t_len, GROUP_W), lambda i: (i, 0))]
    out_specs += [pl.BlockSpec((ns,) + t, lambda i, nd=len(t): (i,) + (0,) * nd) for t in state_tails]
    out_shape = [_f32((n_seq * t_len, GROUP_W))] + [_f32((n_seq,) + t) for t in state_tails]
    return pl.pallas_call(
        functools.partial(kernel, nc=nc, g=g, ns=ns), grid=(n_seq // ns,),
        in_specs=in_specs, out_specs=out_specs, out_shape=out_shape,
        scratch_shapes=[pltpu.VMEM((ns * t_len, GROUP_W), jnp.float32)] + scratch(ns),
        compiler_params=_params(1), name=name)(*seq_inputs, *const_inputs, *state_inputs)


def _bd_state(ns):
    return pltpu.VMEM((2 * ns, GROUP_W, GROUP_W), jnp.float32)


def _rope_tables(t_len):
    nf = HEAD_DIM // 4
    inv = ROPE_BASE ** (-jnp.arange(nf, dtype=jnp.float32) / nf)
    t = jnp.arange(t_len)
    ang_r = (t // GRID_W).astype(jnp.float32)[:, None] * inv[None, :]
    ang_c = (t % GRID_W).astype(jnp.float32)[:, None] * inv[None, :]
    cos = jnp.concatenate([jnp.cos(ang_r)] * 2 + [jnp.cos(ang_c)] * 2, axis=1)
    sin = jnp.concatenate([-jnp.sin(ang_r), jnp.sin(ang_r), -jnp.sin(ang_c), jnp.sin(ang_c)], axis=1)
    return jnp.tile(cos, (1, N_HEADS)), jnp.tile(sin, (1, N_HEADS))


def _lane_row(vals, lo, n_rows=1, row=0, width=GATE_LANES):
    out = jnp.zeros((n_rows, width), jnp.float32)
    return out.at[row, lo:lo + vals.size].set(vals.reshape(-1))


def kernel(x_prompt, x_sample, c, state_mlstm_C, state_mlstm_n, state_mlstm_m, state_delta, state_ret,
           state_rwkv, c_ctx, w_mod, b_mod, w_in, w_out, ln1_g, ln1_b, ln2_g, ln2_b, w_ff1, w_ff2,
           mlstm_i_bias, mlstm_f_bias, mlstm_norm, delta_conv, delta_a_log, delta_dt_bias, delta_norm,
           ret_decay, ret_norm, rwkv_mu, rwkv_w0, rwkv_w2, rwkv_a0, rwkv_a2, rwkv_g2, rwkv_kk, rwkv_ka,
           rwkv_rk, rwkv_norm):
    f32 = jnp.float32
    batch, seq, d = x_prompt.shape
    dec_batch, dec_seq, _ = x_sample.shape
    assert d == D_MODEL and (batch * seq) % ROW_TILE == 0 and dec_seq % ROW_TILE == 0 and seq % CHUNK == 0
    assert (seq // CHUNK) % 2 == 0 and (dec_seq // CHUNK) % 2 == 0
    n_prows = batch * seq
    n_ptiles = n_prows // ROW_TILE
    tiles_per_dec = dec_seq // ROW_TILE

    c_all = jnp.zeros((8, d), f32).at[0].set(c_ctx).at[1:1 + dec_batch].set(c)
    mod = _mod_call(c_all, w_mod, b_mod).reshape(DEPTH, 8, 6, d)
    n_rows = n_prows + dec_batch * dec_seq
    h_pair = (x_prompt.reshape(n_prows, d), x_sample.reshape(dec_batch * dec_seq, d))
    cos_t, sin_t = _rope_tables(dec_seq)

    zmat = jnp.zeros((batch, 1) + MAT, f32)
    zn = jnp.zeros((batch, 1) + VEC, f32)
    zm = jnp.zeros((batch, 1) + SCL, f32)
    n_cache = state_mlstm_n.reshape(state_mlstm_n.shape[:-1] + (1, HEAD_DIM))
    m_cache = state_mlstm_m.reshape(state_mlstm_m.shape + (1, 1))
    groups = ((seq, batch, 0), (dec_seq, dec_batch, n_prows))

    new_states = []
    for l in range(DEPTH):
        wl = w_in[l]
        ws = [wl[:, 0:1024], wl[:, 1040:2064], wl[:, 2080:3104], wl[:, 3104:4256],
              jnp.pad(jnp.concatenate([wl[:, 1024:1040], wl[:, 2064:2080]], axis=1), ((0, 0), (0, GATE_LANES - N_GATES)))]
        ws = [w.astype(BF) for w in ws]
        ws.append(ws[4][:, :N_GATES].T)
        pa, pb, pc, pd, pg, pgt = _proj_call(h_pair, n_rows, mod[l], ws, n_ptiles, tiles_per_dec)

        bias_c = _lane_row(mlstm_i_bias[l], 0) + _lane_row(mlstm_f_bias[l], 8)
        bias_r = bias_c[0, :N_GATES].reshape(N_GATES, 1)
        dl_row = _lane_row(delta_a_log[l], 24, 2, 0) + _lane_row(delta_dt_bias[l], 24, 2, 1)
        dl_col = dl_row[:, :N_GATES].T
        rd_row = _lane_row(ret_decay[l], 0)
        rw_vec = jnp.stack([rwkv_kk[l], rwkv_ka[l], rwkv_rk[l].reshape(-1), rwkv_norm[l]], axis=0)
        row = lambda a: a.reshape(1, -1)

        ys = [[], [], [], []]
        layer_states = None
        for gi, (t_len, n_seq, off) in enumerate(groups):
            ctx = gi == 0
            lay = 0 if ctx else l
            ya, c1, n1, m1 = _mixer_call(
                mlstm_kernel, "mlstm", t_len, n_seq, off, [pa, pg, pgt],
                [bias_c, bias_r, row(mlstm_norm[l])],
                [zmat, zn, zm] if ctx else [state_mlstm_C, n_cache, m_cache], lay, [MAT, VEC, SCL],
                lambda ns: [_bd_state(ns), pltpu.VMEM((2 * ns, 1, GROUP_W), f32), pltpu.VMEM((2 * ns, 1, GROUP_W), f32)])
            yb, sb1 = _mixer_call(
                delta_kernel, "delta", t_len, n_seq, off, [pb, pg, pgt],
                [delta_conv[l], dl_row, dl_col, row(delta_norm[l])],
                [zmat if ctx else state_delta], lay, [MAT], lambda ns: [_bd_state(ns)])
            ret_consts = [rd_row, row(ret_norm[l])]
            if not ctx:
                ret_consts = [cos_t, sin_t] + ret_consts
            yc, sc1 = _mixer_call(
                functools.partial(ret_kernel, rope=not ctx), "ret", t_len, n_seq, off, [pc],
                ret_consts, [zmat if ctx else state_ret], lay, [MAT],
                lambda ns: [_bd_state(ns), pltpu.VMEM((2, 4, CHUNK, GROUP_W), f32)])
            yd, sd1 = _mixer_call(
                rwkv_kernel, "rwkv", t_len, n_seq, off, [pd],
                [row(rwkv_mu[l]), rwkv_w0[l], rwkv_w2[l], rwkv_a0[l], rwkv_a2[l], rwkv_g2[l], rw_vec],
                [zmat if ctx else state_rwkv], lay, [MAT], lambda ns: [_bd_state(ns)])
            for lst, y in zip(ys, (ya, yb, yc, yd)):
                lst.append(y)
            if ctx:
                layer_states = (c1, n1.reshape(n1.shape[:3] + (HEAD_DIM,)), m1.reshape(m1.shape[:3]), sb1, sc1, sd1)
        new_states.append(layer_states)
        post_ws = [w_out[l].astype(BF), row(ln1_g[l]), row(ln1_b[l]), w_ff1[l].astype(BF), w_ff2[l].astype(BF),
                   row(ln2_g[l]), row(ln2_b[l])]
        last = l == DEPTH - 1
        h_new = _post_call(h_pair, [tuple(y) for y in ys], n_rows, mod[l], post_ws, n_ptiles, tiles_per_dec, last)
        h_pair = tuple(h_new) if last else (h_new, h_new)

    outs = [jnp.stack([st[i] for st in new_states], axis=1) for i in range(6)]
    return (h_pair[0].reshape(batch, seq, d), h_pair[1].reshape(dec_batch, dec_seq, d), *outs)
```

```python
import functools

import jax
import jax.numpy as jnp
from jax import lax
from jax.experimental import pallas as pl
from jax.experimental.pallas import tpu as pltpu

D_MODEL = 1024
DEPTH = 2
GRID_W = 64
HEAD_DIM = 64
N_HEADS = 4
GROUP_W = N_HEADS * HEAD_DIM
CHUNK = 64
CONV_K = 5
ROPE_BASE = 10000.0
LN_EPS = 1e-5
DEEPNORM_ALPHA = (2 * DEPTH) ** 0.25
RWKV_LORA = 64
GATE_LANES = 128
N_GATES = 32
ROW_TILE = 512
VMEM_LIMIT = 56 * 1024 * 1024
MAX_SCAN_STEPS = 4
PRE_BATCH = 8
HALO = CHUNK + 16

NEG_INF = float("-inf")
BF = jnp.bfloat16
_NN = (((1,), (0,)), ((), ()))
_NT = (((1,), (1,)), ((), ()))
_TN = (((0,), (0,)), ((), ()))


def _dot(a, b, dims):
    return lax.dot_general(a, b, dims, preferred_element_type=jnp.result_type(a.dtype, jnp.float32))


def _bf(x):
    return x.astype(BF)


def _split(x, n):
    parts = []
    for _ in range(n - 1):
        p = _bf(x)
        parts.append(p)
        x = x - p.astype(x.dtype)
    parts.append(_bf(x))
    return parts


def bmm(a, b, dims=_NN):
    return _dot(_bf(a), _bf(b), dims)


def _expand(x, bdm):
    return jnp.concatenate([x] * N_HEADS, axis=0) * bdm[0].astype(x.dtype)


def wmm(a, b, bdm, passes=1, nt=False):
    dims = _NT if nt else _NN
    if passes == 1:
        return _dot(_bf(a), _expand(_bf(b), bdm), dims)
    ah, al = _split(a, 2)
    bh, bl = _split(b, 2)
    ebh = _expand(bh, bdm)
    lhs = jnp.concatenate([ah, ah, al], axis=1)
    rhs = jnp.concatenate([ebh, _expand(bl, bdm), ebh], axis=1 if nt else 0)
    return _dot(lhs, rhs, dims)


def cumsum_col(mc, x):
    return _dot(jnp.concatenate([_bf(mc)] * 3, axis=1), jnp.concatenate(_split(x, 3), axis=0), _NN)


def cumsum_row(x, mc):
    return _dot(jnp.concatenate(_split(x, 3), axis=1), jnp.concatenate([_bf(mc)] * 3, axis=1), _NT)


def head_sum(x, bd):
    return _dot(jnp.concatenate(_split(x, 2), axis=1), jnp.concatenate([_bf(bd)] * 2, axis=0), _NN)


def sigmoid(x):
    return 0.5 + 0.5 * jnp.tanh(0.5 * x)


def silu(x):
    return x * sigmoid(x)


def softplus(x):
    return jnp.maximum(x, 0.0) + jnp.log(1.0 + jnp.exp(-jnp.abs(x)))


def log_sigmoid(x):
    return -softplus(-x)


def head_norm(y, gain, bd, center):
    if center:
        y = y - head_sum(y, bd) * (1.0 / HEAD_DIM)
    ms = head_sum(y * y, bd) * (1.0 / HEAD_DIM)
    return y * lax.rsqrt(ms + LN_EPS) * gain


def l2norm_heads(x, bd):
    return x * lax.rsqrt(head_sum(x * x, bd) + 1e-6)


def layer_norm(x):
    mu = jnp.mean(x, axis=-1, keepdims=True)
    xc = x - mu
    var = jnp.mean(xc * xc, axis=-1, keepdims=True)
    return xc * lax.rsqrt(var + LN_EPS)


def colw(x, idx):
    m = x.shape[0]
    return jnp.concatenate([jnp.broadcast_to(x[:, j:j + 1], (m, HEAD_DIM)) for j in idx], axis=1)


def roww(x, idx):
    return jnp.concatenate([x[j:j + 1, :] for j in idx], axis=1)


def segmax(x):
    m = x.shape[0]
    return jnp.concatenate(
        [jnp.broadcast_to(jnp.max(x[:, h * HEAD_DIM:(h + 1) * HEAD_DIM], axis=1, keepdims=True), (m, HEAD_DIM))
         for h in range(N_HEADS)], axis=1)


def wide_masks(rev, dtype=jnp.float32):
    r = lax.broadcasted_iota(jnp.int32, (CHUNK, GROUP_W), 0)
    c = jnp.bitwise_and(lax.broadcasted_iota(jnp.int32, (CHUNK, GROUP_W), 1), HEAD_DIM - 1)
    if rev:
        incl_w, strict_w = r <= c, r < c
    else:
        incl_w, strict_w = r >= c, r > c
    mc = incl_w[:, :CHUNK].astype(dtype)
    return incl_w, strict_w, mc, (r == c).astype(dtype)


def block_diag_mask():
    r = jnp.right_shift(lax.broadcasted_iota(jnp.int32, (GROUP_W, GROUP_W), 0), 6)
    c = jnp.right_shift(lax.broadcasted_iota(jnp.int32, (GROUP_W, GROUP_W), 1), 6)
    m = (r == c).astype(jnp.float32)
    return _bf(m), m


def inv_unit_tri_w(a, eye_w, bdm):
    p = _bf(-a)
    e = _expand(p, bdm)
    t = eye_w - a
    for _ in range(5):
        p = _bf(_dot(p, e, _NN))
        e = _expand(p, bdm)
        t = t + _dot(_bf(t), e, _NN)
    r = eye_w - t - wmm(a, t, bdm, 3)
    return t + wmm(t, r, bdm, 1)


def mlstm_pre(q, k, v, bcol_w, brow_w, icol_w, irow_w, bl_w, incl_w, bdm):
    logd = jnp.where(incl_w, bcol_w - brow_w + irow_w, NEG_INF)
    mt0 = segmax(logd)
    s0 = wmm(q, k, bdm, 1, nt=True) * jnp.exp(logd - mt0)
    wk = bl_w - bcol_w + icol_w
    wkmax = jnp.max(wk, axis=0, keepdims=True)
    ks0 = k * jnp.exp(wk - wkmax)
    return (q, bcol_w, bl_w, mt0, wmm(s0, v, bdm, 1), head_sum(s0, bdm[1]), wkmax,
            bmm(ks0, v, _TN) * bdm[1], jnp.sum(ks0, axis=0, keepdims=True))


def mlstm_step(q, bcol_w, bl_w, mt0, num0, den0, wkmax, kv0, ksum0, c_bd, n_row, m_w, bdm):
    inter = bcol_w + m_w
    mt = jnp.maximum(inter, mt0)
    f = jnp.exp(mt0 - mt)
    ew = jnp.exp(inter - mt)
    num = num0 * f + bmm(q, c_bd) * ew
    den = den0 * f + head_sum(q * n_row, bdm[1]) * ew
    h = num / jnp.maximum(jnp.abs(den), jnp.exp(-mt))
    m_new = jnp.maximum(bl_w + m_w, wkmax)
    sp = jnp.exp(bl_w + m_w - m_new)
    sc = jnp.exp(wkmax - m_new)
    return h, c_bd * sp + kv0 * sc, n_row * sp + ksum0 * sc, m_new


def delta_pre(q_raw, k_raw, v, beta_w, gcol_w, grow_w, gl_w, incl_w, strict_w, eye_w, bdm):
    q = l2norm_heads(q_raw, bdm[1]) * HEAD_DIM ** -0.5
    k = l2norm_heads(k_raw, bdm[1])
    decay = jnp.exp(jnp.where(incl_w, gcol_w - grow_w, NEG_INF))
    a = jnp.where(strict_w, wmm(k, k, bdm, 1, nt=True) * decay, 0.0) * beta_w
    t = inv_unit_tri_w(a, eye_w, bdm)
    eg = jnp.exp(gcol_w)
    u = wmm(t, v * beta_w, bdm, 1)
    w = wmm(t, k * (beta_w * eg), bdm, 1)
    qk = wmm(q, k, bdm, 1, nt=True) * decay
    return u, jnp.concatenate([w, q * eg], axis=0), qk, k * jnp.exp(gl_w - gcol_w), jnp.exp(gl_w)


def delta_step(u, wq, qk, kd, egl, s_bd, bdm):
    ws = bmm(wq, s_bd)
    v_new = u - ws[:CHUNK]
    o = ws[CHUNK:] + wmm(qk, v_new, bdm, 1)
    s_new = s_bd * egl + bmm(kd, v_new, _TN) * bdm[1]
    return o, s_new


def ret_pre(q, k, v, dmat_w, zeta_w, bdm):
    sc = wmm(q, k, bdm, 1, nt=True) * dmat_w
    return q, wmm(sc, v, bdm, 1), bmm(k * zeta_w, v, _TN) * bdm[1]


def ret_step(q, o1, kv, xi_w, gl_w, s_bd):
    return o1 + bmm(q, s_bd) * xi_w, s_bd * gl_w + kv


def rwkv_pre(r, kkk, kt, a, v, cw, lw, tot, incl_w, strict_w, eye_w, bdm):
    kh = l2norm_heads(kkk, bdm[1])
    b = a * kh
    en = jnp.exp(-cw)
    et = jnp.exp(tot - cw)
    rp = r * jnp.exp(cw)
    khp = kh * jnp.exp(cw - lw)
    ktn = kt * en
    bn = b * en
    lhs = jnp.concatenate([khp, rp], axis=0)
    ab_kt = wmm(lhs, ktn, bdm, 1, nt=True)
    ab_b = wmm(lhs, bn, bdm, 1, nt=True)
    a_kt = jnp.where(strict_w, ab_kt[:CHUNK], 0.0)
    b_kt = jnp.where(incl_w, ab_kt[CHUNK:], 0.0)
    a_b = jnp.where(strict_w, ab_b[:CHUNK], 0.0)
    b_b = jnp.where(incl_w, ab_b[CHUNK:], 0.0)
    t = inv_unit_tri_w(a_b, eye_w, bdm)
    tk = wmm(t, khp, bdm, 1)
    tav = wmm(t, wmm(a_kt, v, bdm, 1), bdm, 1)
    return (jnp.concatenate([tk, rp], axis=0), tav, wmm(b_kt, v, bdm, 1), b_b,
            jnp.concatenate([kt * et, b * et], axis=0))


def rwkv_step(tkr, tav, o1, b_b, kbe, v, etot, s_bd, bdm):
    ls = bmm(tkr, s_bd, _NT)
    u = ls[:CHUNK] + tav
    o = ls[CHUNK:] + o1 - wmm(b_b, u, bdm, 1)
    upd = bmm(jnp.concatenate([v, -u], axis=0), kbe, _TN)
    s_new = s_bd * etot + upd * bdm[1]
    return o, s_new


def _group_loop(nc, g, body):
    n_it = nc // g

    def run(first):
        def it(i, carry):
            body(i, first)
            return carry
        return it

    lax.fori_loop(0, n_it // 2, run(True), 0)
    lax.fori_loop(n_it // 2, n_it, run(False), 0)


def _scan_steps(nc):
    return max(g for g in (1, 2, 4, 8) if g <= MAX_SCAN_STEPS and nc % (2 * g) == 0)


def _group_chunks(i, g, nc, ns):
    out = []
    for s in range(ns):
        for z in range(2):
            for j in range(g):
                step = i * g + j
                c = s * nc + (step if z == 0 else nc - 1 - step)
                out.append((s, z, c, pl.multiple_of(c * CHUNK, CHUNK)))
    return out


def _group_masks(g, ns):
    shape = (2 * g * ns, CHUNK, GROUP_W)
    lane = jnp.right_shift(lax.broadcasted_iota(jnp.int32, shape, 0), g.bit_length() - 1)
    r = lax.broadcasted_iota(jnp.int32, shape, 1)
    c = jnp.bitwise_and(lax.broadcasted_iota(jnp.int32, shape, 2), HEAD_DIM - 1)
    d = jnp.where(jnp.bitwise_and(lane, 1) == 1, c - r, r - c)
    return d >= 0, d > 0, (d == 0).astype(jnp.float32)


def _stack(items):
    return [jnp.stack(x) for x in zip(*items)]


def _pick(arrs, g, j):
    return [jnp.stack([a[k] for k in range(j, a.shape[0], g)]) for a in arrs]


def _halo(ref, s, c, t0, nc):
    rows = ref.shape[0]
    ps = pl.multiple_of(jnp.maximum(t0 - 8, 0), 8)
    nx = pl.multiple_of(jnp.minimum(t0 + CHUNK, rows - 8), 8)
    prev = jnp.where(c > s * nc, ref[pl.ds(ps, 8), :], 0.0)
    nxt = jnp.where(c < s * nc + nc - 1, ref[pl.ds(nx, 8), :], 0.0)
    return jnp.concatenate([prev, ref[pl.ds(t0, CHUNK), :], nxt], axis=0)


def _shift_matrix(offsets):
    rows = len(offsets) * CHUNK
    r = lax.broadcasted_iota(jnp.int32, (rows, 2 * HALO), 0)
    c = lax.broadcasted_iota(jnp.int32, (rows, 2 * HALO), 1)
    c = jnp.where(c >= HALO, c - HALO, c)
    t = jnp.bitwise_and(r, CHUNK - 1)
    target = t + 8
    for j, off in enumerate(offsets):
        target = jnp.where(jnp.right_shift(r, 6) == j, target + off, target)
    return _bf((c == target).astype(jnp.float32))


def _hs(h):
    return slice(h * HEAD_DIM, (h + 1) * HEAD_DIM)


def _load_state(sbd, s0_ref, ns):
    for s in range(ns):
        for z in range(2):
            sbd[2 * s + z] = jnp.zeros((GROUP_W, GROUP_W), jnp.float32)
            for h in range(N_HEADS):
                sbd[2 * s + z, _hs(h), _hs(h)] = s0_ref[s, 0, z, h]


def _store_state(s1_ref, sbd, ns):
    for s in range(ns):
        for z in range(2):
            for h in range(N_HEADS):
                s1_ref[s, z, h] = sbd[2 * s + z, _hs(h), _hs(h)]


def _merge(first, hscr, t0, h):
    if first:
        hscr[pl.ds(t0, CHUNK), :] = h
        return None
    return hscr[pl.ds(t0, CHUNK), :] + h


def _scan(nc, g, ns, prep, pre_fn, n_masks, step_fn, state_refs, hscr, finish, y_ref):
    bdm = block_diag_mask()
    masks = _group_masks(g, ns)[:n_masks]

    def body(i, first):
        chunks = _group_chunks(i, g, nc, ns)
        preps = [prep(s, z, c, t0, first) for s, z, c, t0 in chunks]
        n_in = len(preps[0][0])
        pre = jax.vmap(pre_fn, in_axes=(0,) * (n_in + n_masks) + (None,))(
            *_stack([p[0] for p in preps]), *masks, bdm)
        extra = _stack([p[1] for p in preps]) if preps[0][1] else []
        states = [ref[...] for ref in state_refs]
        for j in range(g):
            args = _pick(list(pre), g, j) + _pick(extra, g, j) + states
            res = jax.vmap(step_fn, in_axes=(0,) * len(args) + (None,))(*args, bdm)
            states = list(res[1:])
            ks = [lane * g + j for lane in range(2 * ns)]
            tots = [_merge(first, hscr, chunks[k][3], res[0][lane]) for lane, k in enumerate(ks)]
            if not first:
                ys = jax.vmap(finish)(jnp.stack(tots), *_stack([preps[k][2] for k in ks]))
                for lane, k in enumerate(ks):
                    y_ref[pl.ds(chunks[k][3], CHUNK), :] = ys[lane]
        for ref, val in zip(state_refs, states):
            ref[...] = val

    _group_loop(nc, g, body)


def mlstm_kernel(p_ref, gc_ref, gr_ref, bc_ref, br_ref, norm_ref, c0_ref, n0_ref, m0_ref,
                 y_ref, c1_ref, n1_ref, m1_ref, hscr, cbd, nrow, mw, *, nc, g, ns):
    _load_state(cbd, c0_ref, ns)
    for s in range(ns):
        for z in range(2):
            for h in range(N_HEADS):
                nrow[2 * s + z, :, _hs(h)] = n0_ref[s, 0, z, h]
                mw[2 * s + z, :, _hs(h)] = jnp.broadcast_to(m0_ref[s, 0, z, h], (1, HEAD_DIM))
    bd = block_diag_mask()[1]
    mcs = [wide_masks(False)[2], wide_masks(True)[2]]
    gain = norm_ref[...]

    def prep(s, z, c, t0, first):
        x = p_ref[pl.ds(t0, CHUNK), 0:3 * GROUP_W]
        g_c = gc_ref[pl.ds(t0, CHUNK), :] + bc_ref[...]
        g_r = gr_ref[c] + br_ref[...]
        lf_c = log_sigmoid(g_c)
        idx_i = [z * N_HEADS + h for h in range(N_HEADS)]
        idx_f = [8 + j for j in idx_i]
        fin = () if first else (p_ref[pl.ds(t0, CHUNK), 3 * GROUP_W:],)
        return ((x[:, :GROUP_W], x[:, GROUP_W:2 * GROUP_W] * HEAD_DIM ** -0.5, x[:, 2 * GROUP_W:],
                 colw(cumsum_col(mcs[z], lf_c), idx_f), roww(cumsum_row(log_sigmoid(g_r), mcs[z]), idx_f),
                 colw(g_c, idx_i), roww(g_r, idx_i), colw(jnp.sum(lf_c, axis=0, keepdims=True), idx_f)), (), fin)

    def finish(tot, og):
        return _bf(sigmoid(og) * head_norm(tot, gain, bd, True))

    _scan(nc, g, ns, prep, mlstm_pre, 1, mlstm_step, [cbd, nrow, mw], hscr, finish, y_ref)
    _store_state(c1_ref, cbd, ns)
    for s in range(ns):
        for z in range(2):
            for h in range(N_HEADS):
                n1_ref[s, z, h] = nrow[2 * s + z, :, _hs(h)]
                m1_ref[s, z, h] = mw[2 * s + z, :, h * HEAD_DIM:h * HEAD_DIM + 1]


def delta_kernel(p_ref, gc_ref, gr_ref, conv_ref, prow_ref, pcol_ref, norm_ref, s0_ref,
                 y_ref, s1_ref, hscr, sbd, *, nc, g, ns):
    _load_state(sbd, s0_ref, ns)
    bd = block_diag_mask()[1]
    mcs = [wide_masks(False)[2], wide_masks(True)[2]]
    nega_row = -jnp.exp(prow_ref[0:1, :])
    dtb_row = prow_ref[1:2, :]
    nega_col = -jnp.exp(pcol_ref[:, 0:1])
    dtb_col = pcol_ref[:, 1:2]
    qkv_w = 3 * GROUP_W
    gain = norm_ref[...]
    taps = _shift_matrix([j - CONV_K // 2 for j in range(CONV_K)])

    def prep(s, z, c, t0, first):
        ext = _halo(p_ref.at[:, 0:qkv_w], s, c, t0, nc)
        shifted = _dot(taps, jnp.concatenate(_split(ext, 2), axis=0), _NN)
        acc = shifted[0:CHUNK] * conv_ref[0:1, :]
        for j in range(1, CONV_K):
            acc = acc + shifted[j * CHUNK:(j + 1) * CHUNK] * conv_ref[j:j + 1, :]
        qkv = silu(acc)
        g_c = gc_ref[pl.ds(t0, CHUNK), :]
        gg_c = nega_row * softplus(g_c + dtb_row)
        gg_r = nega_col * softplus(gr_ref[c] + dtb_col)
        idx_b = [16 + z * N_HEADS + h for h in range(N_HEADS)]
        idx_a = [8 + j for j in idx_b]
        fin = () if first else (p_ref[pl.ds(t0, CHUNK), qkv_w:],)
        return ((qkv[:, :GROUP_W], qkv[:, GROUP_W:2 * GROUP_W], qkv[:, 2 * GROUP_W:], colw(sigmoid(g_c), idx_b),
                 colw(cumsum_col(mcs[z], gg_c), idx_a), roww(cumsum_row(gg_r, mcs[z]), idx_a),
                 colw(jnp.sum(gg_c, axis=0, keepdims=True), idx_a)), (), fin)

    def finish(tot, zg):
        return _bf(head_norm(tot, gain, bd, False) * silu(zg))

    _scan(nc, g, ns, prep, delta_pre, 3, delta_step, [sbd], hscr, finish, y_ref)
    _store_state(s1_ref, sbd, ns)


def _rope(x, cos, sin, lane_lo):
    n = x.shape[1]
    swapped = jnp.where(lane_lo, pltpu.roll(x, n - 16, 1), pltpu.roll(x, 16, 1))
    return x * cos + swapped * sin


def ret_kernel(*refs, nc, g, ns, rope):
    if rope:
        p_ref, cos_ref, sin_ref, rd_ref, norm_ref, s0_ref, y_ref, s1_ref, hscr, sbd, cst = refs
    else:
        p_ref, rd_ref, norm_ref, s0_ref, y_ref, s1_ref, hscr, sbd, cst = refs
    f32 = jnp.float32
    _load_state(sbd, s0_ref, ns)
    bd = block_diag_mask()[1]
    r = lax.broadcasted_iota(jnp.int32, (CHUNK, GROUP_W), 0).astype(f32)
    cc = jnp.bitwise_and(lax.broadcasted_iota(jnp.int32, (CHUNK, GROUP_W), 1), HEAD_DIM - 1).astype(f32)
    lg_row = -jnp.exp(rd_ref[...])
    gain = norm_ref[...]
    for z in range(2):
        incl_w = wide_masks(z == 1)[0]
        lg = colw(lg_row, [z * N_HEADS + h for h in range(N_HEADS)])
        if z == 0:
            cst[z, 0] = jnp.exp(jnp.where(incl_w, (r - cc) * lg, NEG_INF))
            cst[z, 1] = jnp.exp((r + 1.0) * lg)
            cst[z, 2] = jnp.exp((CHUNK - 1.0 - r) * lg)
        else:
            cst[z, 0] = jnp.exp(jnp.where(incl_w, (cc - r) * lg, NEG_INF))
            cst[z, 1] = jnp.exp((CHUNK - r) * lg)
            cst[z, 2] = jnp.exp(r * lg)
        cst[z, 3] = jnp.exp(CHUNK * lg + 0.0 * r)
    if rope:
        lane = lax.broadcasted_iota(jnp.int32, (CHUNK, GROUP_W), 1)
        lane_lo = jnp.bitwise_and(lane, 31) < 16

    def prep(s, z, c, t0, first):
        x = p_ref[pl.ds(t0, CHUNK), 0:3 * GROUP_W]
        q = x[:, :GROUP_W] * HEAD_DIM ** -0.5
        k = x[:, GROUP_W:2 * GROUP_W]
        if rope:
            tr = pl.multiple_of(t0 - s * nc * CHUNK, CHUNK)
            cos = cos_ref[pl.ds(tr, CHUNK), :]
            sin = sin_ref[pl.ds(tr, CHUNK), :]
            q = _rope(q, cos, sin, lane_lo)
            k = _rope(k, cos, sin, lane_lo)
        fin = () if first else (p_ref[pl.ds(t0, CHUNK), 3 * GROUP_W:],)
        return (q, k, x[:, 2 * GROUP_W:], cst[z, 0], cst[z, 2]), (cst[z, 1], cst[z, 3, 0:1, :]), fin

    def finish(tot, gate):
        return _bf(silu(gate) * head_norm(tot, gain, bd, True))

    _scan(nc, g, ns, prep, ret_pre, 0, lambda *a: ret_step(*a[:-1]), [sbd], hscr, finish, y_ref)
    _store_state(s1_ref, sbd, ns)


def rwkv_kernel(p_ref, mu_ref, w0_ref, w2_ref, a0_ref, a2_ref, g2_ref, vec_ref, s0_ref,
                y_ref, s1_ref, hscr, sbd, *, nc, g, ns):
    _load_state(sbd, s0_ref, ns)
    bd = block_diag_mask()[1]
    mcs = [wide_masks(False)[2], wide_masks(True)[2]]
    kk = vec_ref[0:1, :]
    ka = vec_ref[1:2, :]
    rk = vec_ref[2:3, :]
    gain = vec_ref[3:4, :]
    g2 = g2_ref[...]

    def lora_a(da, zz):
        return sigmoid(a0_ref[zz:zz + 1, :] + bmm(da[:, zz * RWKV_LORA:(zz + 1) * RWKV_LORA], a2_ref[zz]))

    def prep(s, z, c, t0, first):
        ext = _halo(p_ref, s, c, t0, nc)
        cur = ext[8:8 + CHUNK]
        nb = pltpu.roll(ext, 1, 0) + pltpu.roll(ext, HALO - 1, 0)
        pdm = cur + mu_ref[...] * (0.5 * nb[8:8 + CHUNK] - cur)
        r = pdm[:, :GROUP_W]
        k = pdm[:, GROUP_W:2 * GROUP_W]
        v = pdm[:, 2 * GROUP_W:3 * GROUP_W]
        dw = pdm[:, 3 * GROUP_W:3 * GROUP_W + 128]
        da = pdm[:, 3 * GROUP_W + 128:3 * GROUP_W + 256]
        a_z = lora_a(da, z)
        w_pre = w0_ref[z:z + 1, :] + bmm(jnp.tanh(dw[:, z * RWKV_LORA:(z + 1) * RWKV_LORA]), w2_ref[z])
        lw = -jnp.exp(-softplus(-w_pre) - 0.5)
        kt_z = k * (1.0 + (a_z - 1.0) * ka)
        tot = jnp.sum(lw, axis=0, keepdims=True)
        fin = ()
        if not first:
            kt_o = k * (1.0 + (lora_a(da, 1 - z) - 1.0) * ka)
            fin = (pdm[:, 3 * GROUP_W + 256:], r * (kt_z + kt_o) * rk, v)
        return (r, k * kk, kt_z, a_z, v, cumsum_col(mcs[z], lw), lw, tot), (v, jnp.exp(tot)), fin

    def finish(tot_h, dg, rkk, v):
        gate = bmm(sigmoid(dg), g2)
        return _bf(gate * (head_norm(tot_h, gain, bd, True) + head_sum(rkk, bd) * v))

    _scan(nc, g, ns, prep, rwkv_pre, 3, rwkv_step, [sbd], hscr, finish, y_ref)
    _store_state(s1_ref, sbd, ns)


def mod_kernel(c_ref, w_ref, b_ref, o_ref):
    o_ref[0] = bmm(silu(c_ref[...]), w_ref[0]) + b_ref[0]


def _rows(is_ctx, ctx_ref, lat_ref):
    return jnp.where(is_ctx, ctx_ref[...], lat_ref[...])


def proj_kernel(xp_ref, xs_ref, mod_ref, wa, wb, wc, wd, wg, wgt, pa, pb, pc, pd, pg, pgt, *, n_ptiles):
    m = mod_ref[0]
    x = _rows(pl.program_id(0) < n_ptiles, xp_ref, xs_ref)
    u = layer_norm(x) * (1.0 + m[1:2]) + m[0:1]
    ub = u.astype(BF)
    pa[...] = _dot(ub, wa[...], _NN)
    pb[...] = _dot(ub, wb[...], _NN)
    pc[...] = _dot(ub, wc[...], _NN)
    pd[...] = _dot(ub, wd[...], _NN)
    pg[...] = _dot(ub, wg[...], _NN)
    gt = _dot(wgt[...], ub, _NT)
    for j in range(ROW_TILE // CHUNK):
        pgt[j] = gt[:, j * CHUNK:(j + 1) * CHUNK]


def post_kernel(hp, hs, yap, yas, ybp, ybs, ycp, ycs, ydp, yds, mod_ref, wout, l1g, l1b, w1, w2, l2g, l2b,
                *o_refs, n_ptiles):
    m = mod_ref[0]
    is_ctx = pl.program_id(0) < n_ptiles
    y = jnp.concatenate([_rows(is_ctx, yap, yas), _rows(is_ctx, ybp, ybs), _rows(is_ctx, ycp, ycs),
                         _rows(is_ctx, ydp, yds)], axis=1)
    attn = _dot(y, wout[...], _NN)
    h1 = layer_norm(DEEPNORM_ALPHA * _rows(is_ctx, hp, hs) + m[2:3] * attn) * l1g[...] + l1b[...]
    u2 = layer_norm(h1) * (1.0 + m[4:5]) + m[3:4]
    f = _dot(u2.astype(BF), w1[...], _NN)
    f = jnp.square(jnp.maximum(f, 0.0))
    f2 = _dot(f.astype(BF), w2[...], _NN)
    out = layer_norm(DEEPNORM_ALPHA * h1 + m[5:6] * f2) * l2g[...] + l2b[...]
    if len(o_refs) == 1:
        o_refs[0][...] = out
    else:
        @pl.when(is_ctx)
        def _():
            o_refs[0][...] = out

        @pl.when(jnp.logical_not(is_ctx))
        def _():
            o_refs[1][...] = out


def _params(n_grid_axes):
    return pltpu.CompilerParams(dimension_semantics=("arbitrary",) * n_grid_axes,
                                vmem_limit_bytes=VMEM_LIMIT)


def _const_spec(a):
    nd = a.ndim
    return pl.BlockSpec(a.shape, lambda *_: (0,) * nd, pipeline_mode=pl.Buffered(1))


def _f32(shape):
    return jax.ShapeDtypeStruct(shape, jnp.float32)


def _mod_call(c_all, w_mod, b_mod):
    depth, d, n = w_mod.shape
    tn = 2048
    return pl.pallas_call(
        mod_kernel, grid=(depth, n // tn),
        in_specs=[pl.BlockSpec(c_all.shape, lambda l, j: (0, 0)),
                  pl.BlockSpec((1, d, tn), lambda l, j: (l, 0, j)),
                  pl.BlockSpec((1, 1, tn), lambda l, j: (l, 0, j))],
        out_specs=pl.BlockSpec((1, c_all.shape[0], tn), lambda l, j: (l, 0, j)),
        out_shape=_f32((depth, c_all.shape[0], n)),
        compiler_params=_params(2), name="mod")(c_all, w_mod, b_mod.reshape(depth, 1, n))


def _mod_spec(n_ptiles, tiles_per_dec):
    def index(i):
        return (jnp.where(i < n_ptiles, 0, 1 + (i - n_ptiles) // tiles_per_dec), 0, 0)
    return pl.BlockSpec((1, 6, D_MODEL), index)


def _pair_specs(pair, n_ptiles):
    width = pair[0].shape[1]
    off = n_ptiles if pair[0] is pair[1] else 0
    return [pl.BlockSpec((ROW_TILE, width), lambda i: (jnp.minimum(i, n_ptiles - 1), 0)),
            pl.BlockSpec((ROW_TILE, width), lambda i: (jnp.maximum(i - n_ptiles, 0) + off, 0))]


def _proj_call(x_pair, n_rows, mod_l, ws, n_ptiles, tiles_per_dec):
    row = lambda i: (i, 0)
    widths = [w.shape[1] for w in ws[:5]]
    per_tile = ROW_TILE // CHUNK
    return pl.pallas_call(
        functools.partial(proj_kernel, n_ptiles=n_ptiles), grid=(n_rows // ROW_TILE,),
        in_specs=_pair_specs(x_pair, n_ptiles) + [_mod_spec(n_ptiles, tiles_per_dec)]
                 + [_const_spec(w) for w in ws],
        out_specs=[pl.BlockSpec((ROW_TILE, w), row) for w in widths]
                  + [pl.BlockSpec((per_tile, N_GATES, CHUNK), lambda i: (i, 0, 0))],
        out_shape=[_f32((n_rows, w)) for w in widths] + [_f32((n_rows // CHUNK, N_GATES, CHUNK))],
        compiler_params=_params(1), name="proj")(*x_pair, mod_l, *ws)


def _post_call(h_pair, y_pairs, n_rows, mod_l, ws, n_ptiles, tiles_per_dec, split_out):
    pairs = [h_pair] + y_pairs
    if split_out:
        n_prows = n_ptiles * ROW_TILE
        out_specs = _pair_specs((_f32((n_prows, D_MODEL)), _f32((n_rows - n_prows, D_MODEL))), n_ptiles)
        out_shape = [_f32((n_prows, D_MODEL)), _f32((n_rows - n_prows, D_MODEL))]
    else:
        out_specs = pl.BlockSpec((ROW_TILE, D_MODEL), lambda i: (i, 0))
        out_shape = _f32((n_rows, D_MODEL))
    return pl.pallas_call(
        functools.partial(post_kernel, n_ptiles=n_ptiles), grid=(n_rows // ROW_TILE,),
        in_specs=[spec for p in pairs for spec in _pair_specs(p, n_ptiles)]
                 + [_mod_spec(n_ptiles, tiles_per_dec)] + [_const_spec(w) for w in ws],
        out_specs=out_specs, out_shape=out_shape,
        compiler_params=_params(1), name="post")(*[a for p in pairs for a in p], mod_l, *ws)


MAT = (2, N_HEADS, HEAD_DIM, HEAD_DIM)
VEC = (2, N_HEADS, 1, HEAD_DIM)
SCL = (2, N_HEADS, 1, 1)


def _mixer_call(kernel, name, t_len, n_seq, row_off, seq_inputs, const_inputs, state_inputs, layer,
                state_tails, scratch):
    nc = t_len // CHUNK
    g = _scan_steps(nc)
    ns = max(1, min(n_seq, PRE_BATCH // (2 * g)))
    assert n_seq % ns == 0 and row_off % (ns * t_len) == 0
    blk_off = row_off // (ns * t_len)
    in_specs = []
    for a in seq_inputs:
        if a.ndim == 2:
            in_specs.append(pl.BlockSpec((ns * t_len, a.shape[1]), lambda i: (i + blk_off, 0)))
        else:
            in_specs.append(pl.BlockSpec((ns * nc,) + a.shape[1:], lambda i: (i + blk_off, 0, 0)))
    in_specs += [_const_spec(a) for a in const_inputs]
    for a in state_inputs:
        tail = a.shape[2:]
        in_specs.append(pl.BlockSpec((ns, 1) + tail, lambda i, nd=len(tail): (i, layer) + (0,) * nd))
    out_specs = [pl.BlockSpec((ns * t_len, GROUP_W), lambda i: (i, 0))]
    out_specs += [pl.BlockSpec((ns,) + t, lambda i, nd=len(t): (i,) + (0,) * nd) for t in state_tails]
    out_shape = [jax.ShapeDtypeStruct((n_seq * t_len, GROUP_W), BF)] + [_f32((n_seq,) + t) for t in state_tails]
    return pl.pallas_call(
        functools.partial(kernel, nc=nc, g=g, ns=ns), grid=(n_seq // ns,),
        in_specs=in_specs, out_specs=out_specs, out_shape=out_shape,
        scratch_shapes=[pltpu.VMEM((ns * t_len, GROUP_W), jnp.float32)] + scratch(ns),
        compiler_params=_params(1), name=name)(*seq_inputs, *const_inputs, *state_inputs)


def _bd_state(ns):
    return pltpu.VMEM((2 * ns, GROUP_W, GROUP_W), jnp.float32)


def _rope_tables(t_len):
    nf = HEAD_DIM // 4
    inv = ROPE_BASE ** (-jnp.arange(nf, dtype=jnp.float32) / nf)
    t = jnp.arange(t_len)
    ang_r = (t // GRID_W).astype(jnp.float32)[:, None] * inv[None, :]
    ang_c = (t % GRID_W).astype(jnp.float32)[:, None] * inv[None, :]
    cos = jnp.concatenate([jnp.cos(ang_r)] * 2 + [jnp.cos(ang_c)] * 2, axis=1)
    sin = jnp.concatenate([-jnp.sin(ang_r), jnp.sin(ang_r), -jnp.sin(ang_c), jnp.sin(ang_c)], axis=1)
    return jnp.tile(cos, (1, N_HEADS)), jnp.tile(sin, (1, N_HEADS))


def _lane_row(vals, lo, n_rows=1, row=0, width=GATE_LANES):
    out = jnp.zeros((n_rows, width), jnp.float32)
    return out.at[row, lo:lo + vals.size].set(vals.reshape(-1))


def kernel(x_prompt, x_sample, c, state_mlstm_C, state_mlstm_n, state_mlstm_m, state_delta, state_ret,
           state_rwkv, c_ctx, w_mod, b_mod, w_in, w_out, ln1_g, ln1_b, ln2_g, ln2_b, w_ff1, w_ff2,
           mlstm_i_bias, mlstm_f_bias, mlstm_norm, delta_conv, delta_a_log, delta_dt_bias, delta_norm,
           ret_decay, ret_norm, rwkv_mu, rwkv_w0, rwkv_w2, rwkv_a0, rwkv_a2, rwkv_g2, rwkv_kk, rwkv_ka,
           rwkv_rk, rwkv_norm):
    f32 = jnp.float32
    batch, seq, d = x_prompt.shape
    dec_batch, dec_seq, _ = x_sample.shape
    assert d == D_MODEL and (batch * seq) % ROW_TILE == 0 and dec_seq % ROW_TILE == 0 and seq % CHUNK == 0
    assert (seq // CHUNK) % 2 == 0 and (dec_seq // CHUNK) % 2 == 0
    n_prows = batch * seq
    n_ptiles = n_prows // ROW_TILE
    tiles_per_dec = dec_seq // ROW_TILE

    c_all = jnp.zeros((8, d), f32).at[0].set(c_ctx).at[1:1 + dec_batch].set(c)
    mod = _mod_call(c_all, w_mod, b_mod).reshape(DEPTH, 8, 6, d)
    n_rows = n_prows + dec_batch * dec_seq
    h_pair = (x_prompt.reshape(n_prows, d), x_sample.reshape(dec_batch * dec_seq, d))
    cos_t, sin_t = _rope_tables(dec_seq)

    zmat = jnp.zeros((batch, 1) + MAT, f32)
    zn = jnp.zeros((batch, 1) + VEC, f32)
    zm = jnp.zeros((batch, 1) + SCL, f32)
    n_cache = state_mlstm_n.reshape(state_mlstm_n.shape[:-1] + (1, HEAD_DIM))
    m_cache = state_mlstm_m.reshape(state_mlstm_m.shape + (1, 1))
    groups = ((seq, batch, 0), (dec_seq, dec_batch, n_prows))

    new_states = []
    for l in range(DEPTH):
        wl = w_in[l]
        ws = [wl[:, 0:1024], wl[:, 1040:2064], wl[:, 2080:3104], wl[:, 3104:4256],
              jnp.pad(jnp.concatenate([wl[:, 1024:1040], wl[:, 2064:2080]], axis=1), ((0, 0), (0, GATE_LANES - N_GATES)))]
        ws = [w.astype(BF) for w in ws]
        ws.append(ws[4][:, :N_GATES].T)
        pa, pb, pc, pd, pg, pgt = _proj_call(h_pair, n_rows, mod[l], ws, n_ptiles, tiles_per_dec)

        bias_c = _lane_row(mlstm_i_bias[l], 0) + _lane_row(mlstm_f_bias[l], 8)
        bias_r = bias_c[0, :N_GATES].reshape(N_GATES, 1)
        dl_row = _lane_row(delta_a_log[l], 24, 2, 0) + _lane_row(delta_dt_bias[l], 24, 2, 1)
        dl_col = dl_row[:, :N_GATES].T
        rd_row = _lane_row(ret_decay[l], 0)
        rw_vec = jnp.stack([rwkv_kk[l], rwkv_ka[l], rwkv_rk[l].reshape(-1), rwkv_norm[l]], axis=0)
        row = lambda a: a.reshape(1, -1)

        ys = [[], [], [], []]
        layer_states = None
        for gi, (t_len, n_seq, off) in enumerate(groups):
            ctx = gi == 0
            lay = 0 if ctx else l
            ya, c1, n1, m1 = _mixer_call(
                mlstm_kernel, "mlstm", t_len, n_seq, off, [pa, pg, pgt],
                [bias_c, bias_r, row(mlstm_norm[l])],
                [zmat, zn, zm] if ctx else [state_mlstm_C, n_cache, m_cache], lay, [MAT, VEC, SCL],
                lambda ns: [_bd_state(ns), pltpu.VMEM((2 * ns, 1, GROUP_W), f32), pltpu.VMEM((2 * ns, 1, GROUP_W), f32)])
            yb, sb1 = _mixer_call(
                delta_kernel, "delta", t_len, n_seq, off, [pb, pg, pgt],
                [delta_conv[l], dl_row, dl_col, row(delta_norm[l])],
                [zmat if ctx else state_delta], lay, [MAT], lambda ns: [_bd_state(ns)])
            ret_consts = [rd_row, row(ret_norm[l])]
            if not ctx:
                ret_consts = [cos_t, sin_t] + ret_consts
            yc, sc1 = _mixer_call(
                functools.partial(ret_kernel, rope=not ctx), "ret", t_len, n_seq, off, [pc],
                ret_consts, [zmat if ctx else state_ret], lay, [MAT],
                lambda ns: [_bd_state(ns), pltpu.VMEM((2, 4, CHUNK, GROUP_W), f32)])
            yd, sd1 = _mixer_call(
                rwkv_kernel, "rwkv", t_len, n_seq, off, [pd],
                [row(rwkv_mu[l]), rwkv_w0[l], rwkv_w2[l], rwkv_a0[l], rwkv_a2[l], rwkv_g2[l], rw_vec],
                [zmat if ctx else state_rwkv], lay, [MAT], lambda ns: [_bd_state(ns)])
            for lst, y in zip(ys, (ya, yb, yc, yd)):
                lst.append(y)
            if ctx:
                layer_states = (c1, n1.reshape(n1.shape[:3] + (HEAD_DIM,)), m1.reshape(m1.shape[:3]), sb1, sc1, sd1)
        new_states.append(layer_states)
        post_ws = [w_out[l].astype(BF), row(ln1_g[l]), row(ln1_b[l]), w_ff1[l].astype(BF), w_ff2[l].astype(BF),
                   row(ln2_g[l]), row(ln2_b[l])]
        last = l == DEPTH - 1
        h_new = _post_call(h_pair, [tuple(y) for y in ys], n_rows, mod[l], post_ws, n_ptiles, tiles_per_dec, last)
        h_pair = tuple(h_new) if last else (h_new, h_new)

    outs = [jnp.stack([st[i] for st in new_states], axis=1) for i in range(6)]
    return (h_pair[0].reshape(batch, seq, d), h_pair[1].reshape(dec_batch, dec_seq, d), *outs)
```

```python
import functools

import jax
import jax.numpy as jnp
from jax import lax
from jax.experimental import pallas as pl
from jax.experimental.pallas import tpu as pltpu

D_MODEL = 1024
DEPTH = 2
GRID_W = 64
HEAD_DIM = 64
N_HEADS = 4
GROUP_W = N_HEADS * HEAD_DIM
CHUNK = 64
CONV_K = 5
ROPE_BASE = 10000.0
LN_EPS = 1e-5
DEEPNORM_ALPHA = (2 * DEPTH) ** 0.25
RWKV_LORA = 64
GATE_LANES = 128
N_GATES = 32
ROW_TILE = 512
VMEM_LIMIT = 56 * 1024 * 1024
MAX_SCAN_STEPS = 4
PRE_BATCH = 8
HALO = CHUNK + 16

NEG_INF = float("-inf")
BF = jnp.bfloat16
_NN = (((1,), (0,)), ((), ()))
_NT = (((1,), (1,)), ((), ()))
_TN = (((0,), (0,)), ((), ()))


def _dot(a, b, dims):
    return lax.dot_general(a, b, dims, preferred_element_type=jnp.result_type(a.dtype, jnp.float32))


def _bf(x):
    return x.astype(BF)


def _split(x, n):
    parts = []
    for _ in range(n - 1):
        p = _bf(x)
        parts.append(p)
        x = x - p.astype(x.dtype)
    parts.append(_bf(x))
    return parts


def bmm(a, b, dims=_NN):
    return _dot(_bf(a), _bf(b), dims)


def _expand(x, bdm):
    return jnp.concatenate([x] * N_HEADS, axis=0) * bdm[0].astype(x.dtype)


def wmm(a, b, bdm, passes=1, nt=False):
    dims = _NT if nt else _NN
    if passes == 1:
        return _dot(_bf(a), _expand(_bf(b), bdm), dims)
    ah, al = _split(a, 2)
    bh, bl = _split(b, 2)
    ebh = _expand(bh, bdm)
    lhs = jnp.concatenate([ah, ah, al], axis=1)
    rhs = jnp.concatenate([ebh, _expand(bl, bdm), ebh], axis=1 if nt else 0)
    return _dot(lhs, rhs, dims)


def cumsum_col(mc, x):
    return _dot(jnp.concatenate([_bf(mc)] * 3, axis=1), jnp.concatenate(_split(x, 3), axis=0), _NN)


def cumsum_row(x, mc):
    return _dot(jnp.concatenate(_split(x, 3), axis=1), jnp.concatenate([_bf(mc)] * 3, axis=1), _NT)


def head_sum(x, bd):
    return _dot(jnp.concatenate(_split(x, 2), axis=1), jnp.concatenate([_bf(bd)] * 2, axis=0), _NN)


def sigmoid(x):
    return 0.5 + 0.5 * jnp.tanh(0.5 * x)


def silu(x):
    return x * sigmoid(x)


def softplus(x):
    return jnp.maximum(x, 0.0) + jnp.log(1.0 + jnp.exp(-jnp.abs(x)))


def log_sigmoid(x):
    return -softplus(-x)


def head_norm(y, gain, bd, center):
    if center:
        y = y - head_sum(y, bd) * (1.0 / HEAD_DIM)
    ms = head_sum(y * y, bd) * (1.0 / HEAD_DIM)
    return y * lax.rsqrt(ms + LN_EPS) * gain


def l2norm_heads(x, bd):
    return x * lax.rsqrt(head_sum(x * x, bd) + 1e-6)


def layer_norm(x):
    mu = jnp.mean(x, axis=-1, keepdims=True)
    xc = x - mu
    var = jnp.mean(xc * xc, axis=-1, keepdims=True)
    return xc * lax.rsqrt(var + LN_EPS)


def colw(x, idx):
    m = x.shape[0]
    return jnp.concatenate([jnp.broadcast_to(x[:, j:j + 1], (m, HEAD_DIM)) for j in idx], axis=1)


def roww(x, idx):
    return jnp.concatenate([x[j:j + 1, :] for j in idx], axis=1)


def segmax(x):
    m = x.shape[0]
    return jnp.concatenate(
        [jnp.broadcast_to(jnp.max(x[:, h * HEAD_DIM:(h + 1) * HEAD_DIM], axis=1, keepdims=True), (m, HEAD_DIM))
         for h in range(N_HEADS)], axis=1)


def wide_masks(rev, dtype=jnp.float32):
    r = lax.broadcasted_iota(jnp.int32, (CHUNK, GROUP_W), 0)
    c = jnp.bitwise_and(lax.broadcasted_iota(jnp.int32, (CHUNK, GROUP_W), 1), HEAD_DIM - 1)
    if rev:
        incl_w, strict_w = r <= c, r < c
    else:
        incl_w, strict_w = r >= c, r > c
    mc = incl_w[:, :CHUNK].astype(dtype)
    return incl_w, strict_w, mc, (r == c).astype(dtype)


def block_diag_mask():
    r = jnp.right_shift(lax.broadcasted_iota(jnp.int32, (GROUP_W, GROUP_W), 0), 6)
    c = jnp.right_shift(lax.broadcasted_iota(jnp.int32, (GROUP_W, GROUP_W), 1), 6)
    m = (r == c).astype(jnp.float32)
    return _bf(m), m


def inv_unit_tri_w(a, eye_w, bdm):
    p = _bf(-a)
    e = _expand(p, bdm)
    t = eye_w - a
    for _ in range(5):
        p = _bf(_dot(p, e, _NN))
        e = _expand(p, bdm)
        t = t + _dot(_bf(t), e, _NN)
    r = eye_w - t - wmm(a, t, bdm, 3)
    return t + wmm(t, r, bdm, 1)


def mlstm_pre(q, k, v, bcol_w, brow_w, icol_w, irow_w, bl_w, incl_w, bdm):
    logd = jnp.where(incl_w, bcol_w - brow_w + irow_w, NEG_INF)
    mt0 = segmax(logd)
    s0 = wmm(q, k, bdm, 1, nt=True) * jnp.exp(logd - mt0)
    wk = bl_w - bcol_w + icol_w
    wkmax = jnp.max(wk, axis=0, keepdims=True)
    ks0 = k * jnp.exp(wk - wkmax)
    return (q, bcol_w, bl_w, mt0, wmm(s0, v, bdm, 1), head_sum(s0, bdm[1]), wkmax,
            bmm(ks0, v, _TN) * bdm[1], jnp.sum(ks0, axis=0, keepdims=True))


def mlstm_step(q, bcol_w, bl_w, mt0, num0, den0, wkmax, kv0, ksum0, c_bd, n_row, m_w, bdm):
    inter = bcol_w + m_w
    mt = jnp.maximum(inter, mt0)
    f = jnp.exp(mt0 - mt)
    ew = jnp.exp(inter - mt)
    num = num0 * f + bmm(q, c_bd) * ew
    den = den0 * f + head_sum(q * n_row, bdm[1]) * ew
    h = num / jnp.maximum(jnp.abs(den), jnp.exp(-mt))
    m_new = jnp.maximum(bl_w + m_w, wkmax)
    sp = jnp.exp(bl_w + m_w - m_new)
    sc = jnp.exp(wkmax - m_new)
    return h, c_bd * sp + kv0 * sc, n_row * sp + ksum0 * sc, m_new


def delta_pre(q_raw, k_raw, v, beta_w, gcol_w, grow_w, gl_w, incl_w, strict_w, eye_w, bdm):
    q = l2norm_heads(q_raw, bdm[1]) * HEAD_DIM ** -0.5
    k = l2norm_heads(k_raw, bdm[1])
    decay = jnp.exp(jnp.where(incl_w, gcol_w - grow_w, NEG_INF))
    a = jnp.where(strict_w, wmm(k, k, bdm, 1, nt=True) * decay, 0.0) * beta_w
    t = inv_unit_tri_w(a, eye_w, bdm)
    eg = jnp.exp(gcol_w)
    u = wmm(t, v * beta_w, bdm, 1)
    w = wmm(t, k * (beta_w * eg), bdm, 1)
    qk = wmm(q, k, bdm, 1, nt=True) * decay
    return u, jnp.concatenate([w, q * eg], axis=0), qk, k * jnp.exp(gl_w - gcol_w), jnp.exp(gl_w)


def delta_step(u, wq, qk, kd, egl, s_bd, bdm):
    ws = bmm(wq, s_bd)
    v_new = u - ws[:CHUNK]
    o = ws[CHUNK:] + wmm(qk, v_new, bdm, 1)
    s_new = s_bd * egl + bmm(kd, v_new, _TN) * bdm[1]
    return o, s_new


def ret_pre(q, k, v, dmat_w, zeta_w, bdm):
    sc = wmm(q, k, bdm, 1, nt=True) * dmat_w
    return q, wmm(sc, v, bdm, 1), bmm(k * zeta_w, v, _TN) * bdm[1]


def ret_step(q, o1, kv, xi_w, gl_w, s_bd):
    return o1 + bmm(q, s_bd) * xi_w, s_bd * gl_w + kv


def rwkv_pre(r, kkk, kt, a, v, cw, lw, tot, incl_w, strict_w, eye_w, bdm):
    kh = l2norm_heads(kkk, bdm[1])
    b = a * kh
    en = jnp.exp(-cw)
    et = jnp.exp(tot - cw)
    rp = r * jnp.exp(cw)
    khp = kh * jnp.exp(cw - lw)
    ktn = kt * en
    bn = b * en
    lhs = jnp.concatenate([khp, rp], axis=0)
    ab_kt = wmm(lhs, ktn, bdm, 1, nt=True)
    ab_b = wmm(lhs, bn, bdm, 1, nt=True)
    a_kt = jnp.where(strict_w, ab_kt[:CHUNK], 0.0)
    b_kt = jnp.where(incl_w, ab_kt[CHUNK:], 0.0)
    a_b = jnp.where(strict_w, ab_b[:CHUNK], 0.0)
    b_b = jnp.where(incl_w, ab_b[CHUNK:], 0.0)
    t = inv_unit_tri_w(a_b, eye_w, bdm)
    tk = wmm(t, khp, bdm, 1)
    tav = wmm(t, wmm(a_kt, v, bdm, 1), bdm, 1)
    return (jnp.concatenate([tk, rp], axis=0), tav, wmm(b_kt, v, bdm, 1), b_b,
            jnp.concatenate([kt * et, b * et], axis=0))


def rwkv_step(tkr, tav, o1, b_b, kbe, v, etot, s_bd, bdm):
    ls = bmm(tkr, s_bd, _NT)
    u = ls[:CHUNK] + tav
    o = ls[CHUNK:] + o1 - wmm(b_b, u, bdm, 1)
    upd = bmm(jnp.concatenate([v, -u], axis=0), kbe, _TN)
    s_new = s_bd * etot + upd * bdm[1]
    return o, s_new


def _group_loop(nc, g, body):
    n_it = nc // g

    def run(first):
        def it(i, carry):
            body(i, first)
            return carry
        return it

    lax.fori_loop(0, n_it // 2, run(True), 0)
    lax.fori_loop(n_it // 2, n_it, run(False), 0)


def _scan_steps(nc):
    return max(g for g in (1, 2, 4, 8) if g <= MAX_SCAN_STEPS and nc % (2 * g) == 0)


def _group_chunks(i, g, nc, ns):
    out = []
    for s in range(ns):
        for z in range(2):
            for j in range(g):
                step = i * g + j
                c = s * nc + (step if z == 0 else nc - 1 - step)
                out.append((s, z, c, pl.multiple_of(c * CHUNK, CHUNK)))
    return out


def _group_masks(g, ns):
    shape = (2 * g * ns, CHUNK, GROUP_W)
    lane = jnp.right_shift(lax.broadcasted_iota(jnp.int32, shape, 0), g.bit_length() - 1)
    r = lax.broadcasted_iota(jnp.int32, shape, 1)
    c = jnp.bitwise_and(lax.broadcasted_iota(jnp.int32, shape, 2), HEAD_DIM - 1)
    d = jnp.where(jnp.bitwise_and(lane, 1) == 1, c - r, r - c)
    return d >= 0, d > 0, (d == 0).astype(jnp.float32)


def _stack(items):
    return [jnp.stack(x) for x in zip(*items)]


def _pick(arrs, g, j):
    return [jnp.stack([a[k] for k in range(j, a.shape[0], g)]) for a in arrs]


def _halo(ref, s, c, t0, nc):
    rows = ref.shape[0]
    ps = pl.multiple_of(jnp.maximum(t0 - 8, 0), 8)
    nx = pl.multiple_of(jnp.minimum(t0 + CHUNK, rows - 8), 8)
    prev = jnp.where(c > s * nc, ref[pl.ds(ps, 8), :], 0.0)
    nxt = jnp.where(c < s * nc + nc - 1, ref[pl.ds(nx, 8), :], 0.0)
    return jnp.concatenate([prev, ref[pl.ds(t0, CHUNK), :], nxt], axis=0)


def _shift_matrix(offsets):
    rows = len(offsets) * CHUNK
    r = lax.broadcasted_iota(jnp.int32, (rows, 2 * HALO), 0)
    c = lax.broadcasted_iota(jnp.int32, (rows, 2 * HALO), 1)
    c = jnp.where(c >= HALO, c - HALO, c)
    t = jnp.bitwise_and(r, CHUNK - 1)
    target = t + 8
    for j, off in enumerate(offsets):
        target = jnp.where(jnp.right_shift(r, 6) == j, target + off, target)
    return _bf((c == target).astype(jnp.float32))


def _hs(h):
    return slice(h * HEAD_DIM, (h + 1) * HEAD_DIM)


def _load_state(sbd, s0_ref, ns):
    for s in range(ns):
        for z in range(2):
            sbd[2 * s + z] = jnp.zeros((GROUP_W, GROUP_W), jnp.float32)
            for h in range(N_HEADS):
                sbd[2 * s + z, _hs(h), _hs(h)] = s0_ref[s, 0, z, h]


def _store_state(s1_ref, sbd, ns):
    for s in range(ns):
        for z in range(2):
            for h in range(N_HEADS):
                s1_ref[s, 0, z, h] = sbd[2 * s + z, _hs(h), _hs(h)]


def _merge(first, hscr, t0, h):
    if first:
        hscr[pl.ds(t0, CHUNK), :] = h
        return None
    return hscr[pl.ds(t0, CHUNK), :] + h


def _scan(nc, g, ns, prep, pre_fn, n_masks, step_fn, state_refs, hscr, finish, y_ref):
    bdm = block_diag_mask()
    masks = _group_masks(g, ns)[:n_masks]

    def body(i, first):
        chunks = _group_chunks(i, g, nc, ns)
        preps = [prep(s, z, c, t0, first) for s, z, c, t0 in chunks]
        n_in = len(preps[0][0])
        pre = jax.vmap(pre_fn, in_axes=(0,) * (n_in + n_masks) + (None,))(
            *_stack([p[0] for p in preps]), *masks, bdm)
        extra = _stack([p[1] for p in preps]) if preps[0][1] else []
        states = [ref[...] for ref in state_refs]
        for j in range(g):
            args = _pick(list(pre), g, j) + _pick(extra, g, j) + states
            res = jax.vmap(step_fn, in_axes=(0,) * len(args) + (None,))(*args, bdm)
            states = list(res[1:])
            ks = [lane * g + j for lane in range(2 * ns)]
            tots = [_merge(first, hscr, chunks[k][3], res[0][lane]) for lane, k in enumerate(ks)]
            if not first:
                ys = jax.vmap(finish)(jnp.stack(tots), *_stack([preps[k][2] for k in ks]))
                for lane, k in enumerate(ks):
                    y_ref[pl.ds(chunks[k][3], CHUNK), :] = ys[lane]
        for ref, val in zip(state_refs, states):
            ref[...] = val

    _group_loop(nc, g, body)


def mlstm_kernel(p_ref, gc_ref, gr_ref, bc_ref, br_ref, norm_ref, c0_ref, n0_ref, m0_ref,
                 y_ref, c1_ref, n1_ref, m1_ref, hscr, cbd, nrow, mw, *, nc, g, ns):
    _load_state(cbd, c0_ref, ns)
    for s in range(ns):
        for z in range(2):
            for h in range(N_HEADS):
                nrow[2 * s + z, :, _hs(h)] = n0_ref[s, 0, z, h]
                mw[2 * s + z, :, _hs(h)] = jnp.broadcast_to(m0_ref[s, 0, z, h], (1, HEAD_DIM))
    bd = block_diag_mask()[1]
    mcs = [wide_masks(False)[2], wide_masks(True)[2]]
    gain = norm_ref[...]

    def prep(s, z, c, t0, first):
        x = p_ref[pl.ds(t0, CHUNK), 0:3 * GROUP_W]
        g_c = gc_ref[pl.ds(t0, CHUNK), :] + bc_ref[...]
        g_r = gr_ref[c] + br_ref[...]
        lf_c = log_sigmoid(g_c)
        idx_i = [z * N_HEADS + h for h in range(N_HEADS)]
        idx_f = [8 + j for j in idx_i]
        fin = () if first else (p_ref[pl.ds(t0, CHUNK), 3 * GROUP_W:],)
        return ((x[:, :GROUP_W], x[:, GROUP_W:2 * GROUP_W] * HEAD_DIM ** -0.5, x[:, 2 * GROUP_W:],
                 colw(cumsum_col(mcs[z], lf_c), idx_f), roww(cumsum_row(log_sigmoid(g_r), mcs[z]), idx_f),
                 colw(g_c, idx_i), roww(g_r, idx_i), colw(jnp.sum(lf_c, axis=0, keepdims=True), idx_f)), (), fin)

    def finish(tot, og):
        return _bf(sigmoid(og) * head_norm(tot, gain, bd, True))

    _scan(nc, g, ns, prep, mlstm_pre, 1, mlstm_step, [cbd, nrow, mw], hscr, finish, y_ref)
    _store_state(c1_ref, cbd, ns)
    for s in range(ns):
        for z in range(2):
            for h in range(N_HEADS):
                n1_ref[s, 0, z, h] = nrow[2 * s + z, :, _hs(h)]
                m1_ref[s, 0, z, h] = mw[2 * s + z, :, h * HEAD_DIM:h * HEAD_DIM + 1]


def delta_kernel(p_ref, gc_ref, gr_ref, conv_ref, prow_ref, pcol_ref, norm_ref, s0_ref,
                 y_ref, s1_ref, hscr, sbd, *, nc, g, ns):
    _load_state(sbd, s0_ref, ns)
    bd = block_diag_mask()[1]
    mcs = [wide_masks(False)[2], wide_masks(True)[2]]
    nega_row = -jnp.exp(prow_ref[0:1, :])
    dtb_row = prow_ref[1:2, :]
    nega_col = -jnp.exp(pcol_ref[:, 0:1])
    dtb_col = pcol_ref[:, 1:2]
    qkv_w = 3 * GROUP_W
    gain = norm_ref[...]
    taps = _shift_matrix([j - CONV_K // 2 for j in range(CONV_K)])

    def prep(s, z, c, t0, first):
        ext = _halo(p_ref.at[:, 0:qkv_w], s, c, t0, nc)
        shifted = _dot(taps, jnp.concatenate(_split(ext, 2), axis=0), _NN)
        acc = shifted[0:CHUNK] * conv_ref[0:1, :]
        for j in range(1, CONV_K):
            acc = acc + shifted[j * CHUNK:(j + 1) * CHUNK] * conv_ref[j:j + 1, :]
        qkv = silu(acc)
        g_c = gc_ref[pl.ds(t0, CHUNK), :]
        gg_c = nega_row * softplus(g_c + dtb_row)
        gg_r = nega_col * softplus(gr_ref[c] + dtb_col)
        idx_b = [16 + z * N_HEADS + h for h in range(N_HEADS)]
        idx_a = [8 + j for j in idx_b]
        fin = () if first else (p_ref[pl.ds(t0, CHUNK), qkv_w:],)
        return ((qkv[:, :GROUP_W], qkv[:, GROUP_W:2 * GROUP_W], qkv[:, 2 * GROUP_W:], colw(sigmoid(g_c), idx_b),
                 colw(cumsum_col(mcs[z], gg_c), idx_a), roww(cumsum_row(gg_r, mcs[z]), idx_a),
                 colw(jnp.sum(gg_c, axis=0, keepdims=True), idx_a)), (), fin)

    def finish(tot, zg):
        return _bf(head_norm(tot, gain, bd, False) * silu(zg))

    _scan(nc, g, ns, prep, delta_pre, 3, delta_step, [sbd], hscr, finish, y_ref)
    _store_state(s1_ref, sbd, ns)


def _rope(x, cos, sin, lane_lo):
    n = x.shape[1]
    swapped = jnp.where(lane_lo, pltpu.roll(x, n - 16, 1), pltpu.roll(x, 16, 1))
    return x * cos + swapped * sin


def ret_kernel(*refs, nc, g, ns, rope):
    if rope:
        p_ref, cos_ref, sin_ref, rd_ref, norm_ref, s0_ref, y_ref, s1_ref, hscr, sbd, cst = refs
    else:
        p_ref, rd_ref, norm_ref, s0_ref, y_ref, s1_ref, hscr, sbd, cst = refs
    f32 = jnp.float32
    _load_state(sbd, s0_ref, ns)
    bd = block_diag_mask()[1]
    r = lax.broadcasted_iota(jnp.int32, (CHUNK, GROUP_W), 0).astype(f32)
    cc = jnp.bitwise_and(lax.broadcasted_iota(jnp.int32, (CHUNK, GROUP_W), 1), HEAD_DIM - 1).astype(f32)
    lg_row = -jnp.exp(rd_ref[...])
    gain = norm_ref[...]
    for z in range(2):
        incl_w = wide_masks(z == 1)[0]
        lg = colw(lg_row, [z * N_HEADS + h for h in range(N_HEADS)])
        if z == 0:
            cst[z, 0] = jnp.exp(jnp.where(incl_w, (r - cc) * lg, NEG_INF))
            cst[z, 1] = jnp.exp((r + 1.0) * lg)
            cst[z, 2] = jnp.exp((CHUNK - 1.0 - r) * lg)
        else:
            cst[z, 0] = jnp.exp(jnp.where(incl_w, (cc - r) * lg, NEG_INF))
            cst[z, 1] = jnp.exp((CHUNK - r) * lg)
            cst[z, 2] = jnp.exp(r * lg)
        cst[z, 3] = jnp.exp(CHUNK * lg + 0.0 * r)
    if rope:
        lane = lax.broadcasted_iota(jnp.int32, (CHUNK, GROUP_W), 1)
        lane_lo = jnp.bitwise_and(lane, 31) < 16

    def prep(s, z, c, t0, first):
        x = p_ref[pl.ds(t0, CHUNK), 0:3 * GROUP_W]
        q = x[:, :GROUP_W] * HEAD_DIM ** -0.5
        k = x[:, GROUP_W:2 * GROUP_W]
        if rope:
            tr = pl.multiple_of(t0 - s * nc * CHUNK, CHUNK)
            cos = cos_ref[pl.ds(tr, CHUNK), :]
            sin = sin_ref[pl.ds(tr, CHUNK), :]
            q = _rope(q, cos, sin, lane_lo)
            k = _rope(k, cos, sin, lane_lo)
        fin = () if first else (p_ref[pl.ds(t0, CHUNK), 3 * GROUP_W:],)
        return (q, k, x[:, 2 * GROUP_W:], cst[z, 0], cst[z, 2]), (cst[z, 1], cst[z, 3, 0:1, :]), fin

    def finish(tot, gate):
        return _bf(silu(gate) * head_norm(tot, gain, bd, True))

    _scan(nc, g, ns, prep, ret_pre, 0, lambda *a: ret_step(*a[:-1]), [sbd], hscr, finish, y_ref)
    _store_state(s1_ref, sbd, ns)


def rwkv_kernel(p_ref, mu_ref, w0_ref, w2_ref, a0_ref, a2_ref, g2_ref, vec_ref, s0_ref,
                y_ref, s1_ref, hscr, sbd, *, nc, g, ns):
    _load_state(sbd, s0_ref, ns)
    bd = block_diag_mask()[1]
    mcs = [wide_masks(False)[2], wide_masks(True)[2]]
    kk = vec_ref[0:1, :]
    ka = vec_ref[1:2, :]
    rk = vec_ref[2:3, :]
    gain = vec_ref[3:4, :]
    g2 = g2_ref[...]

    def lora_a(da, zz):
        return sigmoid(a0_ref[zz:zz + 1, :] + bmm(da[:, zz * RWKV_LORA:(zz + 1) * RWKV_LORA], a2_ref[zz]))

    def prep(s, z, c, t0, first):
        ext = _halo(p_ref, s, c, t0, nc)
        cur = ext[8:8 + CHUNK]
        nb = pltpu.roll(ext, 1, 0) + pltpu.roll(ext, HALO - 1, 0)
        pdm = cur + mu_ref[...] * (0.5 * nb[8:8 + CHUNK] - cur)
        r = pdm[:, :GROUP_W]
        k = pdm[:, GROUP_W:2 * GROUP_W]
        v = pdm[:, 2 * GROUP_W:3 * GROUP_W]
        dw = pdm[:, 3 * GROUP_W:3 * GROUP_W + 128]
        da = pdm[:, 3 * GROUP_W + 128:3 * GROUP_W + 256]
        a_z = lora_a(da, z)
        w_pre = w0_ref[z:z + 1, :] + bmm(jnp.tanh(dw[:, z * RWKV_LORA:(z + 1) * RWKV_LORA]), w2_ref[z])
        lw = -jnp.exp(-softplus(-w_pre) - 0.5)
        kt_z = k * (1.0 + (a_z - 1.0) * ka)
        tot = jnp.sum(lw, axis=0, keepdims=True)
        fin = ()
        if not first:
            kt_o = k * (1.0 + (lora_a(da, 1 - z) - 1.0) * ka)
            fin = (pdm[:, 3 * GROUP_W + 256:], r * (kt_z + kt_o) * rk, v)
        return (r, k * kk, kt_z, a_z, v, cumsum_col(mcs[z], lw), lw, tot), (v, jnp.exp(tot)), fin

    def finish(tot_h, dg, rkk, v):
        gate = bmm(sigmoid(dg), g2)
        return _bf(gate * (head_norm(tot_h, gain, bd, True) + head_sum(rkk, bd) * v))

    _scan(nc, g, ns, prep, rwkv_pre, 3, rwkv_step, [sbd], hscr, finish, y_ref)
    _store_state(s1_ref, sbd, ns)


def mod_kernel(c_ref, w_ref, b_ref, o_ref):
    o_ref[0] = bmm(silu(c_ref[...]), w_ref[0]) + b_ref[0]


def _rows(is_ctx, ctx_ref, lat_ref):
    return jnp.where(is_ctx, ctx_ref[...], lat_ref[...])


def proj_kernel(xp_ref, xs_ref, mod_ref, wa, wb, wc, wd, wg, wgt, pa, pb, pc, pd, pg, pgt, *, n_ptiles):
    m = mod_ref[0]
    x = _rows(pl.program_id(0) < n_ptiles, xp_ref, xs_ref)
    u = layer_norm(x) * (1.0 + m[1:2]) + m[0:1]
    ub = u.astype(BF)
    pa[...] = _dot(ub, wa[...], _NN)
    pb[...] = _dot(ub, wb[...], _NN)
    pc[...] = _dot(ub, wc[...], _NN)
    pd[...] = _dot(ub, wd[...], _NN)
    pg[...] = _dot(ub, wg[...], _NN)
    gt = _dot(wgt[...], ub, _NT)
    for j in range(ROW_TILE // CHUNK):
        pgt[j] = gt[:, j * CHUNK:(j + 1) * CHUNK]


def post_kernel(hp, hs, yap, yas, ybp, ybs, ycp, ycs, ydp, yds, mod_ref, wout, l1g, l1b, w1, w2, l2g, l2b,
                *o_refs, n_ptiles):
    m = mod_ref[0]
    is_ctx = pl.program_id(0) < n_ptiles
    y = jnp.concatenate([_rows(is_ctx, yap, yas), _rows(is_ctx, ybp, ybs), _rows(is_ctx, ycp, ycs),
                         _rows(is_ctx, ydp, yds)], axis=1)
    attn = _dot(y, wout[...], _NN)
    h1 = layer_norm(DEEPNORM_ALPHA * _rows(is_ctx, hp, hs) + m[2:3] * attn) * l1g[...] + l1b[...]
    u2 = layer_norm(h1) * (1.0 + m[4:5]) + m[3:4]
    f = _dot(u2.astype(BF), w1[...], _NN)
    f = jnp.square(jnp.maximum(f, 0.0))
    f2 = _dot(f.astype(BF), w2[...], _NN)
    out = layer_norm(DEEPNORM_ALPHA * h1 + m[5:6] * f2) * l2g[...] + l2b[...]
    if len(o_refs) == 1:
        o_refs[0][...] = out
    else:
        @pl.when(is_ctx)
        def _():
            o_refs[0][...] = out

        @pl.when(jnp.logical_not(is_ctx))
        def _():
            o_refs[1][...] = out


def _params(n_grid_axes):
    return pltpu.CompilerParams(dimension_semantics=("arbitrary",) * n_grid_axes,
                                vmem_limit_bytes=VMEM_LIMIT)


def _const_spec(a, layer=None):
    nd = a.ndim
    if layer is None:
        return pl.BlockSpec(a.shape, lambda *_: (0,) * nd, pipeline_mode=pl.Buffered(1))
    return pl.BlockSpec((None,) + a.shape[1:], lambda *_: (layer,) + (0,) * (nd - 1), pipeline_mode=pl.Buffered(1))


def _f32(shape):
    return jax.ShapeDtypeStruct(shape, jnp.float32)


def _mod_call(c_all, w_mod, b_mod):
    depth, d, n = w_mod.shape
    tn = 2048
    return pl.pallas_call(
        mod_kernel, grid=(depth, n // tn),
        in_specs=[pl.BlockSpec(c_all.shape, lambda l, j: (0, 0)),
                  pl.BlockSpec((1, d, tn), lambda l, j: (l, 0, j)),
                  pl.BlockSpec((1, 1, tn), lambda l, j: (l, 0, j))],
        out_specs=pl.BlockSpec((1, c_all.shape[0], tn), lambda l, j: (l, 0, j)),
        out_shape=_f32((depth, c_all.shape[0], n)),
        compiler_params=_params(2), name="mod")(c_all, w_mod, b_mod.reshape(depth, 1, n))


def _mod_spec(layer, n_ptiles, tiles_per_dec):
    def index(i):
        return (layer, jnp.where(i < n_ptiles, 0, 1 + (i - n_ptiles) // tiles_per_dec), 0, 0)
    return pl.BlockSpec((None, 1, 6, D_MODEL), index)


def _pair_specs(pair, n_ptiles):
    width = pair[0].shape[1]
    off = n_ptiles if pair[0] is pair[1] else 0
    return [pl.BlockSpec((ROW_TILE, width), lambda i: (jnp.minimum(i, n_ptiles - 1), 0)),
            pl.BlockSpec((ROW_TILE, width), lambda i: (jnp.maximum(i - n_ptiles, 0) + off, 0))]


def _proj_call(x_pair, n_rows, mod, layer, ws, n_ptiles, tiles_per_dec):
    row = lambda i: (i, 0)
    widths = [w.shape[-1] for w in ws[:5]]
    per_tile = ROW_TILE // CHUNK
    return pl.pallas_call(
        functools.partial(proj_kernel, n_ptiles=n_ptiles), grid=(n_rows // ROW_TILE,),
        in_specs=_pair_specs(x_pair, n_ptiles) + [_mod_spec(layer, n_ptiles, tiles_per_dec)]
                 + [_const_spec(w, layer) for w in ws],
        out_specs=[pl.BlockSpec((ROW_TILE, w), row) for w in widths]
                  + [pl.BlockSpec((per_tile, N_GATES, CHUNK), lambda i: (i, 0, 0))],
        out_shape=[_f32((n_rows, w)) for w in widths] + [_f32((n_rows // CHUNK, N_GATES, CHUNK))],
        compiler_params=_params(1), name="proj")(*x_pair, mod, *ws)


def _post_call(h_pair, y_pairs, n_rows, mod, layer, ws, n_ptiles, tiles_per_dec, split_out):
    pairs = [h_pair] + y_pairs
    if split_out:
        n_prows = n_ptiles * ROW_TILE
        out_specs = _pair_specs((_f32((n_prows, D_MODEL)), _f32((n_rows - n_prows, D_MODEL))), n_ptiles)
        out_shape = [_f32((n_prows, D_MODEL)), _f32((n_rows - n_prows, D_MODEL))]
    else:
        out_specs = pl.BlockSpec((ROW_TILE, D_MODEL), lambda i: (i, 0))
        out_shape = _f32((n_rows, D_MODEL))
    return pl.pallas_call(
        functools.partial(post_kernel, n_ptiles=n_ptiles), grid=(n_rows // ROW_TILE,),
        in_specs=[spec for p in pairs for spec in _pair_specs(p, n_ptiles)]
                 + [_mod_spec(layer, n_ptiles, tiles_per_dec)] + [_const_spec(w, layer) for w in ws],
        out_specs=out_specs, out_shape=out_shape,
        compiler_params=_params(1), name="post")(*[a for p in pairs for a in p], mod, *ws)


MAT = (2, N_HEADS, HEAD_DIM, HEAD_DIM)
VEC = (2, N_HEADS, 1, HEAD_DIM)
SCL = (2, N_HEADS, 1, 1)


def _mixer_call(kernel, name, t_len, n_seq, row_off, seq_inputs, const_inputs, state_inputs, layer,
                state_tails, scratch):
    nc = t_len // CHUNK
    g = _scan_steps(nc)
    ns = max(1, min(n_seq, PRE_BATCH // (2 * g)))
    assert n_seq % ns == 0 and row_off % (ns * t_len) == 0
    blk_off = row_off // (ns * t_len)
    in_specs = []
    for a in seq_inputs:
        if a.ndim == 2:
            in_specs.append(pl.BlockSpec((ns * t_len, a.shape[1]), lambda i: (i + blk_off, 0)))
        else:
            in_specs.append(pl.BlockSpec((ns * nc,) + a.shape[1:], lambda i: (i + blk_off, 0, 0)))
    in_specs += [_const_spec(a, lay) for a, lay in const_inputs]
    for a in state_inputs:
        tail = a.shape[2:]
        in_specs.append(pl.BlockSpec((ns, 1) + tail, lambda i, nd=len(tail): (i, layer) + (0,) * nd))
    out_specs = [pl.BlockSpec((ns * t_len, GROUP_W), lambda i: (i, 0))]
    out_specs += [pl.BlockSpec((ns, 1) + t, lambda i, nd=len(t): (i, 0) + (0,) * nd) for t in state_tails]
    out_shape = [jax.ShapeDtypeStruct((n_seq * t_len, GROUP_W), BF)] + [_f32((n_seq, 1) + t) for t in state_tails]
    return pl.pallas_call(
        functools.partial(kernel, nc=nc, g=g, ns=ns), grid=(n_seq // ns,),
        in_specs=in_specs, out_specs=out_specs, out_shape=out_shape,
        scratch_shapes=[pltpu.VMEM((ns * t_len, GROUP_W), jnp.float32)] + scratch(ns),
        compiler_params=_params(1), name=name)(*seq_inputs, *[a for a, _ in const_inputs], *state_inputs)


def _bd_state(ns):
    return pltpu.VMEM((2 * ns, GROUP_W, GROUP_W), jnp.float32)


def _rope_tables(t_len):
    nf = HEAD_DIM // 4
    inv = ROPE_BASE ** (-jnp.arange(nf, dtype=jnp.float32) / nf)
    t = jnp.arange(t_len)
    ang_r = (t // GRID_W).astype(jnp.float32)[:, None] * inv[None, :]
    ang_c = (t % GRID_W).astype(jnp.float32)[:, None] * inv[None, :]
    cos = jnp.concatenate([jnp.cos(ang_r)] * 2 + [jnp.cos(ang_c)] * 2, axis=1)
    sin = jnp.concatenate([-jnp.sin(ang_r), jnp.sin(ang_r), -jnp.sin(ang_c), jnp.sin(ang_c)], axis=1)
    return jnp.tile(cos, (1, N_HEADS)), jnp.tile(sin, (1, N_HEADS))


def kernel(x_prompt, x_sample, c, state_mlstm_C, state_mlstm_n, state_mlstm_m, state_delta, state_ret,
           state_rwkv, c_ctx, w_mod, b_mod, w_in, w_out, ln1_g, ln1_b, ln2_g, ln2_b, w_ff1, w_ff2,
           mlstm_i_bias, mlstm_f_bias, mlstm_norm, delta_conv, delta_a_log, delta_dt_bias, delta_norm,
           ret_decay, ret_norm, rwkv_mu, rwkv_w0, rwkv_w2, rwkv_a0, rwkv_a2, rwkv_g2, rwkv_kk, rwkv_ka,
           rwkv_rk, rwkv_norm):
    f32 = jnp.float32
    batch, seq, d = x_prompt.shape
    dec_batch, dec_seq, _ = x_sample.shape
    assert d == D_MODEL and (batch * seq) % ROW_TILE == 0 and dec_seq % ROW_TILE == 0 and seq % CHUNK == 0
    assert (seq // CHUNK) % 2 == 0 and (dec_seq // CHUNK) % 2 == 0
    n_prows = batch * seq
    n_ptiles = n_prows // ROW_TILE
    tiles_per_dec = dec_seq // ROW_TILE

    c_all = jnp.zeros((8, d), f32).at[0].set(c_ctx).at[1:1 + dec_batch].set(c)
    mod = _mod_call(c_all, w_mod, b_mod).reshape(DEPTH, 8, 6, d)
    n_rows = n_prows + dec_batch * dec_seq
    h_pair = (x_prompt.reshape(n_prows, d), x_sample.reshape(dec_batch * dec_seq, d))
    cos_t, sin_t = _rope_tables(dec_seq)

    zmat = jnp.zeros((batch, 1) + MAT, f32)
    zn = jnp.zeros((batch, 1) + VEC, f32)
    zm = jnp.zeros((batch, 1) + SCL, f32)
    n_cache = state_mlstm_n.reshape(state_mlstm_n.shape[:-1] + (1, HEAD_DIM))
    m_cache = state_mlstm_m.reshape(state_mlstm_m.shape + (1, 1))
    groups = ((seq, batch, 0), (dec_seq, dec_batch, n_prows))

    bf = lambda a: a.astype(BF)
    rows = lambda a: a.reshape(DEPTH, 1, -1)
    zeros = lambda *shape: jnp.zeros((DEPTH,) + shape, f32)
    wg = bf(jnp.pad(jnp.concatenate([w_in[:, :, 1024:1040], w_in[:, :, 2064:2080]], axis=2),
                    ((0, 0), (0, 0), (0, GATE_LANES - N_GATES))))
    proj_ws = [bf(w_in[:, :, 0:1024]), bf(w_in[:, :, 1040:2064]), bf(w_in[:, :, 2080:3104]), bf(w_in[:, :, 3104:4256]),
               wg, jnp.swapaxes(wg[:, :, :N_GATES], 1, 2)]
    post_ws = [bf(w_out), rows(ln1_g), rows(ln1_b), bf(w_ff1), bf(w_ff2), rows(ln2_g), rows(ln2_b)]
    bias_c = jnp.concatenate([rows(mlstm_i_bias), rows(mlstm_f_bias), zeros(1, GATE_LANES - 16)], axis=2)
    bias_r = jnp.swapaxes(bias_c[:, :, :N_GATES], 1, 2)
    dl_row = jnp.concatenate([zeros(2, 24), jnp.concatenate([rows(delta_a_log), rows(delta_dt_bias)], axis=1),
                              zeros(2, GATE_LANES - 32)], axis=2)
    dl_col = jnp.swapaxes(dl_row[:, :, :N_GATES], 1, 2)
    rd_row = jnp.concatenate([rows(ret_decay), zeros(1, GATE_LANES - 8)], axis=2)
    rw_vec = jnp.concatenate([rows(rwkv_kk), rows(rwkv_ka), rows(rwkv_rk), rows(rwkv_norm)], axis=1)

    new_states = []
    for l in range(DEPTH):
        lay_of = lambda *arrs: [(a, l) for a in arrs]
        pa, pb, pc, pd, pg, pgt = _proj_call(h_pair, n_rows, mod, l, proj_ws, n_ptiles, tiles_per_dec)
        ys = [[], [], [], []]
        layer_states = None
        for gi, (t_len, n_seq, off) in enumerate(groups):
            ctx = gi == 0
            lay = 0 if ctx else l
            ya, c1, n1, m1 = _mixer_call(
                mlstm_kernel, "mlstm", t_len, n_seq, off, [pa, pg, pgt],
                lay_of(bias_c, bias_r, rows(mlstm_norm)),
                [zmat, zn, zm] if ctx else [state_mlstm_C, n_cache, m_cache], lay, [MAT, VEC, SCL],
                lambda ns: [_bd_state(ns), pltpu.VMEM((2 * ns, 1, GROUP_W), f32), pltpu.VMEM((2 * ns, 1, GROUP_W), f32)])
            yb, sb1 = _mixer_call(
                delta_kernel, "delta", t_len, n_seq, off, [pb, pg, pgt],
                lay_of(delta_conv, dl_row, dl_col, rows(delta_norm)),
                [zmat if ctx else state_delta], lay, [MAT], lambda ns: [_bd_state(ns)])
            ret_consts = lay_of(rd_row, rows(ret_norm))
            if not ctx:
                ret_consts = [(cos_t, None), (sin_t, None)] + ret_consts
            yc, sc1 = _mixer_call(
                functools.partial(ret_kernel, rope=not ctx), "ret", t_len, n_seq, off, [pc],
                ret_consts, [zmat if ctx else state_ret], lay, [MAT],
                lambda ns: [_bd_state(ns), pltpu.VMEM((2, 4, CHUNK, GROUP_W), f32)])
            yd, sd1 = _mixer_call(
                rwkv_kernel, "rwkv", t_len, n_seq, off, [pd],
                lay_of(rows(rwkv_mu), rwkv_w0, rwkv_w2, rwkv_a0, rwkv_a2, rwkv_g2, rw_vec),
                [zmat if ctx else state_rwkv], lay, [MAT], lambda ns: [_bd_state(ns)])
            for lst, y in zip(ys, (ya, yb, yc, yd)):
                lst.append(y)
            if ctx:
                layer_states = (c1, n1, m1, sb1, sc1, sd1)
        new_states.append(layer_states)
        last = l == DEPTH - 1
        h_new = _post_call(h_pair, [tuple(y) for y in ys], n_rows, mod, l, post_ws, n_ptiles, tiles_per_dec, last)
        h_pair = tuple(h_new) if last else (h_new, h_new)

    outs = [jnp.concatenate([st[i] for st in new_states], axis=1) for i in range(6)]
    outs[1] = outs[1].reshape(outs[1].shape[:4] + (HEAD_DIM,))
    outs[2] = outs[2].reshape(outs[2].shape[:4])
    return (h_pair[0].reshape(batch, seq, d), h_pair[1].reshape(dec_batch, dec_seq, d), *outs)
```

```python
import functools

import jax
import jax.numpy as jnp
from jax import lax
from jax.experimental import pallas as pl
from jax.experimental.pallas import tpu as pltpu

D_MODEL = 1024
DEPTH = 2
GRID_W = 64
HEAD_DIM = 64
N_HEADS = 4
GROUP_W = N_HEADS * HEAD_DIM
CHUNK = 64
CONV_K = 5
ROPE_BASE = 10000.0
LN_EPS = 1e-5
DEEPNORM_ALPHA = (2 * DEPTH) ** 0.25
RWKV_LORA = 64
GATE_LANES = 128
N_GATES = 32
ROW_TILE = 512
VMEM_LIMIT = 56 * 1024 * 1024
MAX_SCAN_STEPS = 4
PRE_BATCH = 8
HALO = CHUNK + 16

NEG_INF = float("-inf")
BF = jnp.bfloat16
_NN = (((1,), (0,)), ((), ()))
_NT = (((1,), (1,)), ((), ()))
_TN = (((0,), (0,)), ((), ()))


def _dot(a, b, dims):
    return lax.dot_general(a, b, dims, preferred_element_type=jnp.result_type(a.dtype, jnp.float32))


def _bf(x):
    return x.astype(BF)


def _split(x, n):
    parts = []
    for _ in range(n - 1):
        p = _bf(x)
        parts.append(p)
        x = x - p.astype(x.dtype)
    parts.append(_bf(x))
    return parts


def bmm(a, b, dims=_NN):
    return _dot(_bf(a), _bf(b), dims)


def _expand(x, bdm):
    lane_head = jnp.right_shift(lax.broadcasted_iota(jnp.int32, (CHUNK, GROUP_W), 1), 6)
    zero = jnp.zeros((), x.dtype)
    return jnp.concatenate([jnp.where(lane_head == h, x, zero) for h in range(N_HEADS)], axis=0)


def wmm(a, b, bdm, passes=1, nt=False):
    dims = _NT if nt else _NN
    if passes == 1:
        return _dot(_bf(a), _expand(_bf(b), bdm), dims)
    ah, al = _split(a, 2)
    bh, bl = _split(b, 2)
    ebh = _expand(bh, bdm)
    lhs = jnp.concatenate([ah, ah, al], axis=1)
    rhs = jnp.concatenate([ebh, _expand(bl, bdm), ebh], axis=1 if nt else 0)
    return _dot(lhs, rhs, dims)


def cumsum_col(mc, x):
    return _dot(jnp.concatenate([_bf(mc)] * 3, axis=1), jnp.concatenate(_split(x, 3), axis=0), _NN)


def cumsum_row(x, mc):
    return _dot(jnp.concatenate(_split(x, 3), axis=1), jnp.concatenate([_bf(mc)] * 3, axis=1), _NT)


def head_sum(x, bd):
    return _dot(jnp.concatenate(_split(x, 2), axis=1), jnp.concatenate([_bf(bd)] * 2, axis=0), _NN)


def sigmoid(x):
    return 0.5 + 0.5 * jnp.tanh(0.5 * x)


def silu(x):
    return x * sigmoid(x)


def softplus(x):
    return jnp.maximum(x, 0.0) + jnp.log(1.0 + jnp.exp(-jnp.abs(x)))


def log_sigmoid(x):
    return -softplus(-x)


def head_norm(y, gain, bd, center):
    if center:
        y = y - head_sum(y, bd) * (1.0 / HEAD_DIM)
    ms = head_sum(y * y, bd) * (1.0 / HEAD_DIM)
    return y * lax.rsqrt(ms + LN_EPS) * gain


def l2norm_heads(x, bd):
    return x * lax.rsqrt(head_sum(x * x, bd) + 1e-6)


def layer_norm(x):
    mu = jnp.mean(x, axis=-1, keepdims=True)
    xc = x - mu
    var = jnp.mean(xc * xc, axis=-1, keepdims=True)
    return xc * lax.rsqrt(var + LN_EPS)


def colw(x, idx):
    m = x.shape[0]
    return jnp.concatenate([jnp.broadcast_to(x[:, j:j + 1], (m, HEAD_DIM)) for j in idx], axis=1)


def roww(x, idx):
    return jnp.concatenate([x[j:j + 1, :] for j in idx], axis=1)


def segmax(x):
    m = x.shape[0]
    return jnp.concatenate(
        [jnp.broadcast_to(jnp.max(x[:, h * HEAD_DIM:(h + 1) * HEAD_DIM], axis=1, keepdims=True), (m, HEAD_DIM))
         for h in range(N_HEADS)], axis=1)


def wide_masks(rev, dtype=jnp.float32):
    r = lax.broadcasted_iota(jnp.int32, (CHUNK, GROUP_W), 0)
    c = jnp.bitwise_and(lax.broadcasted_iota(jnp.int32, (CHUNK, GROUP_W), 1), HEAD_DIM - 1)
    if rev:
        incl_w, strict_w = r <= c, r < c
    else:
        incl_w, strict_w = r >= c, r > c
    mc = incl_w[:, :CHUNK].astype(dtype)
    return incl_w, strict_w, mc, (r == c).astype(dtype)


def block_diag_mask():
    r = jnp.right_shift(lax.broadcasted_iota(jnp.int32, (GROUP_W, GROUP_W), 0), 6)
    c = jnp.right_shift(lax.broadcasted_iota(jnp.int32, (GROUP_W, GROUP_W), 1), 6)
    m = (r == c).astype(jnp.float32)
    return _bf(m), m


def inv_unit_tri_w(a, eye_w, bdm):
    p = _bf(-a)
    e = _expand(p, bdm)
    t = eye_w - a
    for _ in range(5):
        p = _bf(_dot(p, e, _NN))
        e = _expand(p, bdm)
        t = t + _dot(_bf(t), e, _NN)
    r = eye_w - t - wmm(a, t, bdm, 3)
    return t + wmm(t, r, bdm, 1)


def mlstm_pre(q, k, v, bcol_w, brow_w, icol_w, irow_w, bl_w, incl_w, bdm):
    logd = jnp.where(incl_w, bcol_w - brow_w + irow_w, NEG_INF)
    mt0 = segmax(logd)
    s0 = wmm(q, k, bdm, 1, nt=True) * jnp.exp(logd - mt0)
    wk = bl_w - bcol_w + icol_w
    wkmax = jnp.max(wk, axis=0, keepdims=True)
    ks0 = k * jnp.exp(wk - wkmax)
    return (q, bcol_w, bl_w, mt0, wmm(s0, v, bdm, 1), head_sum(s0, bdm[1]), wkmax,
            _bf(ks0), _bf(v), jnp.sum(ks0, axis=0, keepdims=True))


def mlstm_step(q, bcol_w, bl_w, mt0, num0, den0, wkmax, ks0, v, ksum0, c_bd, n_row, m_w, bdm):
    inter = bcol_w + m_w
    mt = jnp.maximum(inter, mt0)
    f = jnp.exp(mt0 - mt)
    ew = jnp.exp(inter - mt)
    num = num0 * f + bmm(q, c_bd) * ew
    den = den0 * f + head_sum(q * n_row, bdm[1]) * ew
    h = num / jnp.maximum(jnp.abs(den), jnp.exp(-mt))
    m_new = jnp.maximum(bl_w + m_w, wkmax)
    sp = jnp.exp(bl_w + m_w - m_new)
    sc = jnp.exp(wkmax - m_new)
    kv0 = bmm(ks0, v, _TN) * bdm[1]
    return h, c_bd * sp + kv0 * sc, n_row * sp + ksum0 * sc, m_new


def delta_pre(q_raw, k_raw, v, beta_w, gcol_w, grow_w, gl_w, incl_w, strict_w, eye_w, bdm):
    q = l2norm_heads(q_raw, bdm[1]) * HEAD_DIM ** -0.5
    k = l2norm_heads(k_raw, bdm[1])
    decay = jnp.exp(jnp.where(incl_w, gcol_w - grow_w, NEG_INF))
    a = jnp.where(strict_w, wmm(k, k, bdm, 1, nt=True) * decay, 0.0) * beta_w
    t = inv_unit_tri_w(a, eye_w, bdm)
    eg = jnp.exp(gcol_w)
    u = wmm(t, v * beta_w, bdm, 1)
    w = wmm(t, k * (beta_w * eg), bdm, 1)
    qk = wmm(q, k, bdm, 1, nt=True) * decay
    return (u, _bf(jnp.concatenate([w, q * eg], axis=0)), _bf(qk), _bf(k * jnp.exp(gl_w - gcol_w)),
            jnp.exp(gl_w))


def delta_step(u, wq, qk, kd, egl, s_bd, bdm):
    ws = bmm(wq, s_bd)
    v_new = u - ws[:CHUNK]
    o = ws[CHUNK:] + wmm(qk, v_new, bdm, 1)
    s_new = s_bd * egl + bmm(kd, v_new, _TN) * bdm[1]
    return o, s_new


def ret_pre(q, k, v, dmat_w, zeta_w, bdm):
    sc = wmm(q, k, bdm, 1, nt=True) * dmat_w
    return _bf(q), wmm(sc, v, bdm, 1), _bf(k * zeta_w), _bf(v)


def ret_step(q, o1, kz, v, xi_w, gl_w, s_bd, bdm):
    return o1 + bmm(q, s_bd) * xi_w, s_bd * gl_w + bmm(kz, v, _TN) * bdm[1]


def rwkv_pre(r, kkk, kt, a, v, cw, lw, tot, incl_w, strict_w, eye_w, bdm):
    kh = l2norm_heads(kkk, bdm[1])
    b = a * kh
    en = jnp.exp(-cw)
    et = jnp.exp(tot - cw)
    rp = r * jnp.exp(cw)
    khp = kh * jnp.exp(cw - lw)
    ktn = kt * en
    bn = b * en
    lhs = jnp.concatenate([khp, rp], axis=0)
    ab_kt = wmm(lhs, ktn, bdm, 1, nt=True)
    ab_b = wmm(lhs, bn, bdm, 1, nt=True)
    a_kt = jnp.where(strict_w, ab_kt[:CHUNK], 0.0)
    b_kt = jnp.where(incl_w, ab_kt[CHUNK:], 0.0)
    a_b = jnp.where(strict_w, ab_b[:CHUNK], 0.0)
    b_b = jnp.where(incl_w, ab_b[CHUNK:], 0.0)
    t = inv_unit_tri_w(a_b, eye_w, bdm)
    tk = wmm(t, khp, bdm, 1)
    tav = wmm(t, wmm(a_kt, v, bdm, 1), bdm, 1)
    return (_bf(jnp.concatenate([tk, rp], axis=0)), tav, wmm(b_kt, v, bdm, 1), _bf(b_b),
            _bf(jnp.concatenate([kt * et, b * et], axis=0)))


def rwkv_step(tkr, tav, o1, b_b, kbe, v, etot, s_bd, bdm):
    ls = bmm(tkr, s_bd, _NT)
    u = ls[:CHUNK] + tav
    o = ls[CHUNK:] + o1 - wmm(b_b, u, bdm, 1)
    upd = bmm(jnp.concatenate([v, -u], axis=0), kbe, _TN)
    s_new = s_bd * etot + upd * bdm[1]
    return o, s_new


def _group_loop(nc, g, body):
    n_it = nc // g

    def run(first):
        def it(i, carry):
            body(i, first)
            return carry
        return it

    lax.fori_loop(0, n_it // 2, run(True), 0)
    lax.fori_loop(n_it // 2, n_it, run(False), 0)


def _scan_steps(nc):
    return max(g for g in (1, 2, 4, 8) if g <= MAX_SCAN_STEPS and nc % (2 * g) == 0)


def _group_chunks(i, g, nc, ns):
    out = []
    for s in range(ns):
        for z in range(2):
            for j in range(g):
                step = i * g + j
                c = s * nc + (step if z == 0 else nc - 1 - step)
                out.append((s, z, c, pl.multiple_of(c * CHUNK, CHUNK)))
    return out


def _group_masks(g, ns):
    shape = (2 * g * ns, CHUNK, GROUP_W)
    lane = jnp.right_shift(lax.broadcasted_iota(jnp.int32, shape, 0), g.bit_length() - 1)
    r = lax.broadcasted_iota(jnp.int32, shape, 1)
    c = jnp.bitwise_and(lax.broadcasted_iota(jnp.int32, shape, 2), HEAD_DIM - 1)
    d = jnp.where(jnp.bitwise_and(lane, 1) == 1, c - r, r - c)
    return d >= 0, d > 0, (d == 0).astype(jnp.float32)


def _stack(items):
    return [jnp.stack(x) for x in zip(*items)]


def _pick(arrs, g, j):
    return [jnp.stack([a[k] for k in range(j, a.shape[0], g)]) for a in arrs]


def _halo(ref, s, c, t0, nc):
    rows = ref.shape[0]
    ps = pl.multiple_of(jnp.maximum(t0 - 8, 0), 8)
    nx = pl.multiple_of(jnp.minimum(t0 + CHUNK, rows - 8), 8)
    prev = jnp.where(c > s * nc, ref[pl.ds(ps, 8), :], 0.0)
    nxt = jnp.where(c < s * nc + nc - 1, ref[pl.ds(nx, 8), :], 0.0)
    return jnp.concatenate([prev, ref[pl.ds(t0, CHUNK), :], nxt], axis=0)


def _shift_matrix(offsets):
    rows = len(offsets) * CHUNK
    r = lax.broadcasted_iota(jnp.int32, (rows, 2 * HALO), 0)
    c = lax.broadcasted_iota(jnp.int32, (rows, 2 * HALO), 1)
    c = jnp.where(c >= HALO, c - HALO, c)
    t = jnp.bitwise_and(r, CHUNK - 1)
    target = t + 8
    for j, off in enumerate(offsets):
        target = jnp.where(jnp.right_shift(r, 6) == j, target + off, target)
    return _bf((c == target).astype(jnp.float32))


def _hs(h):
    return slice(h * HEAD_DIM, (h + 1) * HEAD_DIM)


def _load_state(sbd, s0_ref, ns):
    for s in range(ns):
        for z in range(2):
            sbd[2 * s + z] = jnp.zeros((GROUP_W, GROUP_W), jnp.float32)
            for h in range(N_HEADS):
                sbd[2 * s + z, _hs(h), _hs(h)] = s0_ref[s, 0, z, h]


def _store_state(s1_ref, sbd, ns):
    for s in range(ns):
        for z in range(2):
            for h in range(N_HEADS):
                s1_ref[s, 0, z, h] = sbd[2 * s + z, _hs(h), _hs(h)]


def _merge(first, hscr, t0, h):
    if first:
        hscr[pl.ds(t0, CHUNK), :] = h
        return None
    return hscr[pl.ds(t0, CHUNK), :] + h


def _scan(nc, g, ns, prep, pre_fn, n_masks, step_fn, state_refs, hscr, finish, y_ref):
    bdm = block_diag_mask()
    masks = _group_masks(g, ns)[:n_masks]

    def body(i, first):
        chunks = _group_chunks(i, g, nc, ns)
        preps = [prep(s, z, c, t0, first) for s, z, c, t0 in chunks]
        n_in = len(preps[0][0])
        pre = jax.vmap(pre_fn, in_axes=(0,) * (n_in + n_masks) + (None,))(
            *_stack([p[0] for p in preps]), *masks, bdm)
        extra = _stack([p[1] for p in preps]) if preps[0][1] else []
        states = [ref[...] for ref in state_refs]
        for j in range(g):
            args = _pick(list(pre), g, j) + _pick(extra, g, j) + states
            res = jax.vmap(step_fn, in_axes=(0,) * len(args) + (None,))(*args, bdm)
            states = list(res[1:])
            ks = [lane * g + j for lane in range(2 * ns)]
            tots = [_merge(first, hscr, chunks[k][3], res[0][lane]) for lane, k in enumerate(ks)]
            if not first:
                ys = jax.vmap(finish)(jnp.stack(tots), *_stack([preps[k][2] for k in ks]))
                for lane, k in enumerate(ks):
                    y_ref[pl.ds(chunks[k][3], CHUNK), :] = ys[lane]
        for ref, val in zip(state_refs, states):
            ref[...] = val

    _group_loop(nc, g, body)


def mlstm_kernel(p_ref, gc_ref, gr_ref, bc_ref, br_ref, norm_ref, c0_ref, n0_ref, m0_ref,
                 y_ref, c1_ref, n1_ref, m1_ref, hscr, cbd, nrow, mw, *, nc, g, ns):
    _load_state(cbd, c0_ref, ns)
    for s in range(ns):
        for z in range(2):
            for h in range(N_HEADS):
                nrow[2 * s + z, :, _hs(h)] = n0_ref[s, 0, z, h]
                mw[2 * s + z, :, _hs(h)] = jnp.broadcast_to(m0_ref[s, 0, z, h], (1, HEAD_DIM))
    bd = block_diag_mask()[1]
    mcs = [wide_masks(False)[2], wide_masks(True)[2]]
    gain = norm_ref[...]

    def prep(s, z, c, t0, first):
        x = p_ref[pl.ds(t0, CHUNK), 0:3 * GROUP_W]
        g_c = gc_ref[pl.ds(t0, CHUNK), :] + bc_ref[...]
        g_r = gr_ref[c] + br_ref[...]
        lf_c = log_sigmoid(g_c)
        idx_i = [z * N_HEADS + h for h in range(N_HEADS)]
        idx_f = [8 + j for j in idx_i]
        fin = () if first else (p_ref[pl.ds(t0, CHUNK), 3 * GROUP_W:],)
        return ((x[:, :GROUP_W], x[:, GROUP_W:2 * GROUP_W] * HEAD_DIM ** -0.5, x[:, 2 * GROUP_W:],
                 colw(cumsum_col(mcs[z], lf_c), idx_f), roww(cumsum_row(log_sigmoid(g_r), mcs[z]), idx_f),
                 colw(g_c, idx_i), roww(g_r, idx_i), colw(jnp.sum(lf_c, axis=0, keepdims=True), idx_f)), (), fin)

    def finish(tot, og):
        return _bf(sigmoid(og) * head_norm(tot, gain, bd, True))

    _scan(nc, g, ns, prep, mlstm_pre, 1, mlstm_step, [cbd, nrow, mw], hscr, finish, y_ref)
    _store_state(c1_ref, cbd, ns)
    for s in range(ns):
        for z in range(2):
            for h in range(N_HEADS):
                n1_ref[s, 0, z, h] = nrow[2 * s + z, :, _hs(h)]
                m1_ref[s, 0, z, h] = mw[2 * s + z, :, h * HEAD_DIM:h * HEAD_DIM + 1]


def delta_kernel(p_ref, gc_ref, gr_ref, conv_ref, prow_ref, pcol_ref, norm_ref, s0_ref,
                 y_ref, s1_ref, hscr, sbd, *, nc, g, ns):
    _load_state(sbd, s0_ref, ns)
    bd = block_diag_mask()[1]
    mcs = [wide_masks(False)[2], wide_masks(True)[2]]
    nega_row = -jnp.exp(prow_ref[0:1, :])
    dtb_row = prow_ref[1:2, :]
    nega_col = -jnp.exp(pcol_ref[:, 0:1])
    dtb_col = pcol_ref[:, 1:2]
    qkv_w = 3 * GROUP_W
    gain = norm_ref[...]
    taps = _shift_matrix([j - CONV_K // 2 for j in range(CONV_K)])

    def prep(s, z, c, t0, first):
        ext = _halo(p_ref.at[:, 0:qkv_w], s, c, t0, nc)
        shifted = _dot(taps, jnp.concatenate(_split(ext, 2), axis=0), _NN)
        acc = shifted[0:CHUNK] * conv_ref[0:1, :]
        for j in range(1, CONV_K):
            acc = acc + shifted[j * CHUNK:(j + 1) * CHUNK] * conv_ref[j:j + 1, :]
        qkv = silu(acc)
        g_c = gc_ref[pl.ds(t0, CHUNK), :]
        gg_c = nega_row * softplus(g_c + dtb_row)
        gg_r = nega_col * softplus(gr_ref[c] + dtb_col)
        idx_b = [16 + z * N_HEADS + h for h in range(N_HEADS)]
        idx_a = [8 + j for j in idx_b]
        fin = () if first else (p_ref[pl.ds(t0, CHUNK), qkv_w:],)
        return ((qkv[:, :GROUP_W], qkv[:, GROUP_W:2 * GROUP_W], qkv[:, 2 * GROUP_W:], colw(sigmoid(g_c), idx_b),
                 colw(cumsum_col(mcs[z], gg_c), idx_a), roww(cumsum_row(gg_r, mcs[z]), idx_a),
                 colw(jnp.sum(gg_c, axis=0, keepdims=True), idx_a)), (), fin)

    def finish(tot, zg):
        return _bf(head_norm(tot, gain, bd, False) * silu(zg))

    _scan(nc, g, ns, prep, delta_pre, 3, delta_step, [sbd], hscr, finish, y_ref)
    _store_state(s1_ref, sbd, ns)


def _rope(x, cos, sin, lane_lo):
    n = x.shape[1]
    swapped = jnp.where(lane_lo, pltpu.roll(x, n - 16, 1), pltpu.roll(x, 16, 1))
    return x * cos + swapped * sin


def ret_kernel(*refs, nc, g, ns, rope):
    if rope:
        p_ref, cos_ref, sin_ref, rd_ref, norm_ref, s0_ref, y_ref, s1_ref, hscr, sbd, cst = refs
    else:
        p_ref, rd_ref, norm_ref, s0_ref, y_ref, s1_ref, hscr, sbd, cst = refs
    f32 = jnp.float32
    _load_state(sbd, s0_ref, ns)
    bd = block_diag_mask()[1]
    r = lax.broadcasted_iota(jnp.int32, (CHUNK, GROUP_W), 0).astype(f32)
    cc = jnp.bitwise_and(lax.broadcasted_iota(jnp.int32, (CHUNK, GROUP_W), 1), HEAD_DIM - 1).astype(f32)
    lg_row = -jnp.exp(rd_ref[...])
    gain = norm_ref[...]
    for z in range(2):
        incl_w = wide_masks(z == 1)[0]
        lg = colw(lg_row, [z * N_HEADS + h for h in range(N_HEADS)])
        if z == 0:
            cst[z, 0] = jnp.exp(jnp.where(incl_w, (r - cc) * lg, NEG_INF))
            cst[z, 1] = jnp.exp((r + 1.0) * lg)
            cst[z, 2] = jnp.exp((CHUNK - 1.0 - r) * lg)
        else:
            cst[z, 0] = jnp.exp(jnp.where(incl_w, (cc - r) * lg, NEG_INF))
            cst[z, 1] = jnp.exp((CHUNK - r) * lg)
            cst[z, 2] = jnp.exp(r * lg)
        cst[z, 3] = jnp.exp(CHUNK * lg + 0.0 * r)
    if rope:
        lane = lax.broadcasted_iota(jnp.int32, (CHUNK, GROUP_W), 1)
        lane_lo = jnp.bitwise_and(lane, 31) < 16

    def prep(s, z, c, t0, first):
        x = p_ref[pl.ds(t0, CHUNK), 0:3 * GROUP_W]
        q = x[:, :GROUP_W] * HEAD_DIM ** -0.5
        k = x[:, GROUP_W:2 * GROUP_W]
        if rope:
            tr = pl.multiple_of(t0 - s * nc * CHUNK, CHUNK)
            cos = cos_ref[pl.ds(tr, CHUNK), :]
            sin = sin_ref[pl.ds(tr, CHUNK), :]
            q = _rope(q, cos, sin, lane_lo)
            k = _rope(k, cos, sin, lane_lo)
        fin = () if first else (p_ref[pl.ds(t0, CHUNK), 3 * GROUP_W:],)
        return (q, k, x[:, 2 * GROUP_W:], cst[z, 0], cst[z, 2]), (cst[z, 1], cst[z, 3, 0:1, :]), fin

    def finish(tot, gate):
        return _bf(silu(gate) * head_norm(tot, gain, bd, True))

    _scan(nc, g, ns, prep, ret_pre, 0, ret_step, [sbd], hscr, finish, y_ref)
    _store_state(s1_ref, sbd, ns)


def rwkv_kernel(p_ref, mu_ref, w0_ref, w2_ref, a0_ref, a2_ref, g2_ref, vec_ref, s0_ref,
                y_ref, s1_ref, hscr, sbd, *, nc, g, ns):
    _load_state(sbd, s0_ref, ns)
    bd = block_diag_mask()[1]
    mcs = [wide_masks(False)[2], wide_masks(True)[2]]
    kk = vec_ref[0:1, :]
    ka = vec_ref[1:2, :]
    rk = vec_ref[2:3, :]
    gain = vec_ref[3:4, :]
    g2 = g2_ref[...]

    def lora_a(da, zz):
        return sigmoid(a0_ref[zz:zz + 1, :] + bmm(da[:, zz * RWKV_LORA:(zz + 1) * RWKV_LORA], a2_ref[zz]))

    def prep(s, z, c, t0, first):
        ext = _halo(p_ref, s, c, t0, nc)
        cur = ext[8:8 + CHUNK]
        nb = pltpu.roll(ext, 1, 0) + pltpu.roll(ext, HALO - 1, 0)
        pdm = cur + mu_ref[...] * (0.5 * nb[8:8 + CHUNK] - cur)
        r = pdm[:, :GROUP_W]
        k = pdm[:, GROUP_W:2 * GROUP_W]
        v = pdm[:, 2 * GROUP_W:3 * GROUP_W]
        dw = pdm[:, 3 * GROUP_W:3 * GROUP_W + 128]
        da = pdm[:, 3 * GROUP_W + 128:3 * GROUP_W + 256]
        a_z = lora_a(da, z)
        w_pre = w0_ref[z:z + 1, :] + bmm(jnp.tanh(dw[:, z * RWKV_LORA:(z + 1) * RWKV_LORA]), w2_ref[z])
        lw = -jnp.exp(-softplus(-w_pre) - 0.5)
        kt_z = k * (1.0 + (a_z - 1.0) * ka)
        tot = jnp.sum(lw, axis=0, keepdims=True)
        fin = ()
        if not first:
            kt_o = k * (1.0 + (lora_a(da, 1 - z) - 1.0) * ka)
            fin = (pdm[:, 3 * GROUP_W + 256:], r * (kt_z + kt_o) * rk, v)
        return (r, k * kk, kt_z, a_z, v, cumsum_col(mcs[z], lw), lw, tot), (v, jnp.exp(tot)), fin

    def finish(tot_h, dg, rkk, v):
        gate = bmm(sigmoid(dg), g2)
        return _bf(gate * (head_norm(tot_h, gain, bd, True) + head_sum(rkk, bd) * v))

    _scan(nc, g, ns, prep, rwkv_pre, 3, rwkv_step, [sbd], hscr, finish, y_ref)
    _store_state(s1_ref, sbd, ns)


def mod_kernel(c_ref, w_ref, b_ref, o_ref):
    o_ref[0] = bmm(silu(c_ref[...]), w_ref[0]) + b_ref[0]


def _rows(is_ctx, ctx_ref, lat_ref):
    return jnp.where(is_ctx, ctx_ref[...], lat_ref[...])


def proj_kernel(xp_ref, xs_ref, mod_ref, wa, wb, wc, wd, wg, wgt, pa, pb, pc, pd, pg, pgt, *, n_ptiles):
    m = mod_ref[0]
    x = _rows(pl.program_id(0) < n_ptiles, xp_ref, xs_ref)
    u = layer_norm(x) * (1.0 + m[1:2]) + m[0:1]
    ub = u.astype(BF)
    pa[...] = _dot(ub, wa[...], _NN)
    pb[...] = _dot(ub, wb[...], _NN)
    pc[...] = _dot(ub, wc[...], _NN)
    pd[...] = _dot(ub, wd[...], _NN)
    pg[...] = _dot(ub, wg[...], _NN)
    gt = _dot(wgt[...], ub, _NT)
    for j in range(ROW_TILE // CHUNK):
        pgt[j] = gt[:, j * CHUNK:(j + 1) * CHUNK]


def post_kernel(hp, hs, yap, yas, ybp, ybs, ycp, ycs, ydp, yds, mod_ref, wout, l1g, l1b, w1, w2, l2g, l2b,
                *o_refs, n_ptiles):
    m = mod_ref[0]
    is_ctx = pl.program_id(0) < n_ptiles
    y = jnp.concatenate([_rows(is_ctx, yap, yas), _rows(is_ctx, ybp, ybs), _rows(is_ctx, ycp, ycs),
                         _rows(is_ctx, ydp, yds)], axis=1)
    attn = _dot(y, wout[...], _NN)
    h1 = layer_norm(DEEPNORM_ALPHA * _rows(is_ctx, hp, hs) + m[2:3] * attn) * l1g[...] + l1b[...]
    u2 = layer_norm(h1) * (1.0 + m[4:5]) + m[3:4]
    f = _dot(u2.astype(BF), w1[...], _NN)
    f = jnp.square(jnp.maximum(f, 0.0))
    f2 = _dot(f.astype(BF), w2[...], _NN)
    out = layer_norm(DEEPNORM_ALPHA * h1 + m[5:6] * f2) * l2g[...] + l2b[...]
    if len(o_refs) == 1:
        o_refs[0][...] = out
    else:
        @pl.when(is_ctx)
        def _():
            o_refs[0][...] = out

        @pl.when(jnp.logical_not(is_ctx))
        def _():
            o_refs[1][...] = out


def _params(n_grid_axes):
    return pltpu.CompilerParams(dimension_semantics=("arbitrary",) * n_grid_axes,
                                vmem_limit_bytes=VMEM_LIMIT)


def _const_spec(a, layer=None):
    nd = a.ndim
    if layer is None:
        return pl.BlockSpec(a.shape, lambda *_: (0,) * nd, pipeline_mode=pl.Buffered(1))
    return pl.BlockSpec((None,) + a.shape[1:], lambda *_: (layer,) + (0,) * (nd - 1), pipeline_mode=pl.Buffered(1))


def _f32(shape):
    return jax.ShapeDtypeStruct(shape, jnp.float32)


def _mod_call(c_all, w_mod, b_mod):
    depth, d, n = w_mod.shape
    tn = 2048
    return pl.pallas_call(
        mod_kernel, grid=(depth, n // tn),
        in_specs=[pl.BlockSpec(c_all.shape, lambda l, j: (0, 0)),
                  pl.BlockSpec((1, d, tn), lambda l, j: (l, 0, j)),
                  pl.BlockSpec((1, 1, tn), lambda l, j: (l, 0, j))],
        out_specs=pl.BlockSpec((1, c_all.shape[0], tn), lambda l, j: (l, 0, j)),
        out_shape=_f32((depth, c_all.shape[0], n)),
        compiler_params=_params(2), name="mod")(c_all, w_mod, b_mod.reshape(depth, 1, n))


def _mod_spec(layer, n_ptiles, tiles_per_dec):
    def index(i):
        return (layer, jnp.where(i < n_ptiles, 0, 1 + (i - n_ptiles) // tiles_per_dec), 0, 0)
    return pl.BlockSpec((None, 1, 6, D_MODEL), index)


def _pair_specs(pair, n_ptiles):
    width = pair[0].shape[1]
    off = n_ptiles if pair[0] is pair[1] else 0
    return [pl.BlockSpec((ROW_TILE, width), lambda i: (jnp.minimum(i, n_ptiles - 1), 0)),
            pl.BlockSpec((ROW_TILE, width), lambda i: (jnp.maximum(i - n_ptiles, 0) + off, 0))]


def _proj_call(x_pair, n_rows, mod, layer, ws, n_ptiles, tiles_per_dec):
    row = lambda i: (i, 0)
    widths = [w.shape[-1] for w in ws[:5]]
    per_tile = ROW_TILE // CHUNK
    return pl.pallas_call(
        functools.partial(proj_kernel, n_ptiles=n_ptiles), grid=(n_rows // ROW_TILE,),
        in_specs=_pair_specs(x_pair, n_ptiles) + [_mod_spec(layer, n_ptiles, tiles_per_dec)]
                 + [_const_spec(w, layer) for w in ws],
        out_specs=[pl.BlockSpec((ROW_TILE, w), row) for w in widths]
                  + [pl.BlockSpec((per_tile, N_GATES, CHUNK), lambda i: (i, 0, 0))],
        out_shape=[_f32((n_rows, w)) for w in widths] + [_f32((n_rows // CHUNK, N_GATES, CHUNK))],
        compiler_params=_params(1), name="proj")(*x_pair, mod, *ws)


def _post_call(h_pair, y_pairs, n_rows, mod, layer, ws, n_ptiles, tiles_per_dec, split_out):
    pairs = [h_pair] + y_pairs
    if split_out:
        n_prows = n_ptiles * ROW_TILE
        out_specs = _pair_specs((_f32((n_prows, D_MODEL)), _f32((n_rows - n_prows, D_MODEL))), n_ptiles)
        out_shape = [_f32((n_prows, D_MODEL)), _f32((n_rows - n_prows, D_MODEL))]
    else:
        out_specs = pl.BlockSpec((ROW_TILE, D_MODEL), lambda i: (i, 0))
        out_shape = _f32((n_rows, D_MODEL))
    return pl.pallas_call(
        functools.partial(post_kernel, n_ptiles=n_ptiles), grid=(n_rows // ROW_TILE,),
        in_specs=[spec for p in pairs for spec in _pair_specs(p, n_ptiles)]
                 + [_mod_spec(layer, n_ptiles, tiles_per_dec)] + [_const_spec(w, layer) for w in ws],
        out_specs=out_specs, out_shape=out_shape,
        compiler_params=_params(1), name="post")(*[a for p in pairs for a in p], mod, *ws)


MAT = (2, N_HEADS, HEAD_DIM, HEAD_DIM)
VEC = (2, N_HEADS, 1, HEAD_DIM)
SCL = (2, N_HEADS, 1, 1)


def _mixer_call(kernel, name, t_len, n_seq, row_off, seq_inputs, const_inputs, state_inputs, layer,
                state_tails, scratch):
    nc = t_len // CHUNK
    g = _scan_steps(nc)
    ns = max(1, min(n_seq, PRE_BATCH // (2 * g)))
    assert n_seq % ns == 0 and row_off % (ns * t_len) == 0
    blk_off = row_off // (ns * t_len)
    in_specs = []
    for a in seq_inputs:
        if a.ndim == 2:
            in_specs.append(pl.BlockSpec((ns * t_len, a.shape[1]), lambda i: (i + blk_off, 0)))
        else:
            in_specs.append(pl.BlockSpec((ns * nc,) + a.shape[1:], lambda i: (i + blk_off, 0, 0)))
    in_specs += [_const_spec(a, lay) for a, lay in const_inputs]
    for a in state_inputs:
        tail = a.shape[2:]
        in_specs.append(pl.BlockSpec((ns, 1) + tail, lambda i, nd=len(tail): (i, layer) + (0,) * nd))
    out_specs = [pl.BlockSpec((ns * t_len, GROUP_W), lambda i: (i, 0))]
    out_specs += [pl.BlockSpec((ns, 1) + t, lambda i, nd=len(t): (i, 0) + (0,) * nd) for t in state_tails]
    out_shape = [jax.ShapeDtypeStruct((n_seq * t_len, GROUP_W), BF)] + [_f32((n_seq, 1) + t) for t in state_tails]
    return pl.pallas_call(
        functools.partial(kernel, nc=nc, g=g, ns=ns), grid=(n_seq // ns,),
        in_specs=in_specs, out_specs=out_specs, out_shape=out_shape,
        scratch_shapes=[pltpu.VMEM((ns * t_len, GROUP_W), jnp.float32)] + scratch(ns),
        compiler_params=_params(1), name=name)(*seq_inputs, *[a for a, _ in const_inputs], *state_inputs)


def _bd_state(ns):
    return pltpu.VMEM((2 * ns, GROUP_W, GROUP_W), jnp.float32)


def _rope_tables(t_len):
    nf = HEAD_DIM // 4
    inv = ROPE_BASE ** (-jnp.arange(nf, dtype=jnp.float32) / nf)
    t = jnp.arange(t_len)
    ang_r = (t // GRID_W).astype(jnp.float32)[:, None] * inv[None, :]
    ang_c = (t % GRID_W).astype(jnp.float32)[:, None] * inv[None, :]
    cos = jnp.concatenate([jnp.cos(ang_r)] * 2 + [jnp.cos(ang_c)] * 2, axis=1)
    sin = jnp.concatenate([-jnp.sin(ang_r), jnp.sin(ang_r), -jnp.sin(ang_c), jnp.sin(ang_c)], axis=1)
    return jnp.tile(cos, (1, N_HEADS)), jnp.tile(sin, (1, N_HEADS))


def kernel(x_prompt, x_sample, c, state_mlstm_C, state_mlstm_n, state_mlstm_m, state_delta, state_ret,
           state_rwkv, c_ctx, w_mod, b_mod, w_in, w_out, ln1_g, ln1_b, ln2_g, ln2_b, w_ff1, w_ff2,
           mlstm_i_bias, mlstm_f_bias, mlstm_norm, delta_conv, delta_a_log, delta_dt_bias, delta_norm,
           ret_decay, ret_norm, rwkv_mu, rwkv_w0, rwkv_w2, rwkv_a0, rwkv_a2, rwkv_g2, rwkv_kk, rwkv_ka,
           rwkv_rk, rwkv_norm):
    f32 = jnp.float32
    batch, seq, d = x_prompt.shape
    dec_batch, dec_seq, _ = x_sample.shape
    assert d == D_MODEL and (batch * seq) % ROW_TILE == 0 and dec_seq % ROW_TILE == 0 and seq % CHUNK == 0
    assert (seq // CHUNK) % 2 == 0 and (dec_seq // CHUNK) % 2 == 0
    n_prows = batch * seq
    n_ptiles = n_prows // ROW_TILE
    tiles_per_dec = dec_seq // ROW_TILE

    c_all = jnp.zeros((8, d), f32).at[0].set(c_ctx).at[1:1 + dec_batch].set(c)
    mod = _mod_call(c_all, w_mod, b_mod).reshape(DEPTH, 8, 6, d)
    n_rows = n_prows + dec_batch * dec_seq
    h_pair = (x_prompt.reshape(n_prows, d), x_sample.reshape(dec_batch * dec_seq, d))
    cos_t, sin_t = _rope_tables(dec_seq)

    zmat = jnp.zeros((batch, 1) + MAT, f32)
    zn = jnp.zeros((batch, 1) + VEC, f32)
    zm = jnp.zeros((batch, 1) + SCL, f32)
    n_cache = state_mlstm_n.reshape(state_mlstm_n.shape[:-1] + (1, HEAD_DIM))
    m_cache = state_mlstm_m.reshape(state_mlstm_m.shape + (1, 1))
    groups = ((seq, batch, 0), (dec_seq, dec_batch, n_prows))

    bf = lambda a: a.astype(BF)
    rows = lambda a: a.reshape(DEPTH, 1, -1)
    zeros = lambda *shape: jnp.zeros((DEPTH,) + shape, f32)
    wg = bf(jnp.pad(jnp.concatenate([w_in[:, :, 1024:1040], w_in[:, :, 2064:2080]], axis=2),
                    ((0, 0), (0, 0), (0, GATE_LANES - N_GATES))))
    proj_ws = [bf(w_in[:, :, 0:1024]), bf(w_in[:, :, 1040:2064]), bf(w_in[:, :, 2080:3104]), bf(w_in[:, :, 3104:4256]),
               wg, jnp.swapaxes(wg[:, :, :N_GATES], 1, 2)]
    post_ws = [bf(w_out), rows(ln1_g), rows(ln1_b), bf(w_ff1), bf(w_ff2), rows(ln2_g), rows(ln2_b)]
    bias_c = jnp.concatenate([rows(mlstm_i_bias), rows(mlstm_f_bias), zeros(1, GATE_LANES - 16)], axis=2)
    bias_r = jnp.swapaxes(bias_c[:, :, :N_GATES], 1, 2)
    dl_row = jnp.concatenate([zeros(2, 24), jnp.concatenate([rows(delta_a_log), rows(delta_dt_bias)], axis=1),
                              zeros(2, GATE_LANES - 32)], axis=2)
    dl_col = jnp.swapaxes(dl_row[:, :, :N_GATES], 1, 2)
    rd_row = jnp.concatenate([rows(ret_decay), zeros(1, GATE_LANES - 8)], axis=2)
    rw_vec = jnp.concatenate([rows(rwkv_kk), rows(rwkv_ka), rows(rwkv_rk), rows(rwkv_norm)], axis=1)

    new_states = []
    for l in range(DEPTH):
        lay_of = lambda *arrs: [(a, l) for a in arrs]
        pa, pb, pc, pd, pg, pgt = _proj_call(h_pair, n_rows, mod, l, proj_ws, n_ptiles, tiles_per_dec)
        ys = [[], [], [], []]
        layer_states = None
        for gi, (t_len, n_seq, off) in enumerate(groups):
            ctx = gi == 0
            lay = 0 if ctx else l
            ya, c1, n1, m1 = _mixer_call(
                mlstm_kernel, "mlstm", t_len, n_seq, off, [pa, pg, pgt],
                lay_of(bias_c, bias_r, rows(mlstm_norm)),
                [zmat, zn, zm] if ctx else [state_mlstm_C, n_cache, m_cache], lay, [MAT, VEC, SCL],
                lambda ns: [_bd_state(ns), pltpu.VMEM((2 * ns, 1, GROUP_W), f32), pltpu.VMEM((2 * ns, 1, GROUP_W), f32)])
            yb, sb1 = _mixer_call(
                delta_kernel, "delta", t_len, n_seq, off, [pb, pg, pgt],
                lay_of(delta_conv, dl_row, dl_col, rows(delta_norm)),
                [zmat if ctx else state_delta], lay, [MAT], lambda ns: [_bd_state(ns)])
            ret_consts = lay_of(rd_row, rows(ret_norm))
            if not ctx:
                ret_consts = [(cos_t, None), (sin_t, None)] + ret_consts
            yc, sc1 = _mixer_call(
                functools.partial(ret_kernel, rope=not ctx), "ret", t_len, n_seq, off, [pc],
                ret_consts, [zmat if ctx else state_ret], lay, [MAT],
                lambda ns: [_bd_state(ns), pltpu.VMEM((2, 4, CHUNK, GROUP_W), f32)])
            yd, sd1 = _mixer_call(
                rwkv_kernel, "rwkv", t_len, n_seq, off, [pd],
                lay_of(rows(rwkv_mu), rwkv_w0, rwkv_w2, rwkv_a0, rwkv_a2, rwkv_g2, rw_vec),
                [zmat if ctx else state_rwkv], lay, [MAT], lambda ns: [_bd_state(ns)])
            for lst, y in zip(ys, (ya, yb, yc, yd)):
                lst.append(y)
            if ctx:
                layer_states = (c1, n1, m1, sb1, sc1, sd1)
        new_states.append(layer_states)
        last = l == DEPTH - 1
        h_new = _post_call(h_pair, [tuple(y) for y in ys], n_rows, mod, l, post_ws, n_ptiles, tiles_per_dec, last)
        h_pair = tuple(h_new) if last else (h_new, h_new)

    outs = [jnp.concatenate([st[i] for st in new_states], axis=1) for i in range(6)]
    outs[1] = outs[1].reshape(outs[1].shape[:4] + (HEAD_DIM,))
    outs[2] = outs[2].reshape(outs[2].shape[:4])
    return (h_pair[0].reshape(batch, seq, d), h_pair[1].reshape(dec_batch, dec_seq, d), *outs)
```

```python
import functools

import jax
import jax.numpy as jnp
from jax import lax
from jax.experimental import pallas as pl
from jax.experimental.pallas import tpu as pltpu

D_MODEL = 1024
DEPTH = 2
GRID_W = 64
HEAD_DIM = 64
N_HEADS = 4
GROUP_W = N_HEADS * HEAD_DIM
CHUNK = 64
CONV_K = 5
ROPE_BASE = 10000.0
LN_EPS = 1e-5
DEEPNORM_ALPHA = (2 * DEPTH) ** 0.25
RWKV_LORA = 64
GATE_LANES = 128
N_GATES = 32
ROW_TILE = 512
VMEM_LIMIT = 56 * 1024 * 1024
MAX_SCAN_STEPS = 2
PRE_BATCH = 8
HALO = CHUNK + 16

NEG_INF = float("-inf")
BF = jnp.bfloat16
_NN = (((1,), (0,)), ((), ()))
_NT = (((1,), (1,)), ((), ()))
_TN = (((0,), (0,)), ((), ()))


def _dot(a, b, dims):
    return lax.dot_general(a, b, dims, preferred_element_type=jnp.result_type(a.dtype, jnp.float32))


def _bf(x):
    return x.astype(BF)


def _split(x, n):
    parts = []
    for _ in range(n - 1):
        p = _bf(x)
        parts.append(p)
        x = x - p.astype(x.dtype)
    parts.append(_bf(x))
    return parts


def bmm(a, b, dims=_NN):
    return _dot(_bf(a), _bf(b), dims)


def _expand(x, bdm):
    lane_head = jnp.right_shift(lax.broadcasted_iota(jnp.int32, (CHUNK, GROUP_W), 1), 6)
    zero = jnp.zeros((), x.dtype)
    return jnp.concatenate([jnp.where(lane_head == h, x, zero) for h in range(N_HEADS)], axis=0)


def wmm(a, b, bdm, passes=1, nt=False):
    dims = _NT if nt else _NN
    if passes == 1:
        return _dot(_bf(a), _expand(_bf(b), bdm), dims)
    ah, al = _split(a, 2)
    bh, bl = _split(b, 2)
    ebh = _expand(bh, bdm)
    lhs = jnp.concatenate([ah, ah, al], axis=1)
    rhs = jnp.concatenate([ebh, _expand(bl, bdm), ebh], axis=1 if nt else 0)
    return _dot(lhs, rhs, dims)


def cumsum_col(mc, x):
    return _dot(jnp.concatenate([_bf(mc)] * 3, axis=1), jnp.concatenate(_split(x, 3), axis=0), _NN)


def cumsum_row(x, mc):
    return _dot(jnp.concatenate(_split(x, 3), axis=1), jnp.concatenate([_bf(mc)] * 3, axis=1), _NT)


def head_sum(x, bd):
    return _dot(jnp.concatenate(_split(x, 2), axis=1), jnp.concatenate([_bf(bd)] * 2, axis=0), _NN)


def sigmoid(x):
    return 0.5 + 0.5 * jnp.tanh(0.5 * x)


def silu(x):
    return x * sigmoid(x)


def softplus(x):
    return jnp.maximum(x, 0.0) + jnp.log(1.0 + jnp.exp(-jnp.abs(x)))


def log_sigmoid(x):
    return -softplus(-x)


def head_norm(y, gain, bd, center):
    if center:
        y = y - head_sum(y, bd) * (1.0 / HEAD_DIM)
    ms = head_sum(y * y, bd) * (1.0 / HEAD_DIM)
    return y * lax.rsqrt(ms + LN_EPS) * gain


def l2norm_heads(x, bd):
    return x * lax.rsqrt(head_sum(x * x, bd) + 1e-6)


def layer_norm(x):
    mu = jnp.mean(x, axis=-1, keepdims=True)
    xc = x - mu
    var = jnp.mean(xc * xc, axis=-1, keepdims=True)
    return xc * lax.rsqrt(var + LN_EPS)


def colw(x, idx):
    m = x.shape[0]
    return jnp.concatenate([jnp.broadcast_to(x[:, j:j + 1], (m, HEAD_DIM)) for j in idx], axis=1)


def roww(x, idx):
    return jnp.concatenate([x[j:j + 1, :] for j in idx], axis=1)


def segmax(x):
    m = x.shape[0]
    return jnp.concatenate(
        [jnp.broadcast_to(jnp.max(x[:, h * HEAD_DIM:(h + 1) * HEAD_DIM], axis=1, keepdims=True), (m, HEAD_DIM))
         for h in range(N_HEADS)], axis=1)


def wide_masks(rev, dtype=jnp.float32):
    r = lax.broadcasted_iota(jnp.int32, (CHUNK, GROUP_W), 0)
    c = jnp.bitwise_and(lax.broadcasted_iota(jnp.int32, (CHUNK, GROUP_W), 1), HEAD_DIM - 1)
    if rev:
        incl_w, strict_w = r <= c, r < c
    else:
        incl_w, strict_w = r >= c, r > c
    mc = incl_w[:, :CHUNK].astype(dtype)
    return incl_w, strict_w, mc, (r == c).astype(dtype)


def block_diag_mask():
    r = jnp.right_shift(lax.broadcasted_iota(jnp.int32, (GROUP_W, GROUP_W), 0), 6)
    c = jnp.right_shift(lax.broadcasted_iota(jnp.int32, (GROUP_W, GROUP_W), 1), 6)
    m = (r == c).astype(jnp.float32)
    return _bf(m), m


def inv_unit_tri_w(a, eye_w, bdm):
    p = _bf(-a)
    e = _expand(p, bdm)
    t = eye_w - a
    for _ in range(5):
        p = _bf(_dot(p, e, _NN))
        e = _expand(p, bdm)
        t = t + _dot(_bf(t), e, _NN)
    r = eye_w - t - wmm(a, t, bdm, 3)
    return t + wmm(t, r, bdm, 1)


def mlstm_pre(q, k, v, bcol_w, brow_w, icol_w, irow_w, bl_w, incl_w, bdm):
    logd = jnp.where(incl_w, bcol_w - brow_w + irow_w, NEG_INF)
    mt0 = segmax(logd)
    s0 = wmm(q, k, bdm, 1, nt=True) * jnp.exp(logd - mt0)
    wk = bl_w - bcol_w + icol_w
    wkmax = jnp.max(wk, axis=0, keepdims=True)
    ks0 = k * jnp.exp(wk - wkmax)
    return (q, bcol_w, bl_w, mt0, wmm(s0, v, bdm, 1), head_sum(s0, bdm[1]), wkmax,
            _bf(ks0), _bf(v), jnp.sum(ks0, axis=0, keepdims=True))


def mlstm_step(q, bcol_w, bl_w, mt0, num0, den0, wkmax, ks0, v, ksum0, c_bd, n_row, m_w, bdm):
    inter = bcol_w + m_w
    mt = jnp.maximum(inter, mt0)
    f = jnp.exp(mt0 - mt)
    ew = jnp.exp(inter - mt)
    num = num0 * f + bmm(q, c_bd) * ew
    den = den0 * f + head_sum(q * n_row, bdm[1]) * ew
    h = num / jnp.maximum(jnp.abs(den), jnp.exp(-mt))
    m_new = jnp.maximum(bl_w + m_w, wkmax)
    sp = jnp.exp(bl_w + m_w - m_new)
    sc = jnp.exp(wkmax - m_new)
    kv0 = bmm(ks0, v, _TN) * bdm[1]
    return h, c_bd * sp + kv0 * sc, n_row * sp + ksum0 * sc, m_new


def delta_pre(q_raw, k_raw, v, beta_w, gcol_w, grow_w, gl_w, incl_w, strict_w, eye_w, bdm):
    q = l2norm_heads(q_raw, bdm[1]) * HEAD_DIM ** -0.5
    k = l2norm_heads(k_raw, bdm[1])
    decay = jnp.exp(jnp.where(incl_w, gcol_w - grow_w, NEG_INF))
    a = jnp.where(strict_w, wmm(k, k, bdm, 1, nt=True) * decay, 0.0) * beta_w
    t = inv_unit_tri_w(a, eye_w, bdm)
    eg = jnp.exp(gcol_w)
    u = wmm(t, v * beta_w, bdm, 1)
    w = wmm(t, k * (beta_w * eg), bdm, 1)
    qk = wmm(q, k, bdm, 1, nt=True) * decay
    return (u, _bf(jnp.concatenate([w, q * eg], axis=0)), _bf(qk), _bf(k * jnp.exp(gl_w - gcol_w)),
            jnp.exp(gl_w))


def delta_step(u, wq, qk, kd, egl, s_bd, bdm):
    ws = bmm(wq, s_bd)
    v_new = u - ws[:CHUNK]
    o = ws[CHUNK:] + wmm(qk, v_new, bdm, 1)
    s_new = s_bd * egl + bmm(kd, v_new, _TN) * bdm[1]
    return o, s_new


def ret_pre(q, k, v, dmat_w, zeta_w, bdm):
    sc = wmm(q, k, bdm, 1, nt=True) * dmat_w
    return _bf(q), wmm(sc, v, bdm, 1), _bf(k * zeta_w), _bf(v)


def ret_step(q, o1, kz, v, xi_w, gl_w, s_bd, bdm):
    return o1 + bmm(q, s_bd) * xi_w, s_bd * gl_w + bmm(kz, v, _TN) * bdm[1]


def rwkv_pre(r, kkk, kt, a, v, cw, lw, tot, incl_w, strict_w, eye_w, bdm):
    kh = l2norm_heads(kkk, bdm[1])
    b = a * kh
    en = jnp.exp(-cw)
    et = jnp.exp(tot - cw)
    rp = r * jnp.exp(cw)
    khp = kh * jnp.exp(cw - lw)
    ktn = kt * en
    bn = b * en
    lhs = jnp.concatenate([khp, rp], axis=0)
    ab_kt = wmm(lhs, ktn, bdm, 1, nt=True)
    ab_b = wmm(lhs, bn, bdm, 1, nt=True)
    a_kt = jnp.where(strict_w, ab_kt[:CHUNK], 0.0)
    b_kt = jnp.where(incl_w, ab_kt[CHUNK:], 0.0)
    a_b = jnp.where(strict_w, ab_b[:CHUNK], 0.0)
    b_b = jnp.where(incl_w, ab_b[CHUNK:], 0.0)
    t = inv_unit_tri_w(a_b, eye_w, bdm)
    tk = wmm(t, khp, bdm, 1)
    tav = wmm(t, wmm(a_kt, v, bdm, 1), bdm, 1)
    return (_bf(jnp.concatenate([tk, rp], axis=0)), tav, wmm(b_kt, v, bdm, 1), _bf(b_b),
            _bf(jnp.concatenate([kt * et, b * et], axis=0)))


def rwkv_step(tkr, tav, o1, b_b, kbe, v, etot, s_bd, bdm):
    ls = bmm(tkr, s_bd, _NT)
    u = ls[:CHUNK] + tav
    o = ls[CHUNK:] + o1 - wmm(b_b, u, bdm, 1)
    upd = bmm(jnp.concatenate([v, -u], axis=0), kbe, _TN)
    s_new = s_bd * etot + upd * bdm[1]
    return o, s_new


def _group_loop(nc, g, body):
    n_it = nc // g

    def run(first):
        def it(i, carry):
            body(i, first)
            return carry
        return it

    lax.fori_loop(0, n_it // 2, run(True), 0)
    lax.fori_loop(n_it // 2, n_it, run(False), 0)


def _scan_steps(nc):
    return max(g for g in (1, 2, 4, 8) if g <= MAX_SCAN_STEPS and nc % (2 * g) == 0)


def _group_chunks(i, g, nc, ns):
    out = []
    for s in range(ns):
        for z in range(2):
            for j in range(g):
                step = i * g + j
                c = s * nc + (step if z == 0 else nc - 1 - step)
                out.append((s, z, c, pl.multiple_of(c * CHUNK, CHUNK)))
    return out


def _group_masks(g, ns):
    shape = (2 * g * ns, CHUNK, GROUP_W)
    lane = jnp.right_shift(lax.broadcasted_iota(jnp.int32, shape, 0), g.bit_length() - 1)
    r = lax.broadcasted_iota(jnp.int32, shape, 1)
    c = jnp.bitwise_and(lax.broadcasted_iota(jnp.int32, shape, 2), HEAD_DIM - 1)
    d = jnp.where(jnp.bitwise_and(lane, 1) == 1, c - r, r - c)
    return d >= 0, d > 0, (d == 0).astype(jnp.float32)


def _stack(items):
    return [jnp.stack(x) for x in zip(*items)]


def _pick(arrs, g, j):
    return [jnp.stack([a[k] for k in range(j, a.shape[0], g)]) for a in arrs]


def _halo(ref, s, c, t0, nc):
    rows = ref.shape[0]
    ps = pl.multiple_of(jnp.maximum(t0 - 8, 0), 8)
    nx = pl.multiple_of(jnp.minimum(t0 + CHUNK, rows - 8), 8)
    prev = jnp.where(c > s * nc, ref[pl.ds(ps, 8), :], 0.0)
    nxt = jnp.where(c < s * nc + nc - 1, ref[pl.ds(nx, 8), :], 0.0)
    return jnp.concatenate([prev, ref[pl.ds(t0, CHUNK), :], nxt], axis=0)


def _shift_matrix(offsets):
    rows = len(offsets) * CHUNK
    r = lax.broadcasted_iota(jnp.int32, (rows, 2 * HALO), 0)
    c = lax.broadcasted_iota(jnp.int32, (rows, 2 * HALO), 1)
    c = jnp.where(c >= HALO, c - HALO, c)
    t = jnp.bitwise_and(r, CHUNK - 1)
    target = t + 8
    for j, off in enumerate(offsets):
        target = jnp.where(jnp.right_shift(r, 6) == j, target + off, target)
    return _bf((c == target).astype(jnp.float32))


def _hs(h):
    return slice(h * HEAD_DIM, (h + 1) * HEAD_DIM)


def _load_state(sbd, s0_ref, ns):
    for s in range(ns):
        for z in range(2):
            sbd[2 * s + z] = jnp.zeros((GROUP_W, GROUP_W), jnp.float32)
            for h in range(N_HEADS):
                sbd[2 * s + z, _hs(h), _hs(h)] = s0_ref[s, 0, z, h]


def _store_state(s1_ref, sbd, ns):
    for s in range(ns):
        for z in range(2):
            for h in range(N_HEADS):
                s1_ref[s, 0, z, h] = sbd[2 * s + z, _hs(h), _hs(h)]


def _merge(first, hscr, t0, h):
    if first:
        hscr[pl.ds(t0, CHUNK), :] = h
        return None
    return hscr[pl.ds(t0, CHUNK), :] + h


def _scan(nc, g, ns, prep, pre_fn, n_masks, step_fn, state_refs, hscr, finish, y_ref):
    bdm = block_diag_mask()
    masks = _group_masks(g, ns)[:n_masks]

    def body(i, first):
        chunks = _group_chunks(i, g, nc, ns)
        preps = [prep(s, z, c, t0, first) for s, z, c, t0 in chunks]
        n_in = len(preps[0][0])
        pre = jax.vmap(pre_fn, in_axes=(0,) * (n_in + n_masks) + (None,))(
            *_stack([p[0] for p in preps]), *masks, bdm)
        extra = _stack([p[1] for p in preps]) if preps[0][1] else []
        states = [ref[...] for ref in state_refs]
        for j in range(g):
            args = _pick(list(pre), g, j) + _pick(extra, g, j) + states
            res = jax.vmap(step_fn, in_axes=(0,) * len(args) + (None,))(*args, bdm)
            states = list(res[1:])
            ks = [lane * g + j for lane in range(2 * ns)]
            tots = [_merge(first, hscr, chunks[k][3], res[0][lane]) for lane, k in enumerate(ks)]
            if not first:
                ys = jax.vmap(finish)(jnp.stack(tots), *_stack([preps[k][2] for k in ks]))
                for lane, k in enumerate(ks):
                    y_ref[pl.ds(chunks[k][3], CHUNK), :] = ys[lane]
        for ref, val in zip(state_refs, states):
            ref[...] = val

    _group_loop(nc, g, body)


def mlstm_kernel(p_ref, gc_ref, gr_ref, bc_ref, br_ref, norm_ref, c0_ref, n0_ref, m0_ref,
                 y_ref, c1_ref, n1_ref, m1_ref, hscr, cbd, nrow, mw, *, nc, g, ns):
    _load_state(cbd, c0_ref, ns)
    for s in range(ns):
        for z in range(2):
            for h in range(N_HEADS):
                nrow[2 * s + z, :, _hs(h)] = n0_ref[s, 0, z, h]
                mw[2 * s + z, :, _hs(h)] = jnp.broadcast_to(m0_ref[s, 0, z, h], (1, HEAD_DIM))
    bd = block_diag_mask()[1]
    mcs = [wide_masks(False)[2], wide_masks(True)[2]]
    gain = norm_ref[...]

    def prep(s, z, c, t0, first):
        x = p_ref[pl.ds(t0, CHUNK), 0:3 * GROUP_W]
        g_c = gc_ref[pl.ds(t0, CHUNK), :] + bc_ref[...]
        g_r = gr_ref[c] + br_ref[...]
        lf_c = log_sigmoid(g_c)
        idx_i = [z * N_HEADS + h for h in range(N_HEADS)]
        idx_f = [8 + j for j in idx_i]
        fin = () if first else (p_ref[pl.ds(t0, CHUNK), 3 * GROUP_W:],)
        return ((x[:, :GROUP_W], x[:, GROUP_W:2 * GROUP_W] * HEAD_DIM ** -0.5, x[:, 2 * GROUP_W:],
                 colw(cumsum_col(mcs[z], lf_c), idx_f), roww(cumsum_row(log_sigmoid(g_r), mcs[z]), idx_f),
                 colw(g_c, idx_i), roww(g_r, idx_i), colw(jnp.sum(lf_c, axis=0, keepdims=True), idx_f)), (), fin)

    def finish(tot, og):
        return _bf(sigmoid(og) * head_norm(tot, gain, bd, True))

    _scan(nc, g, ns, prep, mlstm_pre, 1, mlstm_step, [cbd, nrow, mw], hscr, finish, y_ref)
    _store_state(c1_ref, cbd, ns)
    for s in range(ns):
        for z in range(2):
            for h in range(N_HEADS):
                n1_ref[s, 0, z, h] = nrow[2 * s + z, :, _hs(h)]
                m1_ref[s, 0, z, h] = mw[2 * s + z, :, h * HEAD_DIM:h * HEAD_DIM + 1]


def delta_kernel(p_ref, gc_ref, gr_ref, conv_ref, prow_ref, pcol_ref, norm_ref, s0_ref,
                 y_ref, s1_ref, hscr, sbd, *, nc, g, ns):
    _load_state(sbd, s0_ref, ns)
    bd = block_diag_mask()[1]
    mcs = [wide_masks(False)[2], wide_masks(True)[2]]
    nega_row = -jnp.exp(prow_ref[0:1, :])
    dtb_row = prow_ref[1:2, :]
    nega_col = -jnp.exp(pcol_ref[:, 0:1])
    dtb_col = pcol_ref[:, 1:2]
    qkv_w = 3 * GROUP_W
    gain = norm_ref[...]
    taps = _shift_matrix([j - CONV_K // 2 for j in range(CONV_K)])

    def prep(s, z, c, t0, first):
        ext = _halo(p_ref.at[:, 0:qkv_w], s, c, t0, nc)
        shifted = _dot(taps, jnp.concatenate(_split(ext, 2), axis=0), _NN)
        acc = shifted[0:CHUNK] * conv_ref[0:1, :]
        for j in range(1, CONV_K):
            acc = acc + shifted[j * CHUNK:(j + 1) * CHUNK] * conv_ref[j:j + 1, :]
        qkv = silu(acc)
        g_c = gc_ref[pl.ds(t0, CHUNK), :]
        gg_c = nega_row * softplus(g_c + dtb_row)
        gg_r = nega_col * softplus(gr_ref[c] + dtb_col)
        idx_b = [16 + z * N_HEADS + h for h in range(N_HEADS)]
        idx_a = [8 + j for j in idx_b]
        fin = () if first else (p_ref[pl.ds(t0, CHUNK), qkv_w:],)
        return ((qkv[:, :GROUP_W], qkv[:, GROUP_W:2 * GROUP_W], qkv[:, 2 * GROUP_W:], colw(sigmoid(g_c), idx_b),
                 colw(cumsum_col(mcs[z], gg_c), idx_a), roww(cumsum_row(gg_r, mcs[z]), idx_a),
                 colw(jnp.sum(gg_c, axis=0, keepdims=True), idx_a)), (), fin)

    def finish(tot, zg):
        return _bf(head_norm(tot, gain, bd, False) * silu(zg))

    _scan(nc, g, ns, prep, delta_pre, 3, delta_step, [sbd], hscr, finish, y_ref)
    _store_state(s1_ref, sbd, ns)


def _rope(x, cos, sin, lane_lo):
    n = x.shape[1]
    swapped = jnp.where(lane_lo, pltpu.roll(x, n - 16, 1), pltpu.roll(x, 16, 1))
    return x * cos + swapped * sin


def ret_kernel(*refs, nc, g, ns, rope):
    if rope:
        p_ref, cos_ref, sin_ref, rd_ref, norm_ref, s0_ref, y_ref, s1_ref, hscr, sbd, cst = refs
    else:
        p_ref, rd_ref, norm_ref, s0_ref, y_ref, s1_ref, hscr, sbd, cst = refs
    f32 = jnp.float32
    _load_state(sbd, s0_ref, ns)
    bd = block_diag_mask()[1]
    r = lax.broadcasted_iota(jnp.int32, (CHUNK, GROUP_W), 0).astype(f32)
    cc = jnp.bitwise_and(lax.broadcasted_iota(jnp.int32, (CHUNK, GROUP_W), 1), HEAD_DIM - 1).astype(f32)
    lg_row = -jnp.exp(rd_ref[...])
    gain = norm_ref[...]
    for z in range(2):
        incl_w = wide_masks(z == 1)[0]
        lg = colw(lg_row, [z * N_HEADS + h for h in range(N_HEADS)])
        if z == 0:
            cst[z, 0] = jnp.exp(jnp.where(incl_w, (r - cc) * lg, NEG_INF))
            cst[z, 1] = jnp.exp((r + 1.0) * lg)
            cst[z, 2] = jnp.exp((CHUNK - 1.0 - r) * lg)
        else:
            cst[z, 0] = jnp.exp(jnp.where(incl_w, (cc - r) * lg, NEG_INF))
            cst[z, 1] = jnp.exp((CHUNK - r) * lg)
            cst[z, 2] = jnp.exp(r * lg)
        cst[z, 3] = jnp.exp(CHUNK * lg + 0.0 * r)
    if rope:
        lane = lax.broadcasted_iota(jnp.int32, (CHUNK, GROUP_W), 1)
        lane_lo = jnp.bitwise_and(lane, 31) < 16

    def prep(s, z, c, t0, first):
        x = p_ref[pl.ds(t0, CHUNK), 0:3 * GROUP_W]
        q = x[:, :GROUP_W] * HEAD_DIM ** -0.5
        k = x[:, GROUP_W:2 * GROUP_W]
        if rope:
            tr = pl.multiple_of(t0 - s * nc * CHUNK, CHUNK)
            cos = cos_ref[pl.ds(tr, CHUNK), :]
            sin = sin_ref[pl.ds(tr, CHUNK), :]
            q = _rope(q, cos, sin, lane_lo)
            k = _rope(k, cos, sin, lane_lo)
        fin = () if first else (p_ref[pl.ds(t0, CHUNK), 3 * GROUP_W:],)
        return (q, k, x[:, 2 * GROUP_W:], cst[z, 0], cst[z, 2]), (cst[z, 1], cst[z, 3, 0:1, :]), fin

    def finish(tot, gate):
        return _bf(silu(gate) * head_norm(tot, gain, bd, True))

    _scan(nc, g, ns, prep, ret_pre, 0, ret_step, [sbd], hscr, finish, y_ref)
    _store_state(s1_ref, sbd, ns)


def rwkv_kernel(p_ref, mu_ref, w0_ref, w2_ref, a0_ref, a2_ref, g2_ref, vec_ref, s0_ref,
                y_ref, s1_ref, hscr, sbd, *, nc, g, ns):
    _load_state(sbd, s0_ref, ns)
    bd = block_diag_mask()[1]
    mcs = [wide_masks(False)[2], wide_masks(True)[2]]
    kk = vec_ref[0:1, :]
    ka = vec_ref[1:2, :]
    rk = vec_ref[2:3, :]
    gain = vec_ref[3:4, :]
    g2 = g2_ref[...]

    def lora_a(da, zz):
        return sigmoid(a0_ref[zz:zz + 1, :] + bmm(da[:, zz * RWKV_LORA:(zz + 1) * RWKV_LORA], a2_ref[zz]))

    def prep(s, z, c, t0, first):
        ext = _halo(p_ref, s, c, t0, nc)
        cur = ext[8:8 + CHUNK]
        nb = pltpu.roll(ext, 1, 0) + pltpu.roll(ext, HALO - 1, 0)
        pdm = cur + mu_ref[...] * (0.5 * nb[8:8 + CHUNK] - cur)
        r = pdm[:, :GROUP_W]
        k = pdm[:, GROUP_W:2 * GROUP_W]
        v = pdm[:, 2 * GROUP_W:3 * GROUP_W]
        dw = pdm[:, 3 * GROUP_W:3 * GROUP_W + 128]
        da = pdm[:, 3 * GROUP_W + 128:3 * GROUP_W + 256]
        a_z = lora_a(da, z)
        w_pre = w0_ref[z:z + 1, :] + bmm(jnp.tanh(dw[:, z * RWKV_LORA:(z + 1) * RWKV_LORA]), w2_ref[z])
        lw = -jnp.exp(-softplus(-w_pre) - 0.5)
        kt_z = k * (1.0 + (a_z - 1.0) * ka)
        tot = jnp.sum(lw, axis=0, keepdims=True)
        fin = ()
        if not first:
            kt_o = k * (1.0 + (lora_a(da, 1 - z) - 1.0) * ka)
            fin = (pdm[:, 3 * GROUP_W + 256:], r * (kt_z + kt_o) * rk, v)
        return (r, k * kk, kt_z, a_z, v, cumsum_col(mcs[z], lw), lw, tot), (v, jnp.exp(tot)), fin

    def finish(tot_h, dg, rkk, v):
        gate = bmm(sigmoid(dg), g2)
        return _bf(gate * (head_norm(tot_h, gain, bd, True) + head_sum(rkk, bd) * v))

    _scan(nc, g, ns, prep, rwkv_pre, 3, rwkv_step, [sbd], hscr, finish, y_ref)
    _store_state(s1_ref, sbd, ns)


def mod_kernel(c_ref, w_ref, b_ref, o_ref):
    o_ref[0] = bmm(silu(c_ref[...]), w_ref[0]) + b_ref[0]


def _rows(is_ctx, ctx_ref, lat_ref):
    return jnp.where(is_ctx, ctx_ref[...], lat_ref[...])


def proj_kernel(xp_ref, xs_ref, mod_ref, wa, wb, wc, wd, wg, wgt, pa, pb, pc, pd, pg, pgt, *, n_ptiles):
    m = mod_ref[0]
    x = _rows(pl.program_id(0) < n_ptiles, xp_ref, xs_ref)
    u = layer_norm(x) * (1.0 + m[1:2]) + m[0:1]
    ub = u.astype(BF)
    pa[...] = _dot(ub, wa[...], _NN)
    pb[...] = _dot(ub, wb[...], _NN)
    pc[...] = _dot(ub, wc[...], _NN)
    pd[...] = _dot(ub, wd[...], _NN)
    pg[...] = _dot(ub, wg[...], _NN)
    gt = _dot(wgt[...], ub, _NT)
    for j in range(ROW_TILE // CHUNK):
        pgt[j] = gt[:, j * CHUNK:(j + 1) * CHUNK]


def post_kernel(hp, hs, yap, yas, ybp, ybs, ycp, ycs, ydp, yds, mod_ref, wout, l1g, l1b, w1, w2, l2g, l2b,
                *o_refs, n_ptiles):
    m = mod_ref[0]
    is_ctx = pl.program_id(0) < n_ptiles
    y = jnp.concatenate([_rows(is_ctx, yap, yas), _rows(is_ctx, ybp, ybs), _rows(is_ctx, ycp, ycs),
                         _rows(is_ctx, ydp, yds)], axis=1)
    attn = _dot(y, wout[...], _NN)
    h1 = layer_norm(DEEPNORM_ALPHA * _rows(is_ctx, hp, hs) + m[2:3] * attn) * l1g[...] + l1b[...]
    u2 = layer_norm(h1) * (1.0 + m[4:5]) + m[3:4]
    f = _dot(u2.astype(BF), w1[...], _NN)
    f = jnp.square(jnp.maximum(f, 0.0))
    f2 = _dot(f.astype(BF), w2[...], _NN)
    out = layer_norm(DEEPNORM_ALPHA * h1 + m[5:6] * f2) * l2g[...] + l2b[...]
    if len(o_refs) == 1:
        o_refs[0][...] = out
    else:
        @pl.when(is_ctx)
        def _():
            o_refs[0][...] = out

        @pl.when(jnp.logical_not(is_ctx))
        def _():
            o_refs[1][...] = out


def _params(n_grid_axes):
    return pltpu.CompilerParams(dimension_semantics=("arbitrary",) * n_grid_axes,
                                vmem_limit_bytes=VMEM_LIMIT)


def _const_spec(a, layer=None):
    nd = a.ndim
    if layer is None:
        return pl.BlockSpec(a.shape, lambda *_: (0,) * nd, pipeline_mode=pl.Buffered(1))
    return pl.BlockSpec((None,) + a.shape[1:], lambda *_: (layer,) + (0,) * (nd - 1), pipeline_mode=pl.Buffered(1))


def _f32(shape):
    return jax.ShapeDtypeStruct(shape, jnp.float32)


def _mod_call(c_all, w_mod, b_mod):
    depth, d, n = w_mod.shape
    tn = 2048
    return pl.pallas_call(
        mod_kernel, grid=(depth, n // tn),
        in_specs=[pl.BlockSpec(c_all.shape, lambda l, j: (0, 0)),
                  pl.BlockSpec((1, d, tn), lambda l, j: (l, 0, j)),
                  pl.BlockSpec((1, 1, tn), lambda l, j: (l, 0, j))],
        out_specs=pl.BlockSpec((1, c_all.shape[0], tn), lambda l, j: (l, 0, j)),
        out_shape=_f32((depth, c_all.shape[0], n)),
        compiler_params=_params(2), name="mod")(c_all, w_mod, b_mod.reshape(depth, 1, n))


def _mod_spec(layer, n_ptiles, tiles_per_dec):
    def index(i):
        return (layer, jnp.where(i < n_ptiles, 0, 1 + (i - n_ptiles) // tiles_per_dec), 0, 0)
    return pl.BlockSpec((None, 1, 6, D_MODEL), index)


def _pair_specs(pair, n_ptiles):
    width = pair[0].shape[1]
    off = n_ptiles if pair[0] is pair[1] else 0
    return [pl.BlockSpec((ROW_TILE, width), lambda i: (jnp.minimum(i, n_ptiles - 1), 0)),
            pl.BlockSpec((ROW_TILE, width), lambda i: (jnp.maximum(i - n_ptiles, 0) + off, 0))]


def _proj_call(x_pair, n_rows, mod, layer, ws, n_ptiles, tiles_per_dec):
    row = lambda i: (i, 0)
    widths = [w.shape[-1] for w in ws[:5]]
    per_tile = ROW_TILE // CHUNK
    return pl.pallas_call(
        functools.partial(proj_kernel, n_ptiles=n_ptiles), grid=(n_rows // ROW_TILE,),
        in_specs=_pair_specs(x_pair, n_ptiles) + [_mod_spec(layer, n_ptiles, tiles_per_dec)]
                 + [_const_spec(w, layer) for w in ws],
        out_specs=[pl.BlockSpec((ROW_TILE, w), row) for w in widths]
                  + [pl.BlockSpec((per_tile, N_GATES, CHUNK), lambda i: (i, 0, 0))],
        out_shape=[_f32((n_rows, w)) for w in widths] + [_f32((n_rows // CHUNK, N_GATES, CHUNK))],
        compiler_params=_params(1), name="proj")(*x_pair, mod, *ws)


def _post_call(h_pair, y_pairs, n_rows, mod, layer, ws, n_ptiles, tiles_per_dec, split_out):
    pairs = [h_pair] + y_pairs
    if split_out:
        n_prows = n_ptiles * ROW_TILE
        out_specs = _pair_specs((_f32((n_prows, D_MODEL)), _f32((n_rows - n_prows, D_MODEL))), n_ptiles)
        out_shape = [_f32((n_prows, D_MODEL)), _f32((n_rows - n_prows, D_MODEL))]
    else:
        out_specs = pl.BlockSpec((ROW_TILE, D_MODEL), lambda i: (i, 0))
        out_shape = _f32((n_rows, D_MODEL))
    return pl.pallas_call(
        functools.partial(post_kernel, n_ptiles=n_ptiles), grid=(n_rows // ROW_TILE,),
        in_specs=[spec for p in pairs for spec in _pair_specs(p, n_ptiles)]
                 + [_mod_spec(layer, n_ptiles, tiles_per_dec)] + [_const_spec(w, layer) for w in ws],
        out_specs=out_specs, out_shape=out_shape,
        compiler_params=_params(1), name="post")(*[a for p in pairs for a in p], mod, *ws)


MAT = (2, N_HEADS, HEAD_DIM, HEAD_DIM)
VEC = (2, N_HEADS, 1, HEAD_DIM)
SCL = (2, N_HEADS, 1, 1)


def _mixer_call(kernel, name, t_len, n_seq, row_off, seq_inputs, const_inputs, state_inputs, layer,
                state_tails, scratch):
    nc = t_len // CHUNK
    g = _scan_steps(nc)
    ns = max(1, min(n_seq, PRE_BATCH // (2 * g)))
    assert n_seq % ns == 0 and row_off % (ns * t_len) == 0
    blk_off = row_off // (ns * t_len)
    in_specs = []
    for a in seq_inputs:
        if a.ndim == 2:
            in_specs.append(pl.BlockSpec((ns * t_len, a.shape[1]), lambda i: (i + blk_off, 0)))
        else:
            in_specs.append(pl.BlockSpec((ns * nc,) + a.shape[1:], lambda i: (i + blk_off, 0, 0)))
    in_specs += [_const_spec(a, lay) for a, lay in const_inputs]
    for a in state_inputs:
        tail = a.shape[2:]
        in_specs.append(pl.BlockSpec((ns, 1) + tail, lambda i, nd=len(tail): (i, layer) + (0,) * nd))
    out_specs = [pl.BlockSpec((ns * t_len, GROUP_W), lambda i: (i, 0))]
    out_specs += [pl.BlockSpec((ns, 1) + t, lambda i, nd=len(t): (i, 0) + (0,) * nd) for t in state_tails]
    out_shape = [jax.ShapeDtypeStruct((n_seq * t_len, GROUP_W), BF)] + [_f32((n_seq, 1) + t) for t in state_tails]
    return pl.pallas_call(
        functools.partial(kernel, nc=nc, g=g, ns=ns), grid=(n_seq // ns,),
        in_specs=in_specs, out_specs=out_specs, out_shape=out_shape,
        scratch_shapes=[pltpu.VMEM((ns * t_len, GROUP_W), jnp.float32)] + scratch(ns),
        compiler_params=_params(1), name=name)(*seq_inputs, *[a for a, _ in const_inputs], *state_inputs)


def _bd_state(ns):
    return pltpu.VMEM((2 * ns, GROUP_W, GROUP_W), jnp.float32)


def _rope_tables(t_len):
    nf = HEAD_DIM // 4
    inv = ROPE_BASE ** (-jnp.arange(nf, dtype=jnp.float32) / nf)
    t = jnp.arange(t_len)
    ang_r = (t // GRID_W).astype(jnp.float32)[:, None] * inv[None, :]
    ang_c = (t % GRID_W).astype(jnp.float32)[:, None] * inv[None, :]
    cos = jnp.concatenate([jnp.cos(ang_r)] * 2 + [jnp.cos(ang_c)] * 2, axis=1)
    sin = jnp.concatenate([-jnp.sin(ang_r), jnp.sin(ang_r), -jnp.sin(ang_c), jnp.sin(ang_c)], axis=1)
    return jnp.tile(cos, (1, N_HEADS)), jnp.tile(sin, (1, N_HEADS))


def kernel(x_prompt, x_sample, c, state_mlstm_C, state_mlstm_n, state_mlstm_m, state_delta, state_ret,
           state_rwkv, c_ctx, w_mod, b_mod, w_in, w_out, ln1_g, ln1_b, ln2_g, ln2_b, w_ff1, w_ff2,
           mlstm_i_bias, mlstm_f_bias, mlstm_norm, delta_conv, delta_a_log, delta_dt_bias, delta_norm,
           ret_decay, ret_norm, rwkv_mu, rwkv_w0, rwkv_w2, rwkv_a0, rwkv_a2, rwkv_g2, rwkv_kk, rwkv_ka,
           rwkv_rk, rwkv_norm):
    f32 = jnp.float32
    batch, seq, d = x_prompt.shape
    dec_batch, dec_seq, _ = x_sample.shape
    assert d == D_MODEL and (batch * seq) % ROW_TILE == 0 and dec_seq % ROW_TILE == 0 and seq % CHUNK == 0
    assert (seq // CHUNK) % 2 == 0 and (dec_seq // CHUNK) % 2 == 0
    n_prows = batch * seq
    n_ptiles = n_prows // ROW_TILE
    tiles_per_dec = dec_seq // ROW_TILE

    c_all = jnp.zeros((8, d), f32).at[0].set(c_ctx).at[1:1 + dec_batch].set(c)
    mod = _mod_call(c_all, w_mod, b_mod).reshape(DEPTH, 8, 6, d)
    n_rows = n_prows + dec_batch * dec_seq
    h_pair = (x_prompt.reshape(n_prows, d), x_sample.reshape(dec_batch * dec_seq, d))
    cos_t, sin_t = _rope_tables(dec_seq)

    zmat = jnp.zeros((batch, 1) + MAT, f32)
    zn = jnp.zeros((batch, 1) + VEC, f32)
    zm = jnp.zeros((batch, 1) + SCL, f32)
    n_cache = state_mlstm_n.reshape(state_mlstm_n.shape[:-1] + (1, HEAD_DIM))
    m_cache = state_mlstm_m.reshape(state_mlstm_m.shape + (1, 1))
    groups = ((seq, batch, 0), (dec_seq, dec_batch, n_prows))

    bf = lambda a: a.astype(BF)
    rows = lambda a: a.reshape(DEPTH, 1, -1)
    zeros = lambda *shape: jnp.zeros((DEPTH,) + shape, f32)
    wg = bf(jnp.pad(jnp.concatenate([w_in[:, :, 1024:1040], w_in[:, :, 2064:2080]], axis=2),
                    ((0, 0), (0, 0), (0, GATE_LANES - N_GATES))))
    proj_ws = [bf(w_in[:, :, 0:1024]), bf(w_in[:, :, 1040:2064]), bf(w_in[:, :, 2080:3104]), bf(w_in[:, :, 3104:4256]),
               wg, jnp.swapaxes(wg[:, :, :N_GATES], 1, 2)]
    post_ws = [bf(w_out), rows(ln1_g), rows(ln1_b), bf(w_ff1), bf(w_ff2), rows(ln2_g), rows(ln2_b)]
    bias_c = jnp.concatenate([rows(mlstm_i_bias), rows(mlstm_f_bias), zeros(1, GATE_LANES - 16)], axis=2)
    bias_r = jnp.swapaxes(bias_c[:, :, :N_GATES], 1, 2)
    dl_row = jnp.concatenate([zeros(2, 24), jnp.concatenate([rows(delta_a_log), rows(delta_dt_bias)], axis=1),
                              zeros(2, GATE_LANES - 32)], axis=2)
    dl_col = jnp.swapaxes(dl_row[:, :, :N_GATES], 1, 2)
    rd_row = jnp.concatenate([rows(ret_decay), zeros(1, GATE_LANES - 8)], axis=2)
    rw_vec = jnp.concatenate([rows(rwkv_kk), rows(rwkv_ka), rows(rwkv_rk), rows(rwkv_norm)], axis=1)

    new_states = []
    for l in range(DEPTH):
        lay_of = lambda *arrs: [(a, l) for a in arrs]
        pa, pb, pc, pd, pg, pgt = _proj_call(h_pair, n_rows, mod, l, proj_ws, n_ptiles, tiles_per_dec)
        ys = [[], [], [], []]
        layer_states = None
        for gi, (t_len, n_seq, off) in enumerate(groups):
            ctx = gi == 0
            lay = 0 if ctx else l
            ya, c1, n1, m1 = _mixer_call(
                mlstm_kernel, "mlstm", t_len, n_seq, off, [pa, pg, pgt],
                lay_of(bias_c, bias_r, rows(mlstm_norm)),
                [zmat, zn, zm] if ctx else [state_mlstm_C, n_cache, m_cache], lay, [MAT, VEC, SCL],
                lambda ns: [_bd_state(ns), pltpu.VMEM((2 * ns, 1, GROUP_W), f32), pltpu.VMEM((2 * ns, 1, GROUP_W), f32)])
            yb, sb1 = _mixer_call(
                delta_kernel, "delta", t_len, n_seq, off, [pb, pg, pgt],
                lay_of(delta_conv, dl_row, dl_col, rows(delta_norm)),
                [zmat if ctx else state_delta], lay, [MAT], lambda ns: [_bd_state(ns)])
            ret_consts = lay_of(rd_row, rows(ret_norm))
            if not ctx:
                ret_consts = [(cos_t, None), (sin_t, None)] + ret_consts
            yc, sc1 = _mixer_call(
                functools.partial(ret_kernel, rope=not ctx), "ret", t_len, n_seq, off, [pc],
                ret_consts, [zmat if ctx else state_ret], lay, [MAT],
                lambda ns: [_bd_state(ns), pltpu.VMEM((2, 4, CHUNK, GROUP_W), f32)])
            yd, sd1 = _mixer_call(
                rwkv_kernel, "rwkv", t_len, n_seq, off, [pd],
                lay_of(rows(rwkv_mu), rwkv_w0, rwkv_w2, rwkv_a0, rwkv_a2, rwkv_g2, rw_vec),
                [zmat if ctx else state_rwkv], lay, [MAT], lambda ns: [_bd_state(ns)])
            for lst, y in zip(ys, (ya, yb, yc, yd)):
                lst.append(y)
            if ctx:
                layer_states = (c1, n1, m1, sb1, sc1, sd1)
        new_states.append(layer_states)
        last = l == DEPTH - 1
        h_new = _post_call(h_pair, [tuple(y) for y in ys], n_rows, mod, l, post_ws, n_ptiles, tiles_per_dec, last)
        h_pair = tuple(h_new) if last else (h_new, h_new)

    outs = [jnp.concatenate([st[i] for st in new_states], axis=1) for i in range(6)]
    outs[1] = outs[1].reshape(outs[1].shape[:4] + (HEAD_DIM,))
    outs[2] = outs[2].reshape(outs[2].shape[:4])
    return (h_pair[0].reshape(batch, seq, d), h_pair[1].reshape(dec_batch, dec_seq, d), *outs)
```

```python
import functools

import jax
import jax.numpy as jnp
from jax import lax
from jax.experimental import pallas as pl
from jax.experimental.pallas import tpu as pltpu

D_MODEL = 1024
DEPTH = 2
GRID_W = 64
HEAD_DIM = 64
N_HEADS = 4
GROUP_W = N_HEADS * HEAD_DIM
CHUNK = 64
CONV_K = 5
ROPE_BASE = 10000.0
LN_EPS = 1e-5
DEEPNORM_ALPHA = (2 * DEPTH) ** 0.25
RWKV_LORA = 64
GATE_LANES = 128
N_GATES = 32
ROW_TILE = 512
VMEM_LIMIT = 56 * 1024 * 1024
SMALL_BATCH = (2, 8)
LARGE_BATCH = (4, 16)
HALO = CHUNK + 16

NEG_INF = float("-inf")
BF = jnp.bfloat16
_NN = (((1,), (0,)), ((), ()))
_NT = (((1,), (1,)), ((), ()))
_TN = (((0,), (0,)), ((), ()))


def _dot(a, b, dims):
    return lax.dot_general(a, b, dims, preferred_element_type=jnp.result_type(a.dtype, jnp.float32))


def _bf(x):
    return x.astype(BF)


def _split(x, n):
    parts = []
    for _ in range(n - 1):
        p = _bf(x)
        parts.append(p)
        x = x - p.astype(x.dtype)
    parts.append(_bf(x))
    return parts


def bmm(a, b, dims=_NN):
    return _dot(_bf(a), _bf(b), dims)


def _expand(x, bdm):
    lane_head = jnp.right_shift(lax.broadcasted_iota(jnp.int32, (CHUNK, GROUP_W), 1), 6)
    zero = jnp.zeros((), x.dtype)
    return jnp.concatenate([jnp.where(lane_head == h, x, zero) for h in range(N_HEADS)], axis=0)


def wmm(a, b, bdm, passes=1, nt=False):
    dims = _NT if nt else _NN
    if passes == 1:
        return _dot(_bf(a), _expand(_bf(b), bdm), dims)
    ah, al = _split(a, 2)
    bh, bl = _split(b, 2)
    ebh = _expand(bh, bdm)
    lhs = jnp.concatenate([ah, ah, al], axis=1)
    rhs = jnp.concatenate([ebh, _expand(bl, bdm), ebh], axis=1 if nt else 0)
    return _dot(lhs, rhs, dims)


def wmm2(a, b1, b2, bdm, nt=False):
    e1, e2 = _expand(_bf(b1), bdm), _expand(_bf(b2), bdm)
    out = _dot(_bf(a), jnp.concatenate([e1, e2], axis=0 if nt else 1), _NT if nt else _NN)
    return out[:, :GROUP_W], out[:, GROUP_W:]


def cumsum_col(mc, x):
    return _dot(jnp.concatenate([_bf(mc)] * 3, axis=1), jnp.concatenate(_split(x, 3), axis=0), _NN)


def cumsum_row(x, mc):
    return _dot(jnp.concatenate(_split(x, 3), axis=1), jnp.concatenate([_bf(mc)] * 3, axis=1), _NT)


def head_sum(x, bd):
    return _dot(jnp.concatenate(_split(x, 2), axis=1), jnp.concatenate([_bf(bd)] * 2, axis=0), _NN)


def sigmoid(x):
    return 0.5 + 0.5 * jnp.tanh(0.5 * x)


def silu(x):
    return x * sigmoid(x)


def softplus(x):
    return jnp.maximum(x, 0.0) + jnp.log(1.0 + jnp.exp(-jnp.abs(x)))


def log_sigmoid(x):
    return -softplus(-x)


def head_norm(y, gain, bd, center):
    if center:
        y = y - head_sum(y, bd) * (1.0 / HEAD_DIM)
    ms = head_sum(y * y, bd) * (1.0 / HEAD_DIM)
    return y * lax.rsqrt(ms + LN_EPS) * gain


def l2norm_heads(x, bd):
    return x * lax.rsqrt(head_sum(x * x, bd) + 1e-6)


def layer_norm(x):
    mu = jnp.mean(x, axis=-1, keepdims=True)
    xc = x - mu
    var = jnp.mean(xc * xc, axis=-1, keepdims=True)
    return xc * lax.rsqrt(var + LN_EPS)


def colw(x, idx):
    m = x.shape[0]
    return jnp.concatenate([jnp.broadcast_to(x[:, j:j + 1], (m, HEAD_DIM)) for j in idx], axis=1)


def roww(x, idx):
    return jnp.concatenate([x[j:j + 1, :] for j in idx], axis=1)


def segmax(x):
    m = x.shape[0]
    return jnp.concatenate(
        [jnp.broadcast_to(jnp.max(x[:, h * HEAD_DIM:(h + 1) * HEAD_DIM], axis=1, keepdims=True), (m, HEAD_DIM))
         for h in range(N_HEADS)], axis=1)


def wide_masks(rev, dtype=jnp.float32):
    r = lax.broadcasted_iota(jnp.int32, (CHUNK, GROUP_W), 0)
    c = jnp.bitwise_and(lax.broadcasted_iota(jnp.int32, (CHUNK, GROUP_W), 1), HEAD_DIM - 1)
    if rev:
        incl_w, strict_w = r <= c, r < c
    else:
        incl_w, strict_w = r >= c, r > c
    mc = incl_w[:, :CHUNK].astype(dtype)
    return incl_w, strict_w, mc, (r == c).astype(dtype)


def block_diag_mask():
    r = jnp.right_shift(lax.broadcasted_iota(jnp.int32, (GROUP_W, GROUP_W), 0), 6)
    c = jnp.right_shift(lax.broadcasted_iota(jnp.int32, (GROUP_W, GROUP_W), 1), 6)
    m = (r == c).astype(jnp.float32)
    return _bf(m), m


def inv_unit_tri_w(a, eye_w, bdm):
    p = _bf(-a)
    e = _expand(p, bdm)
    t = eye_w - a
    for _ in range(5):
        p = _bf(_dot(p, e, _NN))
        e = _expand(p, bdm)
        t = t + _dot(_bf(t), e, _NN)
    r = eye_w - t - wmm(a, t, bdm, 3)
    return t + wmm(t, r, bdm, 1)


def mlstm_pre(q, k, v, bcol_w, brow_w, icol_w, irow_w, bl_w, incl_w, bdm):
    logd = jnp.where(incl_w, bcol_w - brow_w + irow_w, NEG_INF)
    mt0 = segmax(logd)
    s0 = wmm(q, k, bdm, 1, nt=True) * jnp.exp(logd - mt0)
    wk = bl_w - bcol_w + icol_w
    wkmax = jnp.max(wk, axis=0, keepdims=True)
    ks0 = k * jnp.exp(wk - wkmax)
    return (q, bcol_w, bl_w, mt0, wmm(s0, v, bdm, 1), head_sum(s0, bdm[1]), wkmax,
            _bf(ks0), _bf(v), jnp.sum(ks0, axis=0, keepdims=True))


def mlstm_step(q, bcol_w, bl_w, mt0, num0, den0, wkmax, ks0, v, ksum0, c_bd, n_row, m_w, bdm):
    inter = bcol_w + m_w
    mt = jnp.maximum(inter, mt0)
    f = jnp.exp(mt0 - mt)
    ew = jnp.exp(inter - mt)
    num = num0 * f + bmm(q, c_bd) * ew
    den = den0 * f + head_sum(q * n_row, bdm[1]) * ew
    h = num / jnp.maximum(jnp.abs(den), jnp.exp(-mt))
    m_new = jnp.maximum(bl_w + m_w, wkmax)
    sp = jnp.exp(bl_w + m_w - m_new)
    sc = jnp.exp(wkmax - m_new)
    kv0 = bmm(ks0, v, _TN) * bdm[1]
    return h, c_bd * sp + kv0 * sc, n_row * sp + ksum0 * sc, m_new


def delta_pre(q_raw, k_raw, v, beta_w, gcol_w, grow_w, gl_w, incl_w, strict_w, eye_w, bdm):
    q = l2norm_heads(q_raw, bdm[1]) * HEAD_DIM ** -0.5
    k = l2norm_heads(k_raw, bdm[1])
    decay = jnp.exp(jnp.where(incl_w, gcol_w - grow_w, NEG_INF))
    a = jnp.where(strict_w, wmm(k, k, bdm, 1, nt=True) * decay, 0.0) * beta_w
    t = inv_unit_tri_w(a, eye_w, bdm)
    eg = jnp.exp(gcol_w)
    u, w = wmm2(t, v * beta_w, k * (beta_w * eg), bdm)
    qk = wmm(q, k, bdm, 1, nt=True) * decay
    return (u, _bf(jnp.concatenate([w, q * eg], axis=0)), _bf(qk), _bf(k * jnp.exp(gl_w - gcol_w)),
            jnp.exp(gl_w))


def delta_step(u, wq, qk, kd, egl, s_bd, bdm):
    ws = bmm(wq, s_bd)
    v_new = u - ws[:CHUNK]
    o = ws[CHUNK:] + wmm(qk, v_new, bdm, 1)
    s_new = s_bd * egl + bmm(kd, v_new, _TN) * bdm[1]
    return o, s_new


def ret_pre(q, k, v, dmat_w, zeta_w, bdm):
    sc = wmm(q, k, bdm, 1, nt=True) * dmat_w
    return _bf(q), wmm(sc, v, bdm, 1), _bf(k * zeta_w), _bf(v)


def ret_step(q, o1, kz, v, xi_w, gl_w, s_bd, bdm):
    return o1 + bmm(q, s_bd) * xi_w, s_bd * gl_w + bmm(kz, v, _TN) * bdm[1]


def rwkv_pre(r, kkk, kt, a, v, cw, lw, tot, incl_w, strict_w, eye_w, bdm):
    kh = l2norm_heads(kkk, bdm[1])
    b = a * kh
    en = jnp.exp(-cw)
    et = jnp.exp(tot - cw)
    rp = r * jnp.exp(cw)
    khp = kh * jnp.exp(cw - lw)
    ktn = kt * en
    bn = b * en
    lhs = jnp.concatenate([khp, rp], axis=0)
    ab_kt, ab_b = wmm2(lhs, ktn, bn, bdm, nt=True)
    a_kt = jnp.where(strict_w, ab_kt[:CHUNK], 0.0)
    b_kt = jnp.where(incl_w, ab_kt[CHUNK:], 0.0)
    a_b = jnp.where(strict_w, ab_b[:CHUNK], 0.0)
    b_b = jnp.where(incl_w, ab_b[CHUNK:], 0.0)
    t = inv_unit_tri_w(a_b, eye_w, bdm)
    tk = wmm(t, khp, bdm, 1)
    tav = wmm(t, wmm(a_kt, v, bdm, 1), bdm, 1)
    return (_bf(jnp.concatenate([tk, rp], axis=0)), tav, wmm(b_kt, v, bdm, 1), _bf(b_b),
            _bf(jnp.concatenate([kt * et, b * et], axis=0)))


def rwkv_step(tkr, tav, o1, b_b, kbe, v, etot, s_bd, bdm):
    ls = bmm(tkr, s_bd, _NT)
    u = ls[:CHUNK] + tav
    o = ls[CHUNK:] + o1 - wmm(b_b, u, bdm, 1)
    upd = bmm(jnp.concatenate([v, -u], axis=0), kbe, _TN)
    s_new = s_bd * etot + upd * bdm[1]
    return o, s_new


def _group_loop(nc, g, body):
    n_it = nc // g

    def run(first):
        def it(i, carry):
            body(i, first)
            return carry
        return it

    lax.fori_loop(0, n_it // 2, run(True), 0)
    lax.fori_loop(n_it // 2, n_it, run(False), 0)


def _scan_steps(nc, max_steps):
    return max(g for g in (1, 2, 4, 8) if g <= max_steps and nc % (2 * g) == 0)


def _group_chunks(i, g, nc, ns):
    out = []
    for s in range(ns):
        for z in range(2):
            for j in range(g):
                step = i * g + j
                c = s * nc + (step if z == 0 else nc - 1 - step)
                out.append((s, z, c, pl.multiple_of(c * CHUNK, CHUNK)))
    return out


def _group_masks(g, ns):
    shape = (2 * g * ns, CHUNK, GROUP_W)
    lane = jnp.right_shift(lax.broadcasted_iota(jnp.int32, shape, 0), g.bit_length() - 1)
    r = lax.broadcasted_iota(jnp.int32, shape, 1)
    c = jnp.bitwise_and(lax.broadcasted_iota(jnp.int32, shape, 2), HEAD_DIM - 1)
    d = jnp.where(jnp.bitwise_and(lane, 1) == 1, c - r, r - c)
    return d >= 0, d > 0, (d == 0).astype(jnp.float32)


def _stack(items):
    return [jnp.stack(x) for x in zip(*items)]


def _pick(arrs, g, j):
    return [jnp.stack([a[k] for k in range(j, a.shape[0], g)]) for a in arrs]


def _halo(ref, s, c, t0, nc):
    rows = ref.shape[0]
    ps = pl.multiple_of(jnp.maximum(t0 - 8, 0), 8)
    nx = pl.multiple_of(jnp.minimum(t0 + CHUNK, rows - 8), 8)
    prev = jnp.where(c > s * nc, ref[pl.ds(ps, 8), :], 0.0)
    nxt = jnp.where(c < s * nc + nc - 1, ref[pl.ds(nx, 8), :], 0.0)
    return jnp.concatenate([prev, ref[pl.ds(t0, CHUNK), :], nxt], axis=0)


def _shift_matrix(offsets):
    rows = len(offsets) * CHUNK
    r = lax.broadcasted_iota(jnp.int32, (rows, 2 * HALO), 0)
    c = lax.broadcasted_iota(jnp.int32, (rows, 2 * HALO), 1)
    c = jnp.where(c >= HALO, c - HALO, c)
    t = jnp.bitwise_and(r, CHUNK - 1)
    target = t + 8
    for j, off in enumerate(offsets):
        target = jnp.where(jnp.right_shift(r, 6) == j, target + off, target)
    return _bf((c == target).astype(jnp.float32))


def _hs(h):
    return slice(h * HEAD_DIM, (h + 1) * HEAD_DIM)


def _load_state(sbd, s0_ref, ns):
    for s in range(ns):
        for z in range(2):
            sbd[2 * s + z] = jnp.zeros((GROUP_W, GROUP_W), jnp.float32)
            for h in range(N_HEADS):
                sbd[2 * s + z, _hs(h), _hs(h)] = s0_ref[s, 0, z, h]


def _store_state(s1_ref, sbd, ns):
    for s in range(ns):
        for z in range(2):
            for h in range(N_HEADS):
                s1_ref[s, 0, z, h] = sbd[2 * s + z, _hs(h), _hs(h)]


def _merge(first, hscr, t0, h):
    if first:
        hscr[pl.ds(t0, CHUNK), :] = h
        return None
    return hscr[pl.ds(t0, CHUNK), :] + h


def _scan(nc, g, ns, prep, pre_fn, n_masks, step_fn, state_refs, hscr, finish, y_ref):
    bdm = block_diag_mask()
    masks = _group_masks(g, ns)[:n_masks]

    def body(i, first):
        chunks = _group_chunks(i, g, nc, ns)
        preps = [prep(s, z, c, t0, first) for s, z, c, t0 in chunks]
        n_in = len(preps[0][0])
        pre = jax.vmap(pre_fn, in_axes=(0,) * (n_in + n_masks) + (None,))(
            *_stack([p[0] for p in preps]), *masks, bdm)
        extra = _stack([p[1] for p in preps]) if preps[0][1] else []
        states = [ref[...] for ref in state_refs]
        for j in range(g):
            args = _pick(list(pre), g, j) + _pick(extra, g, j) + states
            res = jax.vmap(step_fn, in_axes=(0,) * len(args) + (None,))(*args, bdm)
            states = list(res[1:])
            ks = [lane * g + j for lane in range(2 * ns)]
            tots = [_merge(first, hscr, chunks[k][3], res[0][lane]) for lane, k in enumerate(ks)]
            if not first:
                ys = jax.vmap(finish)(jnp.stack(tots), *_stack([preps[k][2] for k in ks]))
                for lane, k in enumerate(ks):
                    y_ref[pl.ds(chunks[k][3], CHUNK), :] = ys[lane]
        for ref, val in zip(state_refs, states):
            ref[...] = val

    _group_loop(nc, g, body)


def mlstm_kernel(p_ref, gc_ref, gr_ref, bc_ref, br_ref, norm_ref, c0_ref, n0_ref, m0_ref,
                 y_ref, c1_ref, n1_ref, m1_ref, hscr, cbd, nrow, mw, *, nc, g, ns):
    _load_state(cbd, c0_ref, ns)
    for s in range(ns):
        for z in range(2):
            for h in range(N_HEADS):
                nrow[2 * s + z, :, _hs(h)] = n0_ref[s, 0, z, h]
                mw[2 * s + z, :, _hs(h)] = jnp.broadcast_to(m0_ref[s, 0, z, h], (1, HEAD_DIM))
    bd = block_diag_mask()[1]
    mcs = [wide_masks(False)[2], wide_masks(True)[2]]
    gain = norm_ref[...]

    def prep(s, z, c, t0, first):
        x = p_ref[pl.ds(t0, CHUNK), 0:3 * GROUP_W]
        g_c = gc_ref[pl.ds(t0, CHUNK), :] + bc_ref[...]
        g_r = gr_ref[c] + br_ref[...]
        lf_c = log_sigmoid(g_c)
        idx_i = [z * N_HEADS + h for h in range(N_HEADS)]
        idx_f = [8 + j for j in idx_i]
        fin = () if first else (p_ref[pl.ds(t0, CHUNK), 3 * GROUP_W:],)
        return ((x[:, :GROUP_W], x[:, GROUP_W:2 * GROUP_W] * HEAD_DIM ** -0.5, x[:, 2 * GROUP_W:],
                 colw(cumsum_col(mcs[z], lf_c), idx_f), roww(cumsum_row(log_sigmoid(g_r), mcs[z]), idx_f),
                 colw(g_c, idx_i), roww(g_r, idx_i), colw(jnp.sum(lf_c, axis=0, keepdims=True), idx_f)), (), fin)

    def finish(tot, og):
        return _bf(sigmoid(og) * head_norm(tot, gain, bd, True))

    _scan(nc, g, ns, prep, mlstm_pre, 1, mlstm_step, [cbd, nrow, mw], hscr, finish, y_ref)
    _store_state(c1_ref, cbd, ns)
    for s in range(ns):
        for z in range(2):
            for h in range(N_HEADS):
                n1_ref[s, 0, z, h] = nrow[2 * s + z, :, _hs(h)]
                m1_ref[s, 0, z, h] = mw[2 * s + z, :, h * HEAD_DIM:h * HEAD_DIM + 1]


def delta_kernel(p_ref, gc_ref, gr_ref, conv_ref, prow_ref, pcol_ref, norm_ref, s0_ref,
                 y_ref, s1_ref, hscr, sbd, *, nc, g, ns):
    _load_state(sbd, s0_ref, ns)
    bd = block_diag_mask()[1]
    mcs = [wide_masks(False)[2], wide_masks(True)[2]]
    nega_row = -jnp.exp(prow_ref[0:1, :])
    dtb_row = prow_ref[1:2, :]
    nega_col = -jnp.exp(pcol_ref[:, 0:1])
    dtb_col = pcol_ref[:, 1:2]
    qkv_w = 3 * GROUP_W
    gain = norm_ref[...]
    offsets = [j - CONV_K // 2 for j in range(CONV_K) if j != CONV_K // 2]
    taps = _shift_matrix(offsets)

    def prep(s, z, c, t0, first):
        ext = _halo(p_ref.at[:, 0:qkv_w], s, c, t0, nc)
        shifted = _dot(taps, jnp.concatenate(_split(ext, 2), axis=0), _NN)
        acc = ext[8:8 + CHUNK] * conv_ref[CONV_K // 2:CONV_K // 2 + 1, :]
        for i, off in enumerate(offsets):
            j = off + CONV_K // 2
            acc = acc + shifted[i * CHUNK:(i + 1) * CHUNK] * conv_ref[j:j + 1, :]
        qkv = silu(acc)
        g_c = gc_ref[pl.ds(t0, CHUNK), :]
        gg_c = nega_row * softplus(g_c + dtb_row)
        gg_r = nega_col * softplus(gr_ref[c] + dtb_col)
        idx_b = [16 + z * N_HEADS + h for h in range(N_HEADS)]
        idx_a = [8 + j for j in idx_b]
        fin = () if first else (p_ref[pl.ds(t0, CHUNK), qkv_w:],)
        return ((qkv[:, :GROUP_W], qkv[:, GROUP_W:2 * GROUP_W], qkv[:, 2 * GROUP_W:], colw(sigmoid(g_c), idx_b),
                 colw(cumsum_col(mcs[z], gg_c), idx_a), roww(cumsum_row(gg_r, mcs[z]), idx_a),
                 colw(jnp.sum(gg_c, axis=0, keepdims=True), idx_a)), (), fin)

    def finish(tot, zg):
        return _bf(head_norm(tot, gain, bd, False) * silu(zg))

    _scan(nc, g, ns, prep, delta_pre, 3, delta_step, [sbd], hscr, finish, y_ref)
    _store_state(s1_ref, sbd, ns)


def _rope(x, cos, sin, lane_lo):
    n = x.shape[1]
    swapped = jnp.where(lane_lo, pltpu.roll(x, n - 16, 1), pltpu.roll(x, 16, 1))
    return x * cos + swapped * sin


def ret_kernel(*refs, nc, g, ns, rope):
    if rope:
        p_ref, cos_ref, sin_ref, rd_ref, norm_ref, s0_ref, y_ref, s1_ref, hscr, sbd, cst = refs
    else:
        p_ref, rd_ref, norm_ref, s0_ref, y_ref, s1_ref, hscr, sbd, cst = refs
    f32 = jnp.float32
    _load_state(sbd, s0_ref, ns)
    bd = block_diag_mask()[1]
    r = lax.broadcasted_iota(jnp.int32, (CHUNK, GROUP_W), 0).astype(f32)
    cc = jnp.bitwise_and(lax.broadcasted_iota(jnp.int32, (CHUNK, GROUP_W), 1), HEAD_DIM - 1).astype(f32)
    lg_row = -jnp.exp(rd_ref[...])
    gain = norm_ref[...]
    for z in range(2):
        incl_w = wide_masks(z == 1)[0]
        lg = colw(lg_row, [z * N_HEADS + h for h in range(N_HEADS)])
        if z == 0:
            cst[z, 0] = jnp.exp(jnp.where(incl_w, (r - cc) * lg, NEG_INF))
            cst[z, 1] = jnp.exp((r + 1.0) * lg)
            cst[z, 2] = jnp.exp((CHUNK - 1.0 - r) * lg)
        else:
            cst[z, 0] = jnp.exp(jnp.where(incl_w, (cc - r) * lg, NEG_INF))
            cst[z, 1] = jnp.exp((CHUNK - r) * lg)
            cst[z, 2] = jnp.exp(r * lg)
        cst[z, 3] = jnp.exp(CHUNK * lg + 0.0 * r)
    if rope:
        lane = lax.broadcasted_iota(jnp.int32, (CHUNK, GROUP_W), 1)
        lane_lo = jnp.bitwise_and(lane, 31) < 16

    def prep(s, z, c, t0, first):
        x = p_ref[pl.ds(t0, CHUNK), 0:3 * GROUP_W]
        q = x[:, :GROUP_W] * HEAD_DIM ** -0.5
        k = x[:, GROUP_W:2 * GROUP_W]
        if rope:
            tr = pl.multiple_of(t0 - s * nc * CHUNK, CHUNK)
            cos = cos_ref[pl.ds(tr, CHUNK), :]
            sin = sin_ref[pl.ds(tr, CHUNK), :]
            q = _rope(q, cos, sin, lane_lo)
            k = _rope(k, cos, sin, lane_lo)
        fin = () if first else (p_ref[pl.ds(t0, CHUNK), 3 * GROUP_W:],)
        return (q, k, x[:, 2 * GROUP_W:], cst[z, 0], cst[z, 2]), (cst[z, 1], cst[z, 3, 0:1, :]), fin

    def finish(tot, gate):
        return _bf(silu(gate) * head_norm(tot, gain, bd, True))

    _scan(nc, g, ns, prep, ret_pre, 0, ret_step, [sbd], hscr, finish, y_ref)
    _store_state(s1_ref, sbd, ns)


def rwkv_kernel(p_ref, mu_ref, w0_ref, w2_ref, a0_ref, a2_ref, g2_ref, vec_ref, s0_ref,
                y_ref, s1_ref, hscr, sbd, *, nc, g, ns):
    _load_state(sbd, s0_ref, ns)
    bd = block_diag_mask()[1]
    mcs = [wide_masks(False)[2], wide_masks(True)[2]]
    kk = vec_ref[0:1, :]
    ka = vec_ref[1:2, :]
    rk = vec_ref[2:3, :]
    gain = vec_ref[3:4, :]
    g2 = g2_ref[...]

    def lora_a(da, zz):
        return sigmoid(a0_ref[zz:zz + 1, :] + bmm(da[:, zz * RWKV_LORA:(zz + 1) * RWKV_LORA], a2_ref[zz]))

    def prep(s, z, c, t0, first):
        ext = _halo(p_ref, s, c, t0, nc)
        cur = ext[8:8 + CHUNK]
        nb = pltpu.roll(ext, 1, 0) + pltpu.roll(ext, HALO - 1, 0)
        pdm = cur + mu_ref[...] * (0.5 * nb[8:8 + CHUNK] - cur)
        r = pdm[:, :GROUP_W]
        k = pdm[:, GROUP_W:2 * GROUP_W]
        v = pdm[:, 2 * GROUP_W:3 * GROUP_W]
        dw = pdm[:, 3 * GROUP_W:3 * GROUP_W + 128]
        da = pdm[:, 3 * GROUP_W + 128:3 * GROUP_W + 256]
        a_z = lora_a(da, z)
        w_pre = w0_ref[z:z + 1, :] + bmm(jnp.tanh(dw[:, z * RWKV_LORA:(z + 1) * RWKV_LORA]), w2_ref[z])
        lw = -jnp.exp(-softplus(-w_pre) - 0.5)
        kt_z = k * (1.0 + (a_z - 1.0) * ka)
        tot = jnp.sum(lw, axis=0, keepdims=True)
        fin = ()
        if not first:
            kt_o = k * (1.0 + (lora_a(da, 1 - z) - 1.0) * ka)
            fin = (pdm[:, 3 * GROUP_W + 256:], r * (kt_z + kt_o) * rk, v)
        return (r, k * kk, kt_z, a_z, v, cumsum_col(mcs[z], lw), lw, tot), (v, jnp.exp(tot)), fin

    def finish(tot_h, dg, rkk, v):
        gate = bmm(sigmoid(dg), g2)
        return _bf(gate * (head_norm(tot_h, gain, bd, True) + head_sum(rkk, bd) * v))

    _scan(nc, g, ns, prep, rwkv_pre, 3, rwkv_step, [sbd], hscr, finish, y_ref)
    _store_state(s1_ref, sbd, ns)


def mod_kernel(c_ref, w_ref, b_ref, o_ref):
    o_ref[0] = bmm(silu(c_ref[...]), w_ref[0]) + b_ref[0]


def _rows(is_ctx, ctx_ref, lat_ref):
    return jnp.where(is_ctx, ctx_ref[...], lat_ref[...])


def proj_kernel(xp_ref, xs_ref, mod_ref, wa, wb, wc, wd, wg, wgt, pa, pb, pc, pd, pg, pgt, *, n_ptiles):
    m = mod_ref[0]
    x = _rows(pl.program_id(0) < n_ptiles, xp_ref, xs_ref)
    u = layer_norm(x) * (1.0 + m[1:2]) + m[0:1]
    ub = u.astype(BF)
    pa[...] = _dot(ub, wa[...], _NN)
    pb[...] = _dot(ub, wb[...], _NN)
    pc[...] = _dot(ub, wc[...], _NN)
    pd[...] = _dot(ub, wd[...], _NN)
    pg[...] = _dot(ub, wg[...], _NN)
    gt = _dot(wgt[...], ub, _NT)
    for j in range(ROW_TILE // CHUNK):
        pgt[j] = gt[:, j * CHUNK:(j + 1) * CHUNK]


def post_kernel(hp, hs, yap, yas, ybp, ybs, ycp, ycs, ydp, yds, mod_ref, wout, l1g, l1b, w1, w2, l2g, l2b,
                *o_refs, n_ptiles):
    m = mod_ref[0]
    is_ctx = pl.program_id(0) < n_ptiles
    y = jnp.concatenate([_rows(is_ctx, yap, yas), _rows(is_ctx, ybp, ybs), _rows(is_ctx, ycp, ycs),
                         _rows(is_ctx, ydp, yds)], axis=1)
    attn = _dot(y, wout[...], _NN)
    h1 = layer_norm(DEEPNORM_ALPHA * _rows(is_ctx, hp, hs) + m[2:3] * attn) * l1g[...] + l1b[...]
    u2 = layer_norm(h1) * (1.0 + m[4:5]) + m[3:4]
    f = _dot(u2.astype(BF), w1[...], _NN)
    f = jnp.square(jnp.maximum(f, 0.0))
    f2 = _dot(f.astype(BF), w2[...], _NN)
    out = layer_norm(DEEPNORM_ALPHA * h1 + m[5:6] * f2) * l2g[...] + l2b[...]
    if len(o_refs) == 1:
        o_refs[0][...] = out
    else:
        @pl.when(is_ctx)
        def _():
            o_refs[0][...] = out

        @pl.when(jnp.logical_not(is_ctx))
        def _():
            o_refs[1][...] = out


def _params(n_grid_axes):
    return pltpu.CompilerParams(dimension_semantics=("arbitrary",) * n_grid_axes,
                                vmem_limit_bytes=VMEM_LIMIT)


def _const_spec(a, layer=None):
    nd = a.ndim
    if layer is None:
        return pl.BlockSpec(a.shape, lambda *_: (0,) * nd, pipeline_mode=pl.Buffered(1))
    return pl.BlockSpec((None,) + a.shape[1:], lambda *_: (layer,) + (0,) * (nd - 1), pipeline_mode=pl.Buffered(1))


def _f32(shape):
    return jax.ShapeDtypeStruct(shape, jnp.float32)


def _mod_call(c_all, w_mod, b_mod):
    depth, d, n = w_mod.shape
    tn = 2048
    return pl.pallas_call(
        mod_kernel, grid=(depth, n // tn),
        in_specs=[pl.BlockSpec(c_all.shape, lambda l, j: (0, 0)),
                  pl.BlockSpec((1, d, tn), lambda l, j: (l, 0, j)),
                  pl.BlockSpec((1, 1, tn), lambda l, j: (l, 0, j))],
        out_specs=pl.BlockSpec((1, c_all.shape[0], tn), lambda l, j: (l, 0, j)),
        out_shape=_f32((depth, c_all.shape[0], n)),
        compiler_params=_params(2), name="mod")(c_all, w_mod, b_mod.reshape(depth, 1, n))


def _mod_spec(layer, n_ptiles, tiles_per_dec):
    def index(i):
        return (layer, jnp.where(i < n_ptiles, 0, 1 + (i - n_ptiles) // tiles_per_dec), 0, 0)
    return pl.BlockSpec((None, 1, 6, D_MODEL), index)


def _pair_specs(pair, n_ptiles):
    width = pair[0].shape[1]
    off = n_ptiles if pair[0] is pair[1] else 0
    return [pl.BlockSpec((ROW_TILE, width), lambda i: (jnp.minimum(i, n_ptiles - 1), 0)),
            pl.BlockSpec((ROW_TILE, width), lambda i: (jnp.maximum(i - n_ptiles, 0) + off, 0))]


def _proj_call(x_pair, n_rows, mod, layer, ws, n_ptiles, tiles_per_dec):
    row = lambda i: (i, 0)
    widths = [w.shape[-1] for w in ws[:5]]
    per_tile = ROW_TILE // CHUNK
    return pl.pallas_call(
        functools.partial(proj_kernel, n_ptiles=n_ptiles), grid=(n_rows // ROW_TILE,),
        in_specs=_pair_specs(x_pair, n_ptiles) + [_mod_spec(layer, n_ptiles, tiles_per_dec)]
                 + [_const_spec(w, layer) for w in ws],
        out_specs=[pl.BlockSpec((ROW_TILE, w), row) for w in widths]
                  + [pl.BlockSpec((per_tile, N_GATES, CHUNK), lambda i: (i, 0, 0))],
        out_shape=[_f32((n_rows, w)) for w in widths] + [_f32((n_rows // CHUNK, N_GATES, CHUNK))],
        compiler_params=_params(1), name="proj")(*x_pair, mod, *ws)


def _post_call(h_pair, y_pairs, n_rows, mod, layer, ws, n_ptiles, tiles_per_dec, split_out):
    pairs = [h_pair] + y_pairs
    if split_out:
        n_prows = n_ptiles * ROW_TILE
        out_specs = _pair_specs((_f32((n_prows, D_MODEL)), _f32((n_rows - n_prows, D_MODEL))), n_ptiles)
        out_shape = [_f32((n_prows, D_MODEL)), _f32((n_rows - n_prows, D_MODEL))]
    else:
        out_specs = pl.BlockSpec((ROW_TILE, D_MODEL), lambda i: (i, 0))
        out_shape = _f32((n_rows, D_MODEL))
    return pl.pallas_call(
        functools.partial(post_kernel, n_ptiles=n_ptiles), grid=(n_rows // ROW_TILE,),
        in_specs=[spec for p in pairs for spec in _pair_specs(p, n_ptiles)]
                 + [_mod_spec(layer, n_ptiles, tiles_per_dec)] + [_const_spec(w, layer) for w in ws],
        out_specs=out_specs, out_shape=out_shape,
        compiler_params=_params(1), name="post")(*[a for p in pairs for a in p], mod, *ws)


MAT = (2, N_HEADS, HEAD_DIM, HEAD_DIM)
VEC = (2, N_HEADS, 1, HEAD_DIM)
SCL = (2, N_HEADS, 1, 1)


def _mixer_call(kernel, name, batching, t_len, n_seq, row_off, seq_inputs, const_inputs, state_inputs, layer,
                state_tails, scratch):
    nc = t_len // CHUNK
    g = _scan_steps(nc, batching[0])
    ns = max(1, min(n_seq, batching[1] // (2 * g)))
    assert n_seq % ns == 0 and row_off % (ns * t_len) == 0
    blk_off = row_off // (ns * t_len)
    in_specs = []
    for a in seq_inputs:
        if a.ndim == 2:
            in_specs.append(pl.BlockSpec((ns * t_len, a.shape[1]), lambda i: (i + blk_off, 0)))
        else:
            in_specs.append(pl.BlockSpec((ns * nc,) + a.shape[1:], lambda i: (i + blk_off, 0, 0)))
    in_specs += [_const_spec(a, lay) for a, lay in const_inputs]
    for a in state_inputs:
        tail = a.shape[2:]
        in_specs.append(pl.BlockSpec((ns, 1) + tail, lambda i, nd=len(tail): (i, layer) + (0,) * nd))
    out_specs = [pl.BlockSpec((ns * t_len, GROUP_W), lambda i: (i, 0))]
    out_specs += [pl.BlockSpec((ns, 1) + t, lambda i, nd=len(t): (i, 0) + (0,) * nd) for t in state_tails]
    out_shape = [jax.ShapeDtypeStruct((n_seq * t_len, GROUP_W), BF)] + [_f32((n_seq, 1) + t) for t in state_tails]
    return pl.pallas_call(
        functools.partial(kernel, nc=nc, g=g, ns=ns), grid=(n_seq // ns,),
        in_specs=in_specs, out_specs=out_specs, out_shape=out_shape,
        scratch_shapes=[pltpu.VMEM((ns * t_len, GROUP_W), jnp.float32)] + scratch(ns),
        compiler_params=_params(1), name=name)(*seq_inputs, *[a for a, _ in const_inputs], *state_inputs)


def _bd_state(ns):
    return pltpu.VMEM((2 * ns, GROUP_W, GROUP_W), jnp.float32)


def _rope_tables(t_len):
    nf = HEAD_DIM // 4
    inv = ROPE_BASE ** (-jnp.arange(nf, dtype=jnp.float32) / nf)
    t = jnp.arange(t_len)
    ang_r = (t // GRID_W).astype(jnp.float32)[:, None] * inv[None, :]
    ang_c = (t % GRID_W).astype(jnp.float32)[:, None] * inv[None, :]
    cos = jnp.concatenate([jnp.cos(ang_r)] * 2 + [jnp.cos(ang_c)] * 2, axis=1)
    sin = jnp.concatenate([-jnp.sin(ang_r), jnp.sin(ang_r), -jnp.sin(ang_c), jnp.sin(ang_c)], axis=1)
    return jnp.tile(cos, (1, N_HEADS)), jnp.tile(sin, (1, N_HEADS))


def kernel(x_prompt, x_sample, c, state_mlstm_C, state_mlstm_n, state_mlstm_m, state_delta, state_ret,
           state_rwkv, c_ctx, w_mod, b_mod, w_in, w_out, ln1_g, ln1_b, ln2_g, ln2_b, w_ff1, w_ff2,
           mlstm_i_bias, mlstm_f_bias, mlstm_norm, delta_conv, delta_a_log, delta_dt_bias, delta_norm,
           ret_decay, ret_norm, rwkv_mu, rwkv_w0, rwkv_w2, rwkv_a0, rwkv_a2, rwkv_g2, rwkv_kk, rwkv_ka,
           rwkv_rk, rwkv_norm):
    f32 = jnp.float32
    batch, seq, d = x_prompt.shape
    dec_batch, dec_seq, _ = x_sample.shape
    assert d == D_MODEL and (batch * seq) % ROW_TILE == 0 and dec_seq % ROW_TILE == 0 and seq % CHUNK == 0
    assert (seq // CHUNK) % 2 == 0 and (dec_seq // CHUNK) % 2 == 0
    n_prows = batch * seq
    n_ptiles = n_prows // ROW_TILE
    tiles_per_dec = dec_seq // ROW_TILE

    c_all = jnp.zeros((8, d), f32).at[0].set(c_ctx).at[1:1 + dec_batch].set(c)
    mod = _mod_call(c_all, w_mod, b_mod).reshape(DEPTH, 8, 6, d)
    n_rows = n_prows + dec_batch * dec_seq
    h_pair = (x_prompt.reshape(n_prows, d), x_sample.reshape(dec_batch * dec_seq, d))
    cos_t, sin_t = _rope_tables(dec_seq)

    zmat = jnp.zeros((batch, 1) + MAT, f32)
    zn = jnp.zeros((batch, 1) + VEC, f32)
    zm = jnp.zeros((batch, 1) + SCL, f32)
    n_cache = state_mlstm_n.reshape(state_mlstm_n.shape[:-1] + (1, HEAD_DIM))
    m_cache = state_mlstm_m.reshape(state_mlstm_m.shape + (1, 1))
    groups = ((seq, batch, 0), (dec_seq, dec_batch, n_prows))

    bf = lambda a: a.astype(BF)
    rows = lambda a: a.reshape(DEPTH, 1, -1)
    zeros = lambda *shape: jnp.zeros((DEPTH,) + shape, f32)
    wg = bf(jnp.pad(jnp.concatenate([w_in[:, :, 1024:1040], w_in[:, :, 2064:2080]], axis=2),
                    ((0, 0), (0, 0), (0, GATE_LANES - N_GATES))))
    proj_ws = [bf(w_in[:, :, 0:1024]), bf(w_in[:, :, 1040:2064]), bf(w_in[:, :, 2080:3104]), bf(w_in[:, :, 3104:4256]),
               wg, jnp.swapaxes(wg[:, :, :N_GATES], 1, 2)]
    post_ws = [bf(w_out), rows(ln1_g), rows(ln1_b), bf(w_ff1), bf(w_ff2), rows(ln2_g), rows(ln2_b)]
    bias_c = jnp.concatenate([rows(mlstm_i_bias), rows(mlstm_f_bias), zeros(1, GATE_LANES - 16)], axis=2)
    bias_r = jnp.swapaxes(bias_c[:, :, :N_GATES], 1, 2)
    dl_row = jnp.concatenate([zeros(2, 24), jnp.concatenate([rows(delta_a_log), rows(delta_dt_bias)], axis=1),
                              zeros(2, GATE_LANES - 32)], axis=2)
    dl_col = jnp.swapaxes(dl_row[:, :, :N_GATES], 1, 2)
    rd_row = jnp.concatenate([rows(ret_decay), zeros(1, GATE_LANES - 8)], axis=2)
    rw_vec = jnp.concatenate([rows(rwkv_kk), rows(rwkv_ka), rows(rwkv_rk), rows(rwkv_norm)], axis=1)

    new_states = []
    for l in range(DEPTH):
        lay_of = lambda *arrs: [(a, l) for a in arrs]
        pa, pb, pc, pd, pg, pgt = _proj_call(h_pair, n_rows, mod, l, proj_ws, n_ptiles, tiles_per_dec)
        ys = [[], [], [], []]
        layer_states = None
        for gi, (t_len, n_seq, off) in enumerate(groups):
            ctx = gi == 0
            lay = 0 if ctx else l
            ya, c1, n1, m1 = _mixer_call(
                mlstm_kernel, "mlstm", SMALL_BATCH, t_len, n_seq, off, [pa, pg, pgt],
                lay_of(bias_c, bias_r, rows(mlstm_norm)),
                [zmat, zn, zm] if ctx else [state_mlstm_C, n_cache, m_cache], lay, [MAT, VEC, SCL],
                lambda ns: [_bd_state(ns), pltpu.VMEM((2 * ns, 1, GROUP_W), f32), pltpu.VMEM((2 * ns, 1, GROUP_W), f32)])
            yb, sb1 = _mixer_call(
                delta_kernel, "delta", LARGE_BATCH, t_len, n_seq, off, [pb, pg, pgt],
                lay_of(delta_conv, dl_row, dl_col, rows(delta_norm)),
                [zmat if ctx else state_delta], lay, [MAT], lambda ns: [_bd_state(ns)])
            ret_consts = lay_of(rd_row, rows(ret_norm))
            if not ctx:
                ret_consts = [(cos_t, None), (sin_t, None)] + ret_consts
            yc, sc1 = _mixer_call(
                functools.partial(ret_kernel, rope=not ctx), "ret", LARGE_BATCH, t_len, n_seq, off, [pc],
                ret_consts, [zmat if ctx else state_ret], lay, [MAT],
                lambda ns: [_bd_state(ns), pltpu.VMEM((2, 4, CHUNK, GROUP_W), f32)])
            yd, sd1 = _mixer_call(
                rwkv_kernel, "rwkv", LARGE_BATCH, t_len, n_seq, off, [pd],
                lay_of(rows(rwkv_mu), rwkv_w0, rwkv_w2, rwkv_a0, rwkv_a2, rwkv_g2, rw_vec),
                [zmat if ctx else state_rwkv], lay, [MAT], lambda ns: [_bd_state(ns)])
            for lst, y in zip(ys, (ya, yb, yc, yd)):
                lst.append(y)
            if ctx:
                layer_states = (c1, n1, m1, sb1, sc1, sd1)
        new_states.append(layer_states)
        last = l == DEPTH - 1
        h_new = _post_call(h_pair, [tuple(y) for y in ys], n_rows, mod, l, post_ws, n_ptiles, tiles_per_dec, last)
        h_pair = tuple(h_new) if last else (h_new, h_new)

    outs = [jnp.concatenate([st[i] for st in new_states], axis=1) for i in range(6)]
    outs[1] = outs[1].reshape(outs[1].shape[:4] + (HEAD_DIM,))
    outs[2] = outs[2].reshape(outs[2].shape[:4])
    return (h_pair[0].reshape(batch, seq, d), h_pair[1].reshape(dec_batch, dec_seq, d), *outs)
```

```python
import functools

import jax
import jax.numpy as jnp
from jax import lax
from jax.experimental import pallas as pl
from jax.experimental.pallas import tpu as pltpu

D_MODEL = 1024
DEPTH = 2
GRID_W = 64
HEAD_DIM = 64
N_HEADS = 4
GROUP_W = N_HEADS * HEAD_DIM
CHUNK = 64
CONV_K = 5
ROPE_BASE = 10000.0
LN_EPS = 1e-5
DEEPNORM_ALPHA = (2 * DEPTH) ** 0.25
RWKV_LORA = 64
GATE_LANES = 128
N_GATES = 32
ROW_TILE = 512
VMEM_LIMIT = 56 * 1024 * 1024
SMALL_BATCH = (2, 8)
LARGE_BATCH = (4, 16)
HALO = CHUNK + 16

NEG_INF = float("-inf")
BF = jnp.bfloat16
_NN = (((1,), (0,)), ((), ()))
_NT = (((1,), (1,)), ((), ()))
_TN = (((0,), (0,)), ((), ()))


def _dot(a, b, dims):
    return lax.dot_general(a, b, dims, preferred_element_type=jnp.result_type(a.dtype, jnp.float32))


def _bf(x):
    return x.astype(BF)


def _split(x, n):
    parts = []
    for _ in range(n - 1):
        p = _bf(x)
        parts.append(p)
        x = x - p.astype(x.dtype)
    parts.append(_bf(x))
    return parts


def bmm(a, b, dims=_NN):
    return _dot(_bf(a), _bf(b), dims)


def _expand(x, bdm):
    lane_head = jnp.right_shift(lax.broadcasted_iota(jnp.int32, (CHUNK, GROUP_W), 1), 6)
    zero = jnp.zeros((), x.dtype)
    return jnp.concatenate([jnp.where(lane_head == h, x, zero) for h in range(N_HEADS)], axis=0)


def wmm(a, b, bdm, passes=1, nt=False):
    dims = _NT if nt else _NN
    if passes == 1:
        return _dot(_bf(a), _expand(_bf(b), bdm), dims)
    ah, al = _split(a, 2)
    bh, bl = _split(b, 2)
    ebh = _expand(bh, bdm)
    lhs = jnp.concatenate([ah, ah, al], axis=1)
    rhs = jnp.concatenate([ebh, _expand(bl, bdm), ebh], axis=1 if nt else 0)
    return _dot(lhs, rhs, dims)


def wmm2(a, b1, b2, bdm, nt=False):
    e1, e2 = _expand(_bf(b1), bdm), _expand(_bf(b2), bdm)
    out = _dot(_bf(a), jnp.concatenate([e1, e2], axis=0 if nt else 1), _NT if nt else _NN)
    return out[:, :GROUP_W], out[:, GROUP_W:]


def cumsum_col(mc, x):
    return _dot(jnp.concatenate([_bf(mc)] * 3, axis=1), jnp.concatenate(_split(x, 3), axis=0), _NN)


def cumsum_row(x, mc):
    return _dot(jnp.concatenate(_split(x, 3), axis=1), jnp.concatenate([_bf(mc)] * 3, axis=1), _NT)


def head_sum(x, bd):
    return _dot(jnp.concatenate(_split(x, 2), axis=1), jnp.concatenate([_bf(bd)] * 2, axis=0), _NN)


def sigmoid(x):
    return 0.5 + 0.5 * jnp.tanh(0.5 * x)


def silu(x):
    return x * sigmoid(x)


def softplus(x):
    return jnp.maximum(x, 0.0) + jnp.log(1.0 + jnp.exp(-jnp.abs(x)))


def log_sigmoid(x):
    return -softplus(-x)


def head_norm(y, gain, bd, center):
    if center:
        y = y - head_sum(y, bd) * (1.0 / HEAD_DIM)
    ms = head_sum(y * y, bd) * (1.0 / HEAD_DIM)
    return y * lax.rsqrt(ms + LN_EPS) * gain


def l2norm_heads(x, bd):
    return x * lax.rsqrt(head_sum(x * x, bd) + 1e-6)


def layer_norm(x):
    mu = jnp.mean(x, axis=-1, keepdims=True)
    xc = x - mu
    var = jnp.mean(xc * xc, axis=-1, keepdims=True)
    return xc * lax.rsqrt(var + LN_EPS)


def colw(x, idx):
    m = x.shape[0]
    return jnp.concatenate([jnp.broadcast_to(x[:, j:j + 1], (m, HEAD_DIM)) for j in idx], axis=1)


def roww(x, idx):
    return jnp.concatenate([x[j:j + 1, :] for j in idx], axis=1)


def segmax(x):
    m = x.shape[0]
    return jnp.concatenate(
        [jnp.broadcast_to(jnp.max(x[:, h * HEAD_DIM:(h + 1) * HEAD_DIM], axis=1, keepdims=True), (m, HEAD_DIM))
         for h in range(N_HEADS)], axis=1)


def wide_masks(rev, dtype=jnp.float32):
    r = lax.broadcasted_iota(jnp.int32, (CHUNK, GROUP_W), 0)
    c = jnp.bitwise_and(lax.broadcasted_iota(jnp.int32, (CHUNK, GROUP_W), 1), HEAD_DIM - 1)
    if rev:
        incl_w, strict_w = r <= c, r < c
    else:
        incl_w, strict_w = r >= c, r > c
    mc = incl_w[:, :CHUNK].astype(dtype)
    return incl_w, strict_w, mc, (r == c).astype(dtype)


def block_diag_mask():
    r = jnp.right_shift(lax.broadcasted_iota(jnp.int32, (GROUP_W, GROUP_W), 0), 6)
    c = jnp.right_shift(lax.broadcasted_iota(jnp.int32, (GROUP_W, GROUP_W), 1), 6)
    m = (r == c).astype(jnp.float32)
    return _bf(m), m


def inv_unit_tri_w(a, eye_w, bdm):
    p = _bf(-a)
    e = _expand(p, bdm)
    t = eye_w - a
    for _ in range(5):
        p = _bf(_dot(p, e, _NN))
        e = _expand(p, bdm)
        t = t + _dot(_bf(t), e, _NN)
    r = eye_w - t - wmm(a, t, bdm, 3)
    return t + wmm(t, r, bdm, 1)


def mlstm_pre(q, k, v, bcol_w, brow_w, icol_w, irow_w, bl_w, incl_w, bdm):
    logd = jnp.where(incl_w, bcol_w - brow_w + irow_w, NEG_INF)
    mt0 = segmax(logd)
    s0 = wmm(q, k, bdm, 1, nt=True) * jnp.exp(logd - mt0)
    wk = bl_w - bcol_w + icol_w
    wkmax = jnp.max(wk, axis=0, keepdims=True)
    ks0 = k * jnp.exp(wk - wkmax)
    return (q, bcol_w, bl_w, mt0, wmm(s0, v, bdm, 1), head_sum(s0, bdm[1]), wkmax,
            _bf(ks0), _bf(v), jnp.sum(ks0, axis=0, keepdims=True))


def mlstm_step(q, bcol_w, bl_w, mt0, num0, den0, wkmax, ks0, v, ksum0, c_bd, n_row, m_w, bdm):
    inter = bcol_w + m_w
    mt = jnp.maximum(inter, mt0)
    f = jnp.exp(mt0 - mt)
    ew = jnp.exp(inter - mt)
    num = num0 * f + bmm(q, c_bd) * ew
    den = den0 * f + head_sum(q * n_row, bdm[1]) * ew
    h = num / jnp.maximum(jnp.abs(den), jnp.exp(-mt))
    m_new = jnp.maximum(bl_w + m_w, wkmax)
    sp = jnp.exp(bl_w + m_w - m_new)
    sc = jnp.exp(wkmax - m_new)
    kv0 = bmm(ks0, v, _TN) * bdm[1]
    return h, c_bd * sp + kv0 * sc, n_row * sp + ksum0 * sc, m_new


def delta_pre(q_raw, k_raw, v, beta_w, gcol_w, grow_w, gl_w, incl_w, strict_w, eye_w, bdm):
    q = l2norm_heads(q_raw, bdm[1]) * HEAD_DIM ** -0.5
    k = l2norm_heads(k_raw, bdm[1])
    decay = jnp.exp(jnp.where(incl_w, gcol_w - grow_w, NEG_INF))
    a = jnp.where(strict_w, wmm(k, k, bdm, 1, nt=True) * decay, 0.0) * beta_w
    t = inv_unit_tri_w(a, eye_w, bdm)
    eg = jnp.exp(gcol_w)
    u, w = wmm2(t, v * beta_w, k * (beta_w * eg), bdm)
    qk = wmm(q, k, bdm, 1, nt=True) * decay
    return (u, _bf(jnp.concatenate([w, q * eg], axis=0)), _bf(qk), _bf(k * jnp.exp(gl_w - gcol_w)),
            jnp.exp(gl_w))


def delta_step(u, wq, qk, kd, egl, s_bd, bdm):
    ws = bmm(wq, s_bd)
    v_new = u - ws[:CHUNK]
    o = ws[CHUNK:] + wmm(qk, v_new, bdm, 1)
    s_new = s_bd * egl + bmm(kd, v_new, _TN) * bdm[1]
    return o, s_new


def ret_pre(q, k, v, dmat_w, zeta_w, bdm):
    sc = wmm(q, k, bdm, 1, nt=True) * dmat_w
    return _bf(q), wmm(sc, v, bdm, 1), _bf(k * zeta_w), _bf(v)


def ret_step(q, o1, kz, v, xi_w, gl_w, s_bd, bdm):
    return o1 + bmm(q, s_bd) * xi_w, s_bd * gl_w + bmm(kz, v, _TN) * bdm[1]


def rwkv_pre(r, kkk, kt, a, v, cw, lw, tot, incl_w, strict_w, eye_w, bdm):
    kh = l2norm_heads(kkk, bdm[1])
    b = a * kh
    en = jnp.exp(-cw)
    et = jnp.exp(tot - cw)
    rp = r * jnp.exp(cw)
    khp = kh * jnp.exp(cw - lw)
    ktn = kt * en
    bn = b * en
    lhs = jnp.concatenate([khp, rp], axis=0)
    ab_kt, ab_b = wmm2(lhs, ktn, bn, bdm, nt=True)
    a_kt = jnp.where(strict_w, ab_kt[:CHUNK], 0.0)
    b_kt = jnp.where(incl_w, ab_kt[CHUNK:], 0.0)
    a_b = jnp.where(strict_w, ab_b[:CHUNK], 0.0)
    b_b = jnp.where(incl_w, ab_b[CHUNK:], 0.0)
    t = inv_unit_tri_w(a_b, eye_w, bdm)
    tk = wmm(t, khp, bdm, 1)
    tav = wmm(t, wmm(a_kt, v, bdm, 1), bdm, 1)
    return (_bf(jnp.concatenate([tk, rp], axis=0)), tav, wmm(b_kt, v, bdm, 1), _bf(b_b),
            _bf(jnp.concatenate([kt * et, b * et], axis=0)))


def rwkv_step(tkr, tav, o1, b_b, kbe, v, etot, s_bd, bdm):
    ls = bmm(tkr, s_bd, _NT)
    u = ls[:CHUNK] + tav
    o = ls[CHUNK:] + o1 - wmm(b_b, u, bdm, 1)
    upd = bmm(jnp.concatenate([v, -u], axis=0), kbe, _TN)
    s_new = s_bd * etot + upd * bdm[1]
    return o, s_new


def _group_loop(nc, g, body):
    n_it = nc // g

    def run(first):
        def it(i, carry):
            body(i, first)
            return carry
        return it

    lax.fori_loop(0, n_it // 2, run(True), 0)
    lax.fori_loop(n_it // 2, n_it, run(False), 0)


def _scan_steps(nc, max_steps):
    return max(g for g in (1, 2, 4, 8) if g <= max_steps and nc % (2 * g) == 0)


def _group_chunks(i, g, nc, ns):
    out = []
    for s in range(ns):
        for z in range(2):
            for j in range(g):
                step = i * g + j
                c = s * nc + (step if z == 0 else nc - 1 - step)
                out.append((s, z, c, pl.multiple_of(c * CHUNK, CHUNK)))
    return out


def _group_masks(g, ns):
    shape = (2 * g * ns, CHUNK, GROUP_W)
    lane = jnp.right_shift(lax.broadcasted_iota(jnp.int32, shape, 0), g.bit_length() - 1)
    r = lax.broadcasted_iota(jnp.int32, shape, 1)
    c = jnp.bitwise_and(lax.broadcasted_iota(jnp.int32, shape, 2), HEAD_DIM - 1)
    d = jnp.where(jnp.bitwise_and(lane, 1) == 1, c - r, r - c)
    return d >= 0, d > 0, (d == 0).astype(jnp.float32)


def _stack(items):
    return [jnp.stack(x) for x in zip(*items)]


def _pick(arrs, g, j):
    return [jnp.stack([a[k] for k in range(j, a.shape[0], g)]) for a in arrs]


def _halo(ref, s, c, t0, nc):
    rows = ref.shape[0]
    ps = pl.multiple_of(jnp.maximum(t0 - 8, 0), 8)
    nx = pl.multiple_of(jnp.minimum(t0 + CHUNK, rows - 8), 8)
    prev = jnp.where(c > s * nc, ref[pl.ds(ps, 8), :], 0.0)
    nxt = jnp.where(c < s * nc + nc - 1, ref[pl.ds(nx, 8), :], 0.0)
    return jnp.concatenate([prev, ref[pl.ds(t0, CHUNK), :], nxt], axis=0)


def _shift_matrix(offsets):
    rows = len(offsets) * CHUNK
    r = lax.broadcasted_iota(jnp.int32, (rows, 2 * HALO), 0)
    c = lax.broadcasted_iota(jnp.int32, (rows, 2 * HALO), 1)
    c = jnp.where(c >= HALO, c - HALO, c)
    t = jnp.bitwise_and(r, CHUNK - 1)
    target = t + 8
    for j, off in enumerate(offsets):
        target = jnp.where(jnp.right_shift(r, 6) == j, target + off, target)
    return _bf((c == target).astype(jnp.float32))


def _hs(h):
    return slice(h * HEAD_DIM, (h + 1) * HEAD_DIM)


def _load_state(sbd, s0_ref, ns):
    for s in range(ns):
        for z in range(2):
            sbd[2 * s + z] = jnp.zeros((GROUP_W, GROUP_W), jnp.float32)
            for h in range(N_HEADS):
                sbd[2 * s + z, _hs(h), _hs(h)] = s0_ref[s, 0, z, h]


def _store_state(s1_ref, sbd, ns):
    for s in range(ns):
        for z in range(2):
            for h in range(N_HEADS):
                s1_ref[s, 0, z, h] = sbd[2 * s + z, _hs(h), _hs(h)]


def _merge(first, hscr, t0, h):
    if first:
        hscr[pl.ds(t0, CHUNK), :] = h
        return None
    return hscr[pl.ds(t0, CHUNK), :] + h


def _scan(nc, g, ns, prep, pre_fn, n_masks, step_fn, state_refs, hscr, finish, y_ref):
    bdm = block_diag_mask()
    masks = _group_masks(g, ns)[:n_masks]

    def body(i, first):
        chunks = _group_chunks(i, g, nc, ns)
        preps = [prep(s, z, c, t0, first) for s, z, c, t0 in chunks]
        n_in = len(preps[0][0])
        pre = jax.vmap(pre_fn, in_axes=(0,) * (n_in + n_masks) + (None,))(
            *_stack([p[0] for p in preps]), *masks, bdm)
        extra = _stack([p[1] for p in preps]) if preps[0][1] else []
        states = [ref[...] for ref in state_refs]
        for j in range(g):
            args = _pick(list(pre), g, j) + _pick(extra, g, j) + states
            res = jax.vmap(step_fn, in_axes=(0,) * len(args) + (None,))(*args, bdm)
            states = list(res[1:])
            ks = [lane * g + j for lane in range(2 * ns)]
            tots = [_merge(first, hscr, chunks[k][3], res[0][lane]) for lane, k in enumerate(ks)]
            if not first:
                ys = jax.vmap(finish)(jnp.stack(tots), *_stack([preps[k][2] for k in ks]))
                for lane, k in enumerate(ks):
                    y_ref[pl.ds(chunks[k][3], CHUNK), :] = ys[lane]
        for ref, val in zip(state_refs, states):
            ref[...] = val

    _group_loop(nc, g, body)


def mlstm_kernel(p_ref, gc_ref, gr_ref, bc_ref, br_ref, norm_ref, c0_ref, n0_ref, m0_ref,
                 y_ref, c1_ref, n1_ref, m1_ref, hscr, cbd, nrow, mw, *, nc, g, ns):
    _load_state(cbd, c0_ref, ns)
    for s in range(ns):
        for z in range(2):
            for h in range(N_HEADS):
                nrow[2 * s + z, :, _hs(h)] = n0_ref[s, 0, z, h]
                mw[2 * s + z, :, _hs(h)] = jnp.broadcast_to(m0_ref[s, 0, z, h], (1, HEAD_DIM))
    bd = block_diag_mask()[1]
    mcs = [wide_masks(False)[2], wide_masks(True)[2]]
    gain = norm_ref[...]

    def prep(s, z, c, t0, first):
        x = p_ref[pl.ds(t0, CHUNK), 0:3 * GROUP_W]
        g_c = gc_ref[pl.ds(t0, CHUNK), :] + bc_ref[...]
        g_r = gr_ref[c] + br_ref[...]
        lf_c = log_sigmoid(g_c)
        idx_i = [z * N_HEADS + h for h in range(N_HEADS)]
        idx_f = [8 + j for j in idx_i]
        fin = () if first else (p_ref[pl.ds(t0, CHUNK), 3 * GROUP_W:],)
        return ((x[:, :GROUP_W], x[:, GROUP_W:2 * GROUP_W] * HEAD_DIM ** -0.5, x[:, 2 * GROUP_W:],
                 colw(cumsum_col(mcs[z], lf_c), idx_f), roww(cumsum_row(log_sigmoid(g_r), mcs[z]), idx_f),
                 colw(g_c, idx_i), roww(g_r, idx_i), colw(jnp.sum(lf_c, axis=0, keepdims=True), idx_f)), (), fin)

    def finish(tot, og):
        return _bf(sigmoid(og) * head_norm(tot, gain, bd, True))

    _scan(nc, g, ns, prep, mlstm_pre, 1, mlstm_step, [cbd, nrow, mw], hscr, finish, y_ref)
    _store_state(c1_ref, cbd, ns)
    for s in range(ns):
        for z in range(2):
            for h in range(N_HEADS):
                n1_ref[s, 0, z, h] = nrow[2 * s + z, :, _hs(h)]
                m1_ref[s, 0, z, h] = mw[2 * s + z, :, h * HEAD_DIM:h * HEAD_DIM + 1]


def delta_kernel(p_ref, gc_ref, gr_ref, conv_ref, prow_ref, pcol_ref, norm_ref, s0_ref,
                 y_ref, s1_ref, hscr, sbd, *, nc, g, ns):
    _load_state(sbd, s0_ref, ns)
    bd = block_diag_mask()[1]
    mcs = [wide_masks(False)[2], wide_masks(True)[2]]
    nega_row = -jnp.exp(prow_ref[0:1, :])
    dtb_row = prow_ref[1:2, :]
    nega_col = -jnp.exp(pcol_ref[:, 0:1])
    dtb_col = pcol_ref[:, 1:2]
    qkv_w = 3 * GROUP_W
    gain = norm_ref[...]
    offsets = [j - CONV_K // 2 for j in range(CONV_K) if j != CONV_K // 2]
    taps = _shift_matrix(offsets)

    def prep(s, z, c, t0, first):
        ext = _halo(p_ref.at[:, 0:qkv_w], s, c, t0, nc)
        shifted = _dot(taps, jnp.concatenate(_split(ext, 2), axis=0), _NN)
        acc = ext[8:8 + CHUNK] * conv_ref[CONV_K // 2:CONV_K // 2 + 1, :]
        for i, off in enumerate(offsets):
            j = off + CONV_K // 2
            acc = acc + shifted[i * CHUNK:(i + 1) * CHUNK] * conv_ref[j:j + 1, :]
        qkv = silu(acc)
        g_c = gc_ref[pl.ds(t0, CHUNK), :]
        gg_c = nega_row * softplus(g_c + dtb_row)
        gg_r = nega_col * softplus(gr_ref[c] + dtb_col)
        idx_b = [16 + z * N_HEADS + h for h in range(N_HEADS)]
        idx_a = [8 + j for j in idx_b]
        fin = () if first else (p_ref[pl.ds(t0, CHUNK), qkv_w:],)
        return ((qkv[:, :GROUP_W], qkv[:, GROUP_W:2 * GROUP_W], qkv[:, 2 * GROUP_W:], colw(sigmoid(g_c), idx_b),
                 colw(cumsum_col(mcs[z], gg_c), idx_a), roww(cumsum_row(gg_r, mcs[z]), idx_a),
                 colw(jnp.sum(gg_c, axis=0, keepdims=True), idx_a)), (), fin)

    def finish(tot, zg):
        return _bf(head_norm(tot, gain, bd, False) * silu(zg))

    _scan(nc, g, ns, prep, delta_pre, 3, delta_step, [sbd], hscr, finish, y_ref)
    _store_state(s1_ref, sbd, ns)


def _rope(x, cos, sin, lane_lo):
    n = x.shape[1]
    swapped = jnp.where(lane_lo, pltpu.roll(x, n - 16, 1), pltpu.roll(x, 16, 1))
    return x * cos + swapped * sin


def ret_kernel(*refs, nc, g, ns, rope):
    if rope:
        p_ref, cos_ref, sin_ref, rd_ref, norm_ref, s0_ref, y_ref, s1_ref, hscr, sbd, cst = refs
    else:
        p_ref, rd_ref, norm_ref, s0_ref, y_ref, s1_ref, hscr, sbd, cst = refs
    f32 = jnp.float32
    _load_state(sbd, s0_ref, ns)
    bd = block_diag_mask()[1]
    r = lax.broadcasted_iota(jnp.int32, (CHUNK, GROUP_W), 0).astype(f32)
    cc = jnp.bitwise_and(lax.broadcasted_iota(jnp.int32, (CHUNK, GROUP_W), 1), HEAD_DIM - 1).astype(f32)
    lg_row = -jnp.exp(rd_ref[...])
    gain = norm_ref[...]
    for z in range(2):
        incl_w = wide_masks(z == 1)[0]
        lg = colw(lg_row, [z * N_HEADS + h for h in range(N_HEADS)])
        if z == 0:
            cst[z, 0] = jnp.exp(jnp.where(incl_w, (r - cc) * lg, NEG_INF))
            cst[z, 1] = jnp.exp((r + 1.0) * lg)
            cst[z, 2] = jnp.exp((CHUNK - 1.0 - r) * lg)
        else:
            cst[z, 0] = jnp.exp(jnp.where(incl_w, (cc - r) * lg, NEG_INF))
            cst[z, 1] = jnp.exp((CHUNK - r) * lg)
            cst[z, 2] = jnp.exp(r * lg)
        cst[z, 3] = jnp.exp(CHUNK * lg + 0.0 * r)
    if rope:
        lane = lax.broadcasted_iota(jnp.int32, (CHUNK, GROUP_W), 1)
        lane_lo = jnp.bitwise_and(lane, 31) < 16

    def prep(s, z, c, t0, first):
        x = p_ref[pl.ds(t0, CHUNK), 0:3 * GROUP_W]
        q = x[:, :GROUP_W] * HEAD_DIM ** -0.5
        k = x[:, GROUP_W:2 * GROUP_W]
        if rope:
            tr = pl.multiple_of(t0 - s * nc * CHUNK, CHUNK)
            cos = cos_ref[pl.ds(tr, CHUNK), :]
            sin = sin_ref[pl.ds(tr, CHUNK), :]
            q = _rope(q, cos, sin, lane_lo)
            k = _rope(k, cos, sin, lane_lo)
        fin = () if first else (p_ref[pl.ds(t0, CHUNK), 3 * GROUP_W:],)
        return (q, k, x[:, 2 * GROUP_W:], cst[z, 0], cst[z, 2]), (cst[z, 1], cst[z, 3, 0:1, :]), fin

    def finish(tot, gate):
        return _bf(silu(gate) * head_norm(tot, gain, bd, True))

    _scan(nc, g, ns, prep, ret_pre, 0, ret_step, [sbd], hscr, finish, y_ref)
    _store_state(s1_ref, sbd, ns)


def rwkv_kernel(p_ref, mu_ref, w0_ref, w2_ref, a0_ref, a2_ref, g2_ref, vec_ref, s0_ref,
                y_ref, s1_ref, hscr, sbd, *, nc, g, ns):
    _load_state(sbd, s0_ref, ns)
    bd = block_diag_mask()[1]
    mcs = [wide_masks(False)[2], wide_masks(True)[2]]
    kk = vec_ref[0:1, :]
    ka = vec_ref[1:2, :]
    rk = vec_ref[2:3, :]
    gain = vec_ref[3:4, :]
    g2 = g2_ref[...]

    def lora_a(da, zz):
        return sigmoid(a0_ref[zz:zz + 1, :] + bmm(da[:, zz * RWKV_LORA:(zz + 1) * RWKV_LORA], a2_ref[zz]))

    def prep(s, z, c, t0, first):
        ext = _halo(p_ref, s, c, t0, nc)
        cur = ext[8:8 + CHUNK]
        nb = pltpu.roll(ext, 1, 0) + pltpu.roll(ext, HALO - 1, 0)
        pdm = cur + mu_ref[...] * (0.5 * nb[8:8 + CHUNK] - cur)
        r = pdm[:, :GROUP_W]
        k = pdm[:, GROUP_W:2 * GROUP_W]
        v = pdm[:, 2 * GROUP_W:3 * GROUP_W]
        dw = pdm[:, 3 * GROUP_W:3 * GROUP_W + 128]
        da = pdm[:, 3 * GROUP_W + 128:3 * GROUP_W + 256]
        a_z = lora_a(da, z)
        w_pre = w0_ref[z:z + 1, :] + bmm(jnp.tanh(dw[:, z * RWKV_LORA:(z + 1) * RWKV_LORA]), w2_ref[z])
        lw = -jnp.exp(-softplus(-w_pre) - 0.5)
        kt_z = k * (1.0 + (a_z - 1.0) * ka)
        tot = jnp.sum(lw, axis=0, keepdims=True)
        fin = ()
        if not first:
            kt_o = k * (1.0 + (lora_a(da, 1 - z) - 1.0) * ka)
            fin = (pdm[:, 3 * GROUP_W + 256:], r * (kt_z + kt_o) * rk, v)
        return (r, k * kk, kt_z, a_z, v, cumsum_col(mcs[z], lw), lw, tot), (v, jnp.exp(tot)), fin

    def finish(tot_h, dg, rkk, v):
        gate = bmm(sigmoid(dg), g2)
        return _bf(gate * (head_norm(tot_h, gain, bd, True) + head_sum(rkk, bd) * v))

    _scan(nc, g, ns, prep, rwkv_pre, 3, rwkv_step, [sbd], hscr, finish, y_ref)
    _store_state(s1_ref, sbd, ns)


def mod_kernel(c_ref, w_ref, b_ref, o_ref):
    o_ref[0] = bmm(silu(c_ref[...]), w_ref[0]) + b_ref[0]


def _rows(is_ctx, ctx_ref, lat_ref):
    return jnp.where(is_ctx, ctx_ref[...], lat_ref[...])


def proj_kernel(xp_ref, xs_ref, mod_ref, wa, wb, wc, wd, wg, wgt, pa, pb, pc, pd, pg, pgt, *, n_ptiles):
    m = mod_ref[0]
    x = _rows(pl.program_id(0) < n_ptiles, xp_ref, xs_ref)
    u = layer_norm(x) * (1.0 + m[1:2]) + m[0:1]
    ub = u.astype(BF)
    pa[...] = _dot(ub, wa[...], _NN)
    pb[...] = _dot(ub, wb[...], _NN)
    pc[...] = _dot(ub, wc[...], _NN)
    pd[...] = _dot(ub, wd[...], _NN)
    pg[...] = _dot(ub, wg[...], _NN)
    gt = _dot(wgt[...], ub, _NT)
    for j in range(ROW_TILE // CHUNK):
        pgt[j] = gt[:, j * CHUNK:(j + 1) * CHUNK]


def post_kernel(hp, hs, yap, yas, ybp, ybs, ycp, ycs, ydp, yds, mod_ref, wout, l1g, l1b, w1, w2, l2g, l2b,
                *o_refs, n_ptiles):
    m = mod_ref[0]
    is_ctx = pl.program_id(0) < n_ptiles
    y = jnp.concatenate([_rows(is_ctx, yap, yas), _rows(is_ctx, ybp, ybs), _rows(is_ctx, ycp, ycs),
                         _rows(is_ctx, ydp, yds)], axis=1)
    attn = _dot(y, wout[...], _NN)
    h1 = layer_norm(DEEPNORM_ALPHA * _rows(is_ctx, hp, hs) + m[2:3] * attn) * l1g[...] + l1b[...]
    u2 = layer_norm(h1) * (1.0 + m[4:5]) + m[3:4]
    f = _dot(u2.astype(BF), w1[...], _NN)
    f = jnp.square(jnp.maximum(f, 0.0))
    f2 = _dot(f.astype(BF), w2[...], _NN)
    out = layer_norm(DEEPNORM_ALPHA * h1 + m[5:6] * f2) * l2g[...] + l2b[...]
    if len(o_refs) == 1:
        o_refs[0][...] = out
    else:
        @pl.when(is_ctx)
        def _():
            o_refs[0][...] = out

        @pl.when(jnp.logical_not(is_ctx))
        def _():
            o_refs[1][...] = out


def _params(n_grid_axes):
    return pltpu.CompilerParams(dimension_semantics=("arbitrary",) * n_grid_axes,
                                vmem_limit_bytes=VMEM_LIMIT)


def _const_spec(a, layer=None):
    nd = a.ndim
    if layer is None:
        return pl.BlockSpec(a.shape, lambda *_: (0,) * nd, pipeline_mode=pl.Buffered(1))
    return pl.BlockSpec((None,) + a.shape[1:], lambda *_: (layer,) + (0,) * (nd - 1), pipeline_mode=pl.Buffered(1))


def _f32(shape):
    return jax.ShapeDtypeStruct(shape, jnp.float32)


def _mod_call(c_all, w_mod, b_mod):
    depth, d, n = w_mod.shape
    tn = 2048
    return pl.pallas_call(
        mod_kernel, grid=(depth, n // tn),
        in_specs=[pl.BlockSpec(c_all.shape, lambda l, j: (0, 0)),
                  pl.BlockSpec((1, d, tn), lambda l, j: (l, 0, j)),
                  pl.BlockSpec((1, 1, tn), lambda l, j: (l, 0, j))],
        out_specs=pl.BlockSpec((1, c_all.shape[0], tn), lambda l, j: (l, 0, j)),
        out_shape=_f32((depth, c_all.shape[0], n)),
        compiler_params=_params(2), name="mod")(c_all, w_mod, b_mod.reshape(depth, 1, n))


def _mod_spec(layer, n_ptiles, tiles_per_dec):
    def index(i):
        return (layer, jnp.where(i < n_ptiles, 0, 1 + (i - n_ptiles) // tiles_per_dec), 0, 0)
    return pl.BlockSpec((None, 1, 6, D_MODEL), index)


def _pair_specs(pair, n_ptiles):
    width = pair[0].shape[1]
    off = n_ptiles if pair[0] is pair[1] else 0
    return [pl.BlockSpec((ROW_TILE, width), lambda i: (jnp.minimum(i, n_ptiles - 1), 0)),
            pl.BlockSpec((ROW_TILE, width), lambda i: (jnp.maximum(i - n_ptiles, 0) + off, 0))]


def _proj_call(x_pair, n_rows, mod, layer, ws, n_ptiles, tiles_per_dec):
    row = lambda i: (i, 0)
    widths = [w.shape[-1] for w in ws[:5]]
    per_tile = ROW_TILE // CHUNK
    return pl.pallas_call(
        functools.partial(proj_kernel, n_ptiles=n_ptiles), grid=(n_rows // ROW_TILE,),
        in_specs=_pair_specs(x_pair, n_ptiles) + [_mod_spec(layer, n_ptiles, tiles_per_dec)]
                 + [_const_spec(w, layer) for w in ws],
        out_specs=[pl.BlockSpec((ROW_TILE, w), row) for w in widths]
                  + [pl.BlockSpec((per_tile, N_GATES, CHUNK), lambda i: (i, 0, 0))],
        out_shape=[_f32((n_rows, w)) for w in widths] + [_f32((n_rows // CHUNK, N_GATES, CHUNK))],
        compiler_params=_params(1), name="proj")(*x_pair, mod, *ws)


def _post_call(h_pair, y_pairs, n_rows, mod, layer, ws, n_ptiles, tiles_per_dec, split_out):
    pairs = [h_pair] + y_pairs
    if split_out:
        n_prows = n_ptiles * ROW_TILE
        out_specs = _pair_specs((_f32((n_prows, D_MODEL)), _f32((n_rows - n_prows, D_MODEL))), n_ptiles)
        out_shape = [_f32((n_prows, D_MODEL)), _f32((n_rows - n_prows, D_MODEL))]
    else:
        out_specs = pl.BlockSpec((ROW_TILE, D_MODEL), lambda i: (i, 0))
        out_shape = _f32((n_rows, D_MODEL))
    return pl.pallas_call(
        functools.partial(post_kernel, n_ptiles=n_ptiles), grid=(n_rows // ROW_TILE,),
        in_specs=[spec for p in pairs for spec in _pair_specs(p, n_ptiles)]
                 + [_mod_spec(layer, n_ptiles, tiles_per_dec)] + [_const_spec(w, layer) for w in ws],
        out_specs=out_specs, out_shape=out_shape,
        compiler_params=_params(1), name="post")(*[a for p in pairs for a in p], mod, *ws)


MAT = (2, N_HEADS, HEAD_DIM, HEAD_DIM)
VEC = (2, N_HEADS, 1, HEAD_DIM)
SCL = (2, N_HEADS, 1, 1)


def _mixer_call(kernel, name, batching, t_len, n_seq, row_off, seq_inputs, const_inputs, state_inputs, layer,
                state_tails, scratch):
    nc = t_len // CHUNK
    g = _scan_steps(nc, batching[0])
    ns = max(1, min(n_seq, batching[1] // (2 * g)))
    assert n_seq % ns == 0 and row_off % (ns * t_len) == 0
    blk_off = row_off // (ns * t_len)
    in_specs = []
    for a in seq_inputs:
        if a.ndim == 2:
            in_specs.append(pl.BlockSpec((ns * t_len, a.shape[1]), lambda i: (i + blk_off, 0)))
        else:
            in_specs.append(pl.BlockSpec((ns * nc,) + a.shape[1:], lambda i: (i + blk_off, 0, 0)))
    in_specs += [_const_spec(a, lay) for a, lay in const_inputs]
    for a in state_inputs:
        tail = a.shape[2:]
        in_specs.append(pl.BlockSpec((ns, 1) + tail, lambda i, nd=len(tail): (i, layer) + (0,) * nd))
    out_specs = [pl.BlockSpec((ns * t_len, GROUP_W), lambda i: (i, 0))]
    out_specs += [pl.BlockSpec((ns, 1) + t, lambda i, nd=len(t): (i, 0) + (0,) * nd) for t in state_tails]
    out_shape = [jax.ShapeDtypeStruct((n_seq * t_len, GROUP_W), BF)] + [_f32((n_seq, 1) + t) for t in state_tails]
    return pl.pallas_call(
        functools.partial(kernel, nc=nc, g=g, ns=ns), grid=(n_seq // ns,),
        in_specs=in_specs, out_specs=out_specs, out_shape=out_shape,
        scratch_shapes=[pltpu.VMEM((ns * t_len, GROUP_W), jnp.float32)] + scratch(ns),
        compiler_params=_params(1), name=name)(*seq_inputs, *[a for a, _ in const_inputs], *state_inputs)


def _bd_state(ns):
    return pltpu.VMEM((2 * ns, GROUP_W, GROUP_W), jnp.float32)


def _rope_tables(t_len):
    nf = HEAD_DIM // 4
    inv = ROPE_BASE ** (-jnp.arange(nf, dtype=jnp.float32) / nf)
    t = jnp.arange(t_len)
    ang_r = (t // GRID_W).astype(jnp.float32)[:, None] * inv[None, :]
    ang_c = (t % GRID_W).astype(jnp.float32)[:, None] * inv[None, :]
    cos = jnp.concatenate([jnp.cos(ang_r)] * 2 + [jnp.cos(ang_c)] * 2, axis=1)
    sin = jnp.concatenate([-jnp.sin(ang_r), jnp.sin(ang_r), -jnp.sin(ang_c), jnp.sin(ang_c)], axis=1)
    return jnp.tile(cos, (1, N_HEADS)), jnp.tile(sin, (1, N_HEADS))


def kernel(x_prompt, x_sample, c, state_mlstm_C, state_mlstm_n, state_mlstm_m, state_delta, state_ret,
           state_rwkv, c_ctx, w_mod, b_mod, w_in, w_out, ln1_g, ln1_b, ln2_g, ln2_b, w_ff1, w_ff2,
           mlstm_i_bias, mlstm_f_bias, mlstm_norm, delta_conv, delta_a_log, delta_dt_bias, delta_norm,
           ret_decay, ret_norm, rwkv_mu, rwkv_w0, rwkv_w2, rwkv_a0, rwkv_a2, rwkv_g2, rwkv_kk, rwkv_ka,
           rwkv_rk, rwkv_norm):
    f32 = jnp.float32
    batch, seq, d = x_prompt.shape
    dec_batch, dec_seq, _ = x_sample.shape
    assert d == D_MODEL and (batch * seq) % ROW_TILE == 0 and dec_seq % ROW_TILE == 0 and seq % CHUNK == 0
    assert (seq // CHUNK) % 2 == 0 and (dec_seq // CHUNK) % 2 == 0
    n_prows = batch * seq
    n_ptiles = n_prows // ROW_TILE
    tiles_per_dec = dec_seq // ROW_TILE

    c_all = jnp.zeros((8, d), f32).at[0].set(c_ctx).at[1:1 + dec_batch].set(c)
    mod = _mod_call(c_all, w_mod, b_mod).reshape(DEPTH, 8, 6, d)
    n_rows = n_prows + dec_batch * dec_seq
    h_pair = (x_prompt.reshape(n_prows, d), x_sample.reshape(dec_batch * dec_seq, d))
    cos_t, sin_t = _rope_tables(dec_seq)

    zmat = jnp.zeros((batch, 1) + MAT, f32)
    zn = jnp.zeros((batch, 1) + VEC, f32)
    zm = jnp.zeros((batch, 1) + SCL, f32)
    n_cache = state_mlstm_n.reshape(state_mlstm_n.shape[:-1] + (1, HEAD_DIM))
    m_cache = state_mlstm_m.reshape(state_mlstm_m.shape + (1, 1))
    groups = ((seq, batch, 0), (dec_seq, dec_batch, n_prows))

    bf = lambda a: a.astype(BF)
    rows = lambda a: a.reshape(DEPTH, 1, -1)
    zeros = lambda *shape: jnp.zeros((DEPTH,) + shape, f32)
    wg = bf(jnp.pad(jnp.concatenate([w_in[:, :, 1024:1040], w_in[:, :, 2064:2080]], axis=2),
                    ((0, 0), (0, 0), (0, GATE_LANES - N_GATES))))
    proj_ws = [bf(w_in[:, :, 0:1024]), bf(w_in[:, :, 1040:2064]), bf(w_in[:, :, 2080:3104]), bf(w_in[:, :, 3104:4256]),
               wg, jnp.swapaxes(wg[:, :, :N_GATES], 1, 2)]
    post_ws = [bf(w_out), rows(ln1_g), rows(ln1_b), bf(w_ff1), bf(w_ff2), rows(ln2_g), rows(ln2_b)]
    bias_c = jnp.concatenate([rows(mlstm_i_bias), rows(mlstm_f_bias), zeros(1, GATE_LANES - 16)], axis=2)
    bias_r = jnp.swapaxes(bias_c[:, :, :N_GATES], 1, 2)
    dl_row = jnp.concatenate([zeros(2, 24), jnp.concatenate([rows(delta_a_log), rows(delta_dt_bias)], axis=1),
                              zeros(2, GATE_LANES - 32)], axis=2)
    dl_col = jnp.swapaxes(dl_row[:, :, :N_GATES], 1, 2)
    rd_row = jnp.concatenate([rows(ret_decay), zeros(1, GATE_LANES - 8)], axis=2)
    rw_vec = jnp.concatenate([rows(rwkv_kk), rows(rwkv_ka), rows(rwkv_rk), rows(rwkv_norm)], axis=1)

    new_states = []
    for l in range(DEPTH):
        lay_of = lambda *arrs: [(a, l) for a in arrs]
        pa, pb, pc, pd, pg, pgt = _proj_call(h_pair, n_rows, mod, l, proj_ws, n_ptiles, tiles_per_dec)
        ys = [[], [], [], []]
        layer_states = None
        for gi, (t_len, n_seq, off) in enumerate(groups):
            ctx = gi == 0
            lay = 0 if ctx else l
            ya, c1, n1, m1 = _mixer_call(
                mlstm_kernel, "mlstm", SMALL_BATCH, t_len, n_seq, off, [pa, pg, pgt],
                lay_of(bias_c, bias_r, rows(mlstm_norm)),
                [zmat, zn, zm] if ctx else [state_mlstm_C, n_cache, m_cache], lay, [MAT, VEC, SCL],
                lambda ns: [_bd_state(ns), pltpu.VMEM((2 * ns, 1, GROUP_W), f32), pltpu.VMEM((2 * ns, 1, GROUP_W), f32)])
            yb, sb1 = _mixer_call(
                delta_kernel, "delta", LARGE_BATCH, t_len, n_seq, off, [pb, pg, pgt],
                lay_of(delta_conv, dl_row, dl_col, rows(delta_norm)),
                [zmat if ctx else state_delta], lay, [MAT], lambda ns: [_bd_state(ns)])
            ret_consts = lay_of(rd_row, rows(ret_norm))
            if not ctx:
                ret_consts = [(cos_t, None), (sin_t, None)] + ret_consts
            yc, sc1 = _mixer_call(
                functools.partial(ret_kernel, rope=not ctx), "ret", LARGE_BATCH, t_len, n_seq, off, [pc],
                ret_consts, [zmat if ctx else state_ret], lay, [MAT],
                lambda ns: [_bd_state(ns), pltpu.VMEM((2, 4, CHUNK, GROUP_W), f32)])
            yd, sd1 = _mixer_call(
                rwkv_kernel, "rwkv", SMALL_BATCH if ctx else LARGE_BATCH, t_len, n_seq, off, [pd],
                lay_of(rows(rwkv_mu), rwkv_w0, rwkv_w2, rwkv_a0, rwkv_a2, rwkv_g2, rw_vec),
                [zmat if ctx else state_rwkv], lay, [MAT], lambda ns: [_bd_state(ns)])
            for lst, y in zip(ys, (ya, yb, yc, yd)):
                lst.append(y)
            if ctx:
                layer_states = (c1, n1, m1, sb1, sc1, sd1)
        new_states.append(layer_states)
        last = l == DEPTH - 1
        h_new = _post_call(h_pair, [tuple(y) for y in ys], n_rows, mod, l, post_ws, n_ptiles, tiles_per_dec, last)
        h_pair = tuple(h_new) if last else (h_new, h_new)

    outs = [jnp.concatenate([st[i] for st in new_states], axis=1) for i in range(6)]
    outs[1] = outs[1].reshape(outs[1].shape[:4] + (HEAD_DIM,))
    outs[2] = outs[2].reshape(outs[2].shape[:4])
    return (h_pair[0].reshape(batch, seq, d), h_pair[1].reshape(dec_batch, dec_seq, d), *outs)
```

```python
import functools

import jax
import jax.numpy as jnp
from jax import lax
from jax.experimental import pallas as pl
from jax.experimental.pallas import tpu as pltpu

D_MODEL = 1024
DEPTH = 2
GRID_W = 64
HEAD_DIM = 64
N_HEADS = 4
GROUP_W = N_HEADS * HEAD_DIM
CHUNK = 64
CONV_K = 5
ROPE_BASE = 10000.0
LN_EPS = 1e-5
DEEPNORM_ALPHA = (2 * DEPTH) ** 0.25
RWKV_LORA = 64
GATE_LANES = 128
N_GATES = 32
ROW_TILE = 512
VMEM_LIMIT = 56 * 1024 * 1024
SMALL_BATCH = (2, 8)
LARGE_BATCH = (4, 16)
HALO = CHUNK + 16

NEG_INF = float("-inf")
BF = jnp.bfloat16
_NN = (((1,), (0,)), ((), ()))
_NT = (((1,), (1,)), ((), ()))
_TN = (((0,), (0,)), ((), ()))


def _dot(a, b, dims):
    return lax.dot_general(a, b, dims, preferred_element_type=jnp.result_type(a.dtype, jnp.float32))


def _bf(x):
    return x.astype(BF)


def _split(x, n):
    parts = []
    for _ in range(n - 1):
        p = _bf(x)
        parts.append(p)
        x = x - p.astype(x.dtype)
    parts.append(_bf(x))
    return parts


def bmm(a, b, dims=_NN):
    return _dot(_bf(a), _bf(b), dims)


def _expand(x, bdm):
    lane_head = jnp.right_shift(lax.broadcasted_iota(jnp.int32, (CHUNK, GROUP_W), 1), 6)
    zero = jnp.zeros((), x.dtype)
    return jnp.concatenate([jnp.where(lane_head == h, x, zero) for h in range(N_HEADS)], axis=0)


def wmm(a, b, bdm, passes=1, nt=False):
    dims = _NT if nt else _NN
    if passes == 1:
        return _dot(_bf(a), _expand(_bf(b), bdm), dims)
    ah, al = _split(a, 2)
    bh, bl = _split(b, 2)
    ebh = _expand(bh, bdm)
    lhs = jnp.concatenate([ah, ah, al], axis=1)
    rhs = jnp.concatenate([ebh, _expand(bl, bdm), ebh], axis=1 if nt else 0)
    return _dot(lhs, rhs, dims)


def wmm2(a, b1, b2, bdm, nt=False):
    e1, e2 = _expand(_bf(b1), bdm), _expand(_bf(b2), bdm)
    out = _dot(_bf(a), jnp.concatenate([e1, e2], axis=0 if nt else 1), _NT if nt else _NN)
    return out[:, :GROUP_W], out[:, GROUP_W:]


def cumsum_col(mc, x):
    return _dot(jnp.concatenate([_bf(mc)] * 3, axis=1), jnp.concatenate(_split(x, 3), axis=0), _NN)


def cumsum_row(x, mc):
    return _dot(jnp.concatenate(_split(x, 3), axis=1), jnp.concatenate([_bf(mc)] * 3, axis=1), _NT)


def head_sum(x, bd):
    return _dot(jnp.concatenate(_split(x, 2), axis=1), jnp.concatenate([_bf(bd)] * 2, axis=0), _NN)


def sigmoid(x):
    return 0.5 + 0.5 * jnp.tanh(0.5 * x)


def silu(x):
    return x * sigmoid(x)


def softplus(x):
    return jnp.maximum(x, 0.0) + jnp.log(1.0 + jnp.exp(-jnp.abs(x)))


def log_sigmoid(x):
    return -softplus(-x)


def head_norm(y, gain, bd, center):
    if center:
        y = y - head_sum(y, bd) * (1.0 / HEAD_DIM)
    ms = head_sum(y * y, bd) * (1.0 / HEAD_DIM)
    return y * lax.rsqrt(ms + LN_EPS) * gain


def l2norm_heads(x, bd):
    return x * lax.rsqrt(head_sum(x * x, bd) + 1e-6)


def layer_norm(x):
    mu = jnp.mean(x, axis=-1, keepdims=True)
    xc = x - mu
    var = jnp.mean(xc * xc, axis=-1, keepdims=True)
    return xc * lax.rsqrt(var + LN_EPS)


def colw(x, idx):
    m = x.shape[0]
    return jnp.concatenate([jnp.broadcast_to(x[:, j:j + 1], (m, HEAD_DIM)) for j in idx], axis=1)


def roww(x, idx):
    return jnp.concatenate([x[j:j + 1, :] for j in idx], axis=1)


def segmax(x):
    m = x.shape[0]
    return jnp.concatenate(
        [jnp.broadcast_to(jnp.max(x[:, h * HEAD_DIM:(h + 1) * HEAD_DIM], axis=1, keepdims=True), (m, HEAD_DIM))
         for h in range(N_HEADS)], axis=1)


def wide_masks(rev, dtype=jnp.float32):
    r = lax.broadcasted_iota(jnp.int32, (CHUNK, GROUP_W), 0)
    c = jnp.bitwise_and(lax.broadcasted_iota(jnp.int32, (CHUNK, GROUP_W), 1), HEAD_DIM - 1)
    if rev:
        incl_w, strict_w = r <= c, r < c
    else:
        incl_w, strict_w = r >= c, r > c
    mc = incl_w[:, :CHUNK].astype(dtype)
    return incl_w, strict_w, mc, (r == c).astype(dtype)


def block_diag_mask():
    r = jnp.right_shift(lax.broadcasted_iota(jnp.int32, (GROUP_W, GROUP_W), 0), 6)
    c = jnp.right_shift(lax.broadcasted_iota(jnp.int32, (GROUP_W, GROUP_W), 1), 6)
    m = (r == c).astype(jnp.float32)
    return _bf(m), m


def inv_unit_tri_w(a, eye_w, bdm):
    p = _bf(-a)
    e = _expand(p, bdm)
    t = eye_w - a
    for _ in range(5):
        p = _bf(_dot(p, e, _NN))
        e = _expand(p, bdm)
        t = t + _dot(_bf(t), e, _NN)
    r = eye_w - t - wmm(a, t, bdm, 3)
    return t + wmm(t, r, bdm, 1)


def mlstm_pre(q, k, v, bcol_w, brow_w, icol_w, irow_w, bl_w, incl_w, bdm):
    logd = jnp.where(incl_w, bcol_w - brow_w + irow_w, NEG_INF)
    mt0 = segmax(logd)
    s0 = wmm(q, k, bdm, 1, nt=True) * jnp.exp(logd - mt0)
    wk = bl_w - bcol_w + icol_w
    wkmax = jnp.max(wk, axis=0, keepdims=True)
    ks0 = k * jnp.exp(wk - wkmax)
    return (q, bcol_w, bl_w, mt0, wmm(s0, v, bdm, 1), head_sum(s0, bdm[1]), wkmax,
            _bf(ks0), _bf(v), jnp.sum(ks0, axis=0, keepdims=True))


def mlstm_step(q, bcol_w, bl_w, mt0, num0, den0, wkmax, ks0, v, ksum0, c_bd, n_row, m_w, bdm):
    inter = bcol_w + m_w
    mt = jnp.maximum(inter, mt0)
    f = jnp.exp(mt0 - mt)
    ew = jnp.exp(inter - mt)
    num = num0 * f + bmm(q, c_bd) * ew
    den = den0 * f + head_sum(q * n_row, bdm[1]) * ew
    h = num / jnp.maximum(jnp.abs(den), jnp.exp(-mt))
    m_new = jnp.maximum(bl_w + m_w, wkmax)
    sp = jnp.exp(bl_w + m_w - m_new)
    sc = jnp.exp(wkmax - m_new)
    kv0 = bmm(ks0, v, _TN) * bdm[1]
    return h, c_bd * sp + kv0 * sc, n_row * sp + ksum0 * sc, m_new


def delta_pre(q_raw, k_raw, v, beta_w, gcol_w, grow_w, gl_w, incl_w, strict_w, eye_w, bdm):
    q = l2norm_heads(q_raw, bdm[1]) * HEAD_DIM ** -0.5
    k = l2norm_heads(k_raw, bdm[1])
    decay = jnp.exp(jnp.where(incl_w, gcol_w - grow_w, NEG_INF))
    a = jnp.where(strict_w, wmm(k, k, bdm, 1, nt=True) * decay, 0.0) * beta_w
    t = inv_unit_tri_w(a, eye_w, bdm)
    eg = jnp.exp(gcol_w)
    u, w = wmm2(t, v * beta_w, k * (beta_w * eg), bdm)
    qk = wmm(q, k, bdm, 1, nt=True) * decay
    return (u, _bf(jnp.concatenate([w, q * eg], axis=0)), _bf(qk), _bf(k * jnp.exp(gl_w - gcol_w)),
            jnp.exp(gl_w))


def delta_step(u, wq, qk, kd, egl, s_bd, bdm):
    ws = bmm(wq, s_bd)
    v_new = u - ws[:CHUNK]
    o = ws[CHUNK:] + wmm(qk, v_new, bdm, 1)
    s_new = s_bd * egl + bmm(kd, v_new, _TN) * bdm[1]
    return o, s_new


def ret_pre(q, k, v, dmat_w, zeta_w, bdm):
    sc = wmm(q, k, bdm, 1, nt=True) * dmat_w
    return _bf(q), wmm(sc, v, bdm, 1), _bf(k * zeta_w), _bf(v)


def ret_step(q, o1, kz, v, xi_w, gl_w, s_bd, bdm):
    return o1 + bmm(q, s_bd) * xi_w, s_bd * gl_w + bmm(kz, v, _TN) * bdm[1]


def rwkv_pre(r, kkk, kt, a, v, cw, lw, tot, incl_w, strict_w, eye_w, bdm):
    kh = l2norm_heads(kkk, bdm[1])
    b = a * kh
    en = jnp.exp(-cw)
    et = jnp.exp(tot - cw)
    rp = r * jnp.exp(cw)
    khp = kh * jnp.exp(cw - lw)
    ktn = kt * en
    bn = b * en
    lhs = jnp.concatenate([khp, rp], axis=0)
    ab_kt, ab_b = wmm2(lhs, ktn, bn, bdm, nt=True)
    a_kt = jnp.where(strict_w, ab_kt[:CHUNK], 0.0)
    b_kt = jnp.where(incl_w, ab_kt[CHUNK:], 0.0)
    a_b = jnp.where(strict_w, ab_b[:CHUNK], 0.0)
    b_b = jnp.where(incl_w, ab_b[CHUNK:], 0.0)
    t = inv_unit_tri_w(a_b, eye_w, bdm)
    tk = wmm(t, khp, bdm, 1)
    tav = wmm(t, wmm(a_kt, v, bdm, 1), bdm, 1)
    return (_bf(jnp.concatenate([tk, rp], axis=0)), tav, wmm(b_kt, v, bdm, 1), _bf(b_b),
            _bf(jnp.concatenate([kt * et, b * et], axis=0)))


def rwkv_step(tkr, tav, o1, b_b, kbe, v, etot, s_bd, bdm):
    ls = bmm(tkr, s_bd, _NT)
    u = ls[:CHUNK] + tav
    o = ls[CHUNK:] + o1 - wmm(b_b, u, bdm, 1)
    upd = bmm(jnp.concatenate([v, -u], axis=0), kbe, _TN)
    s_new = s_bd * etot + upd * bdm[1]
    return o, s_new


def _group_loop(nc, g, body):
    n_it = nc // g

    def run(first):
        def it(i, carry):
            body(i, first)
            return carry
        return it

    lax.fori_loop(0, n_it // 2, run(True), 0)
    lax.fori_loop(n_it // 2, n_it, run(False), 0)


def _scan_steps(nc, max_steps):
    return max(g for g in (1, 2, 4, 8) if g <= max_steps and nc % (2 * g) == 0)


def _group_chunks(i, g, nc, ns):
    out = []
    for s in range(ns):
        for z in range(2):
            for j in range(g):
                step = i * g + j
                c = s * nc + (step if z == 0 else nc - 1 - step)
                out.append((s, z, c, pl.multiple_of(c * CHUNK, CHUNK)))
    return out


def _group_masks(g, ns):
    shape = (2 * g * ns, CHUNK, GROUP_W)
    lane = jnp.right_shift(lax.broadcasted_iota(jnp.int32, shape, 0), g.bit_length() - 1)
    r = lax.broadcasted_iota(jnp.int32, shape, 1)
    c = jnp.bitwise_and(lax.broadcasted_iota(jnp.int32, shape, 2), HEAD_DIM - 1)
    d = jnp.where(jnp.bitwise_and(lane, 1) == 1, c - r, r - c)
    return d >= 0, d > 0, (d == 0).astype(jnp.float32)


def _stack(items):
    return [jnp.stack(x) for x in zip(*items)]


def _pick(arrs, g, j):
    return [jnp.stack([a[k] for k in range(j, a.shape[0], g)]) for a in arrs]


def _halo(ref, s, c, t0, nc):
    rows = ref.shape[0]
    ps = pl.multiple_of(jnp.maximum(t0 - 8, 0), 8)
    nx = pl.multiple_of(jnp.minimum(t0 + CHUNK, rows - 8), 8)
    prev = jnp.where(c > s * nc, ref[pl.ds(ps, 8), :], 0.0)
    nxt = jnp.where(c < s * nc + nc - 1, ref[pl.ds(nx, 8), :], 0.0)
    return jnp.concatenate([prev, ref[pl.ds(t0, CHUNK), :], nxt], axis=0)


def _shift_matrix(offsets):
    rows = len(offsets) * CHUNK
    r = lax.broadcasted_iota(jnp.int32, (rows, 2 * HALO), 0)
    c = lax.broadcasted_iota(jnp.int32, (rows, 2 * HALO), 1)
    c = jnp.where(c >= HALO, c - HALO, c)
    t = jnp.bitwise_and(r, CHUNK - 1)
    target = t + 8
    for j, off in enumerate(offsets):
        target = jnp.where(jnp.right_shift(r, 6) == j, target + off, target)
    return _bf((c == target).astype(jnp.float32))


def _hs(h):
    return slice(h * HEAD_DIM, (h + 1) * HEAD_DIM)


def _load_state(sbd, s0_ref, ns):
    for s in range(ns):
        for z in range(2):
            sbd[2 * s + z] = jnp.zeros((GROUP_W, GROUP_W), jnp.float32)
            for h in range(N_HEADS):
                sbd[2 * s + z, _hs(h), _hs(h)] = s0_ref[s, 0, z, h]


def _store_state(s1_ref, sbd, ns):
    for s in range(ns):
        for z in range(2):
            for h in range(N_HEADS):
                s1_ref[s, 0, z, h] = sbd[2 * s + z, _hs(h), _hs(h)]


def _merge(first, hscr, t0, h):
    if first:
        hscr[pl.ds(t0, CHUNK), :] = h
        return None
    return hscr[pl.ds(t0, CHUNK), :] + h


def _scan(nc, g, ns, prep, pre_fn, n_masks, step_fn, state_refs, hscr, finish, y_ref):
    bdm = block_diag_mask()
    masks = _group_masks(g, ns)[:n_masks]

    def body(i, first):
        chunks = _group_chunks(i, g, nc, ns)
        preps = [prep(s, z, c, t0, first) for s, z, c, t0 in chunks]
        n_in = len(preps[0][0])
        pre = jax.vmap(pre_fn, in_axes=(0,) * (n_in + n_masks) + (None,))(
            *_stack([p[0] for p in preps]), *masks, bdm)
        extra = _stack([p[1] for p in preps]) if preps[0][1] else []
        states = [ref[...] for ref in state_refs]
        for j in range(g):
            args = _pick(list(pre), g, j) + _pick(extra, g, j) + states
            res = jax.vmap(step_fn, in_axes=(0,) * len(args) + (None,))(*args, bdm)
            states = list(res[1:])
            ks = [lane * g + j for lane in range(2 * ns)]
            tots = [_merge(first, hscr, chunks[k][3], res[0][lane]) for lane, k in enumerate(ks)]
            if not first:
                ys = jax.vmap(finish)(jnp.stack(tots), *_stack([preps[k][2] for k in ks]))
                for lane, k in enumerate(ks):
                    y_ref[pl.ds(chunks[k][3], CHUNK), :] = ys[lane]
        for ref, val in zip(state_refs, states):
            ref[...] = val

    _group_loop(nc, g, body)


def mlstm_kernel(p_ref, gc_ref, gr_ref, bc_ref, br_ref, norm_ref, c0_ref, n0_ref, m0_ref,
                 y_ref, c1_ref, n1_ref, m1_ref, hscr, cbd, nrow, mw, *, nc, g, ns):
    _load_state(cbd, c0_ref, ns)
    for s in range(ns):
        for z in range(2):
            for h in range(N_HEADS):
                nrow[2 * s + z, :, _hs(h)] = n0_ref[s, 0, z, h]
                mw[2 * s + z, :, _hs(h)] = jnp.broadcast_to(m0_ref[s, 0, z, h], (1, HEAD_DIM))
    bd = block_diag_mask()[1]
    mcs = [wide_masks(False)[2], wide_masks(True)[2]]
    gain = norm_ref[...]

    def prep(s, z, c, t0, first):
        x = p_ref[pl.ds(t0, CHUNK), 0:3 * GROUP_W]
        g_c = gc_ref[pl.ds(t0, CHUNK), :] + bc_ref[...]
        g_r = gr_ref[c] + br_ref[...]
        lf_c = log_sigmoid(g_c)
        idx_i = [z * N_HEADS + h for h in range(N_HEADS)]
        idx_f = [8 + j for j in idx_i]
        fin = () if first else (p_ref[pl.ds(t0, CHUNK), 3 * GROUP_W:],)
        return ((x[:, :GROUP_W], x[:, GROUP_W:2 * GROUP_W] * HEAD_DIM ** -0.5, x[:, 2 * GROUP_W:],
                 colw(cumsum_col(mcs[z], lf_c), idx_f), roww(cumsum_row(log_sigmoid(g_r), mcs[z]), idx_f),
                 colw(g_c, idx_i), roww(g_r, idx_i), colw(jnp.sum(lf_c, axis=0, keepdims=True), idx_f)), (), fin)

    def finish(tot, og):
        return _bf(sigmoid(og) * head_norm(tot, gain, bd, True))

    _scan(nc, g, ns, prep, mlstm_pre, 1, mlstm_step, [cbd, nrow, mw], hscr, finish, y_ref)
    _store_state(c1_ref, cbd, ns)
    for s in range(ns):
        for z in range(2):
            for h in range(N_HEADS):
                n1_ref[s, 0, z, h] = nrow[2 * s + z, :, _hs(h)]
                m1_ref[s, 0, z, h] = mw[2 * s + z, :, h * HEAD_DIM:h * HEAD_DIM + 1]


def delta_kernel(p_ref, gc_ref, gr_ref, conv_ref, prow_ref, pcol_ref, norm_ref, s0_ref,
                 y_ref, s1_ref, hscr, sbd, *, nc, g, ns):
    _load_state(sbd, s0_ref, ns)
    bd = block_diag_mask()[1]
    mcs = [wide_masks(False)[2], wide_masks(True)[2]]
    nega_row = -jnp.exp(prow_ref[0:1, :])
    dtb_row = prow_ref[1:2, :]
    nega_col = -jnp.exp(pcol_ref[:, 0:1])
    dtb_col = pcol_ref[:, 1:2]
    qkv_w = 3 * GROUP_W
    gain = norm_ref[...]
    offsets = [j - CONV_K // 2 for j in range(CONV_K) if j != CONV_K // 2]
    taps = _shift_matrix(offsets)

    def prep(s, z, c, t0, first):
        ext = _halo(p_ref.at[:, 0:qkv_w], s, c, t0, nc)
        shifted = _dot(taps, jnp.concatenate(_split(ext, 2), axis=0), _NN)
        acc = ext[8:8 + CHUNK] * conv_ref[CONV_K // 2:CONV_K // 2 + 1, :]
        for i, off in enumerate(offsets):
            j = off + CONV_K // 2
            acc = acc + shifted[i * CHUNK:(i + 1) * CHUNK] * conv_ref[j:j + 1, :]
        qkv = silu(acc)
        g_c = gc_ref[pl.ds(t0, CHUNK), :]
        gg_c = nega_row * softplus(g_c + dtb_row)
        gg_r = nega_col * softplus(gr_ref[c] + dtb_col)
        idx_b = [16 + z * N_HEADS + h for h in range(N_HEADS)]
        idx_a = [8 + j for j in idx_b]
        fin = () if first else (p_ref[pl.ds(t0, CHUNK), qkv_w:],)
        return ((qkv[:, :GROUP_W], qkv[:, GROUP_W:2 * GROUP_W], qkv[:, 2 * GROUP_W:], colw(sigmoid(g_c), idx_b),
                 colw(cumsum_col(mcs[z], gg_c), idx_a), roww(cumsum_row(gg_r, mcs[z]), idx_a),
                 colw(jnp.sum(gg_c, axis=0, keepdims=True), idx_a)), (), fin)

    def finish(tot, zg):
        return _bf(head_norm(tot, gain, bd, False) * silu(zg))

    _scan(nc, g, ns, prep, delta_pre, 3, delta_step, [sbd], hscr, finish, y_ref)
    _store_state(s1_ref, sbd, ns)


def _rope(x, cos, sin, lane_lo):
    n = x.shape[1]
    swapped = jnp.where(lane_lo, pltpu.roll(x, n - 16, 1), pltpu.roll(x, 16, 1))
    return x * cos + swapped * sin


def ret_kernel(*refs, nc, g, ns, rope):
    if rope:
        p_ref, cos_ref, sin_ref, rd_ref, norm_ref, s0_ref, y_ref, s1_ref, hscr, sbd, cst = refs
    else:
        p_ref, rd_ref, norm_ref, s0_ref, y_ref, s1_ref, hscr, sbd, cst = refs
    f32 = jnp.float32
    _load_state(sbd, s0_ref, ns)
    bd = block_diag_mask()[1]
    r = lax.broadcasted_iota(jnp.int32, (CHUNK, GROUP_W), 0).astype(f32)
    cc = jnp.bitwise_and(lax.broadcasted_iota(jnp.int32, (CHUNK, GROUP_W), 1), HEAD_DIM - 1).astype(f32)
    lg_row = -jnp.exp(rd_ref[...])
    gain = norm_ref[...]
    for z in range(2):
        incl_w = wide_masks(z == 1)[0]
        lg = colw(lg_row, [z * N_HEADS + h for h in range(N_HEADS)])
        if z == 0:
            cst[z, 0] = jnp.exp(jnp.where(incl_w, (r - cc) * lg, NEG_INF))
            cst[z, 1] = jnp.exp((r + 1.0) * lg)
            cst[z, 2] = jnp.exp((CHUNK - 1.0 - r) * lg)
        else:
            cst[z, 0] = jnp.exp(jnp.where(incl_w, (cc - r) * lg, NEG_INF))
            cst[z, 1] = jnp.exp((CHUNK - r) * lg)
            cst[z, 2] = jnp.exp(r * lg)
        cst[z, 3] = jnp.exp(CHUNK * lg + 0.0 * r)
    if rope:
        lane = lax.broadcasted_iota(jnp.int32, (CHUNK, GROUP_W), 1)
        lane_lo = jnp.bitwise_and(lane, 31) < 16

    def prep(s, z, c, t0, first):
        x = p_ref[pl.ds(t0, CHUNK), 0:3 * GROUP_W]
        q = x[:, :GROUP_W] * HEAD_DIM ** -0.5
        k = x[:, GROUP_W:2 * GROUP_W]
        if rope:
            tr = pl.multiple_of(t0 - s * nc * CHUNK, CHUNK)
            cos = cos_ref[pl.ds(tr, CHUNK), :]
            sin = sin_ref[pl.ds(tr, CHUNK), :]
            q = _rope(q, cos, sin, lane_lo)
            k = _rope(k, cos, sin, lane_lo)
        fin = () if first else (p_ref[pl.ds(t0, CHUNK), 3 * GROUP_W:],)
        return (q, k, x[:, 2 * GROUP_W:], cst[z, 0], cst[z, 2]), (cst[z, 1], cst[z, 3, 0:1, :]), fin

    def finish(tot, gate):
        return _bf(silu(gate) * head_norm(tot, gain, bd, True))

    _scan(nc, g, ns, prep, ret_pre, 0, ret_step, [sbd], hscr, finish, y_ref)
    _store_state(s1_ref, sbd, ns)


def rwkv_kernel(p_ref, mu_ref, w0_ref, w2_ref, a0_ref, a2_ref, g2_ref, vec_ref, s0_ref,
                y_ref, s1_ref, hscr, sbd, *, nc, g, ns):
    _load_state(sbd, s0_ref, ns)
    bd = block_diag_mask()[1]
    mcs = [wide_masks(False)[2], wide_masks(True)[2]]
    kk = vec_ref[0:1, :]
    ka = vec_ref[1:2, :]
    rk = vec_ref[2:3, :]
    gain = vec_ref[3:4, :]
    g2 = g2_ref[...]

    def lora_a(da, zz):
        return sigmoid(a0_ref[zz:zz + 1, :] + bmm(da[:, zz * RWKV_LORA:(zz + 1) * RWKV_LORA], a2_ref[zz]))

    def prep(s, z, c, t0, first):
        ext = _halo(p_ref, s, c, t0, nc)
        cur = ext[8:8 + CHUNK]
        nb = pltpu.roll(ext, 1, 0) + pltpu.roll(ext, HALO - 1, 0)
        pdm = cur + mu_ref[...] * (0.5 * nb[8:8 + CHUNK] - cur)
        r = pdm[:, :GROUP_W]
        k = pdm[:, GROUP_W:2 * GROUP_W]
        v = pdm[:, 2 * GROUP_W:3 * GROUP_W]
        dw = pdm[:, 3 * GROUP_W:3 * GROUP_W + 128]
        da = pdm[:, 3 * GROUP_W + 128:3 * GROUP_W + 256]
        a_z = lora_a(da, z)
        w_pre = w0_ref[z:z + 1, :] + bmm(jnp.tanh(dw[:, z * RWKV_LORA:(z + 1) * RWKV_LORA]), w2_ref[z])
        lw = -jnp.exp(-softplus(-w_pre) - 0.5)
        kt_z = k * (1.0 + (a_z - 1.0) * ka)
        tot = jnp.sum(lw, axis=0, keepdims=True)
        fin = ()
        if not first:
            kt_o = k * (1.0 + (lora_a(da, 1 - z) - 1.0) * ka)
            fin = (pdm[:, 3 * GROUP_W + 256:], r * (kt_z + kt_o) * rk, v)
        return (r, k * kk, kt_z, a_z, v, cumsum_col(mcs[z], lw), lw, tot), (v, jnp.exp(tot)), fin

    def finish(tot_h, dg, rkk, v):
        gate = bmm(sigmoid(dg), g2)
        return _bf(gate * (head_norm(tot_h, gain, bd, True) + head_sum(rkk, bd) * v))

    _scan(nc, g, ns, prep, rwkv_pre, 3, rwkv_step, [sbd], hscr, finish, y_ref)
    _store_state(s1_ref, sbd, ns)


def mod_kernel(c_ref, w_ref, b_ref, o_ref):
    o_ref[0] = bmm(silu(c_ref[...]), w_ref[0]) + b_ref[0]


def _rows(is_ctx, ctx_ref, lat_ref):
    return jnp.where(is_ctx, ctx_ref[...], lat_ref[...])


def proj_kernel(xp_ref, xs_ref, mod_ref, wa, wb, wc, wd, wg, wgt, pa, pb, pc, pd, pg, pgt, *, n_ptiles):
    m = mod_ref[0]
    x = _rows(pl.program_id(0) < n_ptiles, xp_ref, xs_ref)
    u = layer_norm(x) * (1.0 + m[1:2]) + m[0:1]
    ub = u.astype(BF)
    pa[...] = _dot(ub, wa[...], _NN)
    pb[...] = _dot(ub, wb[...], _NN)
    pc[...] = _dot(ub, wc[...], _NN)
    pd[...] = _dot(ub, wd[...], _NN)
    pg[...] = _dot(ub, wg[...], _NN)
    gt = _dot(wgt[...], ub, _NT)
    for j in range(ROW_TILE // CHUNK):
        pgt[j] = gt[:, j * CHUNK:(j + 1) * CHUNK]


def post_kernel(hp, hs, yap, yas, ybp, ybs, ycp, ycs, ydp, yds, mod_ref, wout, l1g, l1b, w1, w2, l2g, l2b,
                *o_refs, n_ptiles):
    m = mod_ref[0]
    is_ctx = pl.program_id(0) < n_ptiles
    y = jnp.concatenate([_rows(is_ctx, yap, yas), _rows(is_ctx, ybp, ybs), _rows(is_ctx, ycp, ycs),
                         _rows(is_ctx, ydp, yds)], axis=1)
    attn = _dot(y, wout[...], _NN)
    h1 = layer_norm(DEEPNORM_ALPHA * _rows(is_ctx, hp, hs) + m[2:3] * attn) * l1g[...] + l1b[...]
    u2 = layer_norm(h1) * (1.0 + m[4:5]) + m[3:4]
    f = _dot(u2.astype(BF), w1[...], _NN)
    f = jnp.square(jnp.maximum(f, 0.0))
    f2 = _dot(f.astype(BF), w2[...], _NN)
    out = layer_norm(DEEPNORM_ALPHA * h1 + m[5:6] * f2) * l2g[...] + l2b[...]
    if len(o_refs) == 1:
        o_refs[0][...] = out
    else:
        @pl.when(is_ctx)
        def _():
            o_refs[0][...] = out

        @pl.when(jnp.logical_not(is_ctx))
        def _():
            o_refs[1][...] = out


def _params(n_grid_axes):
    return pltpu.CompilerParams(dimension_semantics=("arbitrary",) * n_grid_axes,
                                vmem_limit_bytes=VMEM_LIMIT)


def _const_spec(a, layer=None):
    nd = a.ndim
    if layer is None:
        return pl.BlockSpec(a.shape, lambda *_: (0,) * nd, pipeline_mode=pl.Buffered(1))
    return pl.BlockSpec((None,) + a.shape[1:], lambda *_: (layer,) + (0,) * (nd - 1), pipeline_mode=pl.Buffered(1))


def _f32(shape):
    return jax.ShapeDtypeStruct(shape, jnp.float32)


def _mod_call(c_all, w_mod, b_mod):
    depth, d, n = w_mod.shape
    tn = 2048
    return pl.pallas_call(
        mod_kernel, grid=(depth, n // tn),
        in_specs=[pl.BlockSpec(c_all.shape, lambda l, j: (0, 0)),
                  pl.BlockSpec((1, d, tn), lambda l, j: (l, 0, j)),
                  pl.BlockSpec((1, 1, tn), lambda l, j: (l, 0, j))],
        out_specs=pl.BlockSpec((1, c_all.shape[0], tn), lambda l, j: (l, 0, j)),
        out_shape=_f32((depth, c_all.shape[0], n)),
        compiler_params=_params(2), name="mod")(c_all, w_mod, b_mod.reshape(depth, 1, n))


def _mod_spec(layer, n_ptiles, tiles_per_dec):
    def index(i):
        return (layer, jnp.where(i < n_ptiles, 0, 1 + (i - n_ptiles) // tiles_per_dec), 0, 0)
    return pl.BlockSpec((None, 1, 6, D_MODEL), index)


def _pair_specs(pair, n_ptiles):
    width = pair[0].shape[1]
    off = n_ptiles if pair[0] is pair[1] else 0
    return [pl.BlockSpec((ROW_TILE, width), lambda i: (jnp.minimum(i, n_ptiles - 1), 0)),
            pl.BlockSpec((ROW_TILE, width), lambda i: (jnp.maximum(i - n_ptiles, 0) + off, 0))]


def _proj_call(x_pair, n_rows, mod, layer, ws, n_ptiles, tiles_per_dec):
    row = lambda i: (i, 0)
    widths = [w.shape[-1] for w in ws[:5]]
    per_tile = ROW_TILE // CHUNK
    return pl.pallas_call(
        functools.partial(proj_kernel, n_ptiles=n_ptiles), grid=(n_rows // ROW_TILE,),
        in_specs=_pair_specs(x_pair, n_ptiles) + [_mod_spec(layer, n_ptiles, tiles_per_dec)]
                 + [_const_spec(w, layer) for w in ws],
        out_specs=[pl.BlockSpec((ROW_TILE, w), row) for w in widths]
                  + [pl.BlockSpec((per_tile, N_GATES, CHUNK), lambda i: (i, 0, 0))],
        out_shape=[_f32((n_rows, w)) for w in widths] + [_f32((n_rows // CHUNK, N_GATES, CHUNK))],
        compiler_params=_params(1), name="proj")(*x_pair, mod, *ws)


def _post_call(h_pair, y_pairs, n_rows, mod, layer, ws, n_ptiles, tiles_per_dec, split_out):
    pairs = [h_pair] + y_pairs
    if split_out:
        n_prows = n_ptiles * ROW_TILE
        out_specs = _pair_specs((_f32((n_prows, D_MODEL)), _f32((n_rows - n_prows, D_MODEL))), n_ptiles)
        out_shape = [_f32((n_prows, D_MODEL)), _f32((n_rows - n_prows, D_MODEL))]
    else:
        out_specs = pl.BlockSpec((ROW_TILE, D_MODEL), lambda i: (i, 0))
        out_shape = _f32((n_rows, D_MODEL))
    return pl.pallas_call(
        functools.partial(post_kernel, n_ptiles=n_ptiles), grid=(n_rows // ROW_TILE,),
        in_specs=[spec for p in pairs for spec in _pair_specs(p, n_ptiles)]
                 + [_mod_spec(layer, n_ptiles, tiles_per_dec)] + [_const_spec(w, layer) for w in ws],
        out_specs=out_specs, out_shape=out_shape,
        compiler_params=_params(1), name="post")(*[a for p in pairs for a in p], mod, *ws)


MAT = (2, N_HEADS, HEAD_DIM, HEAD_DIM)
VEC = (2, N_HEADS, 1, HEAD_DIM)
SCL = (2, N_HEADS, 1, 1)


def _mixer_call(kernel, name, batching, t_len, n_seq, row_off, seq_inputs, const_inputs, state_inputs, layer,
                state_tails, scratch, out_layer=None, carried=None):
    nc = t_len // CHUNK
    g = _scan_steps(nc, batching[0])
    ns = max(1, min(n_seq, batching[1] // (2 * g)))
    assert n_seq % ns == 0 and row_off % (ns * t_len) == 0
    blk_off = row_off // (ns * t_len)
    in_specs = []
    for a in seq_inputs:
        if a.ndim == 2:
            in_specs.append(pl.BlockSpec((ns * t_len, a.shape[1]), lambda i: (i + blk_off, 0)))
        else:
            in_specs.append(pl.BlockSpec((ns * nc,) + a.shape[1:], lambda i: (i + blk_off, 0, 0)))
    in_specs += [_const_spec(a, lay) for a, lay in const_inputs]
    for a in state_inputs:
        tail = a.shape[2:]
        in_specs.append(pl.BlockSpec((ns, 1) + tail, lambda i, nd=len(tail): (i, layer) + (0,) * nd))
    out_specs = [pl.BlockSpec((ns * t_len, GROUP_W), lambda i: (i, 0))]
    n_layers, at_layer = (1, 0) if out_layer is None else (DEPTH, out_layer)
    out_specs += [pl.BlockSpec((ns, 1) + t, lambda i, nd=len(t): (i, at_layer) + (0,) * nd) for t in state_tails]
    out_shape = [jax.ShapeDtypeStruct((n_seq * t_len, GROUP_W), BF)]
    out_shape += [_f32((n_seq, n_layers) + t) for t in state_tails]
    carried = list(carried or [])
    n_in = len(in_specs)
    in_specs += [pl.BlockSpec(memory_space=pl.ANY)] * len(carried)
    body = functools.partial(kernel, nc=nc, g=g, ns=ns)

    def skip_carried(*refs):
        body(*refs[:n_in], *refs[n_in + len(carried):])

    return pl.pallas_call(
        skip_carried, grid=(n_seq // ns,),
        in_specs=in_specs, out_specs=out_specs, out_shape=out_shape,
        input_output_aliases={n_in + k: 1 + k for k in range(len(carried))},
        scratch_shapes=[pltpu.VMEM((ns * t_len, GROUP_W), jnp.float32)] + scratch(ns),
        compiler_params=_params(1), name=name)(*seq_inputs, *[a for a, _ in const_inputs], *state_inputs, *carried)


def _bd_state(ns):
    return pltpu.VMEM((2 * ns, GROUP_W, GROUP_W), jnp.float32)


def _rope_tables(t_len):
    nf = HEAD_DIM // 4
    inv = ROPE_BASE ** (-jnp.arange(nf, dtype=jnp.float32) / nf)
    t = jnp.arange(t_len)
    ang_r = (t // GRID_W).astype(jnp.float32)[:, None] * inv[None, :]
    ang_c = (t % GRID_W).astype(jnp.float32)[:, None] * inv[None, :]
    cos = jnp.concatenate([jnp.cos(ang_r)] * 2 + [jnp.cos(ang_c)] * 2, axis=1)
    sin = jnp.concatenate([-jnp.sin(ang_r), jnp.sin(ang_r), -jnp.sin(ang_c), jnp.sin(ang_c)], axis=1)
    return jnp.tile(cos, (1, N_HEADS)), jnp.tile(sin, (1, N_HEADS))


def kernel(x_prompt, x_sample, c, state_mlstm_C, state_mlstm_n, state_mlstm_m, state_delta, state_ret,
           state_rwkv, c_ctx, w_mod, b_mod, w_in, w_out, ln1_g, ln1_b, ln2_g, ln2_b, w_ff1, w_ff2,
           mlstm_i_bias, mlstm_f_bias, mlstm_norm, delta_conv, delta_a_log, delta_dt_bias, delta_norm,
           ret_decay, ret_norm, rwkv_mu, rwkv_w0, rwkv_w2, rwkv_a0, rwkv_a2, rwkv_g2, rwkv_kk, rwkv_ka,
           rwkv_rk, rwkv_norm):
    f32 = jnp.float32
    batch, seq, d = x_prompt.shape
    dec_batch, dec_seq, _ = x_sample.shape
    assert d == D_MODEL and (batch * seq) % ROW_TILE == 0 and dec_seq % ROW_TILE == 0 and seq % CHUNK == 0
    assert (seq // CHUNK) % 2 == 0 and (dec_seq // CHUNK) % 2 == 0
    n_prows = batch * seq
    n_ptiles = n_prows // ROW_TILE
    tiles_per_dec = dec_seq // ROW_TILE

    c_all = jnp.zeros((8, d), f32).at[0].set(c_ctx).at[1:1 + dec_batch].set(c)
    mod = _mod_call(c_all, w_mod, b_mod).reshape(DEPTH, 8, 6, d)
    n_rows = n_prows + dec_batch * dec_seq
    h_pair = (x_prompt.reshape(n_prows, d), x_sample.reshape(dec_batch * dec_seq, d))
    cos_t, sin_t = _rope_tables(dec_seq)

    zmat = jnp.zeros((batch, 1) + MAT, f32)
    zn = jnp.zeros((batch, 1) + VEC, f32)
    zm = jnp.zeros((batch, 1) + SCL, f32)
    n_cache = state_mlstm_n.reshape(state_mlstm_n.shape[:-1] + (1, HEAD_DIM))
    m_cache = state_mlstm_m.reshape(state_mlstm_m.shape + (1, 1))
    groups = ((seq, batch, 0), (dec_seq, dec_batch, n_prows))

    bf = lambda a: a.astype(BF)
    rows = lambda a: a.reshape(DEPTH, 1, -1)
    zeros = lambda *shape: jnp.zeros((DEPTH,) + shape, f32)
    wg = bf(jnp.pad(jnp.concatenate([w_in[:, :, 1024:1040], w_in[:, :, 2064:2080]], axis=2),
                    ((0, 0), (0, 0), (0, GATE_LANES - N_GATES))))
    proj_ws = [bf(w_in[:, :, 0:1024]), bf(w_in[:, :, 1040:2064]), bf(w_in[:, :, 2080:3104]), bf(w_in[:, :, 3104:4256]),
               wg, jnp.swapaxes(wg[:, :, :N_GATES], 1, 2)]
    post_ws = [bf(w_out), rows(ln1_g), rows(ln1_b), bf(w_ff1), bf(w_ff2), rows(ln2_g), rows(ln2_b)]
    bias_c = jnp.concatenate([rows(mlstm_i_bias), rows(mlstm_f_bias), zeros(1, GATE_LANES - 16)], axis=2)
    bias_r = jnp.swapaxes(bias_c[:, :, :N_GATES], 1, 2)
    dl_row = jnp.concatenate([zeros(2, 24), jnp.concatenate([rows(delta_a_log), rows(delta_dt_bias)], axis=1),
                              zeros(2, GATE_LANES - 32)], axis=2)
    dl_col = jnp.swapaxes(dl_row[:, :, :N_GATES], 1, 2)
    rd_row = jnp.concatenate([rows(ret_decay), zeros(1, GATE_LANES - 8)], axis=2)
    rw_vec = jnp.concatenate([rows(rwkv_kk), rows(rwkv_ka), rows(rwkv_rk), rows(rwkv_norm)], axis=1)

    new_states = []
    for l in range(DEPTH):
        lay_of = lambda *arrs: [(a, l) for a in arrs]
        pa, pb, pc, pd, pg, pgt = _proj_call(h_pair, n_rows, mod, l, proj_ws, n_ptiles, tiles_per_dec)
        ys = [[], [], [], []]
        layer_states = None
        for gi, (t_len, n_seq, off) in enumerate(groups):
            ctx = gi == 0
            lay = 0 if ctx else l
            out_layer = l if ctx else None
            prev = new_states[-1] if ctx and new_states else (None,) * 6
            ya, c1, n1, m1 = _mixer_call(
                mlstm_kernel, "mlstm", SMALL_BATCH, t_len, n_seq, off, [pa, pg, pgt],
                lay_of(bias_c, bias_r, rows(mlstm_norm)),
                [zmat, zn, zm] if ctx else [state_mlstm_C, n_cache, m_cache], lay, [MAT, VEC, SCL],
                lambda ns: [_bd_state(ns), pltpu.VMEM((2 * ns, 1, GROUP_W), f32), pltpu.VMEM((2 * ns, 1, GROUP_W), f32)],
                out_layer, prev[0:3] if prev[0] is not None else None)
            yb, sb1 = _mixer_call(
                delta_kernel, "delta", LARGE_BATCH, t_len, n_seq, off, [pb, pg, pgt],
                lay_of(delta_conv, dl_row, dl_col, rows(delta_norm)),
                [zmat if ctx else state_delta], lay, [MAT], lambda ns: [_bd_state(ns)],
                out_layer, prev[3:4] if prev[3] is not None else None)
            ret_consts = lay_of(rd_row, rows(ret_norm))
            if not ctx:
                ret_consts = [(cos_t, None), (sin_t, None)] + ret_consts
            yc, sc1 = _mixer_call(
                functools.partial(ret_kernel, rope=not ctx), "ret", LARGE_BATCH, t_len, n_seq, off, [pc],
                ret_consts, [zmat if ctx else state_ret], lay, [MAT],
                lambda ns: [_bd_state(ns), pltpu.VMEM((2, 4, CHUNK, GROUP_W), f32)],
                out_layer, prev[4:5] if prev[4] is not None else None)
            yd, sd1 = _mixer_call(
                rwkv_kernel, "rwkv", SMALL_BATCH if ctx else LARGE_BATCH, t_len, n_seq, off, [pd],
                lay_of(rows(rwkv_mu), rwkv_w0, rwkv_w2, rwkv_a0, rwkv_a2, rwkv_g2, rw_vec),
                [zmat if ctx else state_rwkv], lay, [MAT], lambda ns: [_bd_state(ns)],
                out_layer, prev[5:6] if prev[5] is not None else None)
            for lst, y in zip(ys, (ya, yb, yc, yd)):
                lst.append(y)
            if ctx:
                layer_states = (c1, n1, m1, sb1, sc1, sd1)
        new_states.append(layer_states)
        last = l == DEPTH - 1
        h_new = _post_call(h_pair, [tuple(y) for y in ys], n_rows, mod, l, post_ws, n_ptiles, tiles_per_dec, last)
        h_pair = tuple(h_new) if last else (h_new, h_new)

    outs = list(new_states[-1])
    outs[1] = outs[1].reshape(outs[1].shape[:4] + (HEAD_DIM,))
    outs[2] = outs[2].reshape(outs[2].shape[:4])
    return (h_pair[0].reshape(batch, seq, d), h_pair[1].reshape(dec_batch, dec_seq, d), *outs)
```
